```python
import math
import jax, jax.numpy as jnp
from jax import lax
import numpy as np

D_MODEL = 1024
BATCH = 8
SEQ = 8192
DEPTH = 2
DEC_BATCH = 4
DEC_SEQ = 8192
PAST_LEN = 128

A_HEADS = 8
A_HEAD_DIM = 64
A_WIDTH = A_HEADS * A_HEAD_DIM
DILATED_PATTERNS = ((128, 1), (512, 4), (2048, 16))
QUERY_BLOCK = 128
ROPE_THETA = 500000.0
ROPE_DIM = A_HEAD_DIM // 4
B_WIDTH = D_MODEL // 2
CONV_WIDTH = 3
HG_HEADS = 8
HG_KEY_DIM = D_MODEL // HG_HEADS
HG_VAL_DIM = D_MODEL // HG_HEADS
HG_WIDTH = HG_HEADS * HG_KEY_DIM
HG_CHUNK = 64
MEM_LEN = 256
XA_HEADS = 4
XA_HEAD_DIM = D_MODEL // XA_HEADS
N_GROUPS = 4
EXPERTS_PER_GROUP = 8
TOP_K_IN_GROUP = 2
EXPERT_FF = D_MODEL // 4
LN_EPS = 1e-5
RMS_EPS = 1e-6
DEEPNORM_ALPHA = (2 * DEPTH) ** 0.25
DEEPNORM_BETA = (8 * DEPTH) ** -0.25
N_EVEN = (DEPTH + 1) // 2
N_ODD = DEPTH // 2

kernel_name = "hybrid_dilated_conv_hgrn2_moe_encoder"

F32 = jnp.float32


def layer_norm(x, g, b):
    xf = x.astype(F32)
    mu = jnp.mean(xf, -1, keepdims=True)
    var = jnp.mean(jnp.square(xf - mu), -1, keepdims=True)
    y = (xf - mu) * lax.rsqrt(var + LN_EPS) * g.astype(F32) + b.astype(F32)
    return y.astype(x.dtype)


def rope_partial(t):
    s = t.shape[1]
    half = ROPE_DIM // 2
    inv_freq = jnp.exp(-math.log(ROPE_THETA) * jnp.arange(half, dtype=F32) * (2.0 / ROPE_DIM))
    ang = jnp.arange(s, dtype=F32)[:, None] * inv_freq[None, :]
    cos = jnp.cos(ang)[None, :, None, :]
    sin = jnp.sin(ang)[None, :, None, :]
    tf = t.astype(F32)
    t1 = tf[..., :half]
    t2 = tf[..., half:ROPE_DIM]
    out = jnp.concatenate([t1 * cos - t2 * sin, t1 * sin + t2 * cos, tf[..., ROPE_DIM:]], -1)
    return out.astype(t.dtype)


def band_attention(q, k, v, radius):
    n, L, h, dh = q.shape
    qb_len = math.gcd(L, QUERY_BLOCK)
    nb = L // qb_len
    kw = qb_len + 2 * radius
    pad = ((0, 0), (radius, radius), (0, 0), (0, 0))
    kp = jnp.pad(k, pad)
    vp = jnp.pad(v, pad)
    idx = (jnp.arange(nb) * qb_len)[:, None] + jnp.arange(kw)[None, :]
    kb = kp[:, idx]
    vb = vp[:, idx]
    qb = q.reshape(n, nb, qb_len, h, dh)
    s = jnp.einsum('nbqhd,nbkhd->nbhqk', qb, kb, preferred_element_type=F32) * (dh ** -0.5)
    rel = jnp.arange(kw)[None, :] - jnp.arange(qb_len)[:, None]
    in_band = (rel >= 0) & (rel <= 2 * radius)
    in_seq = (idx >= radius) & (idx < radius + L)
    mask = in_band[None, :, :] & in_seq[:, None, :]
    s = jnp.where(mask[None, :, None], s, -jnp.inf)
    lse = jax.nn.logsumexp(s, axis=-1)
    p = jnp.exp(s - lse[..., None])
    o = jnp.einsum('nbhqk,nbkhd->nbqhd', p.astype(v.dtype), vb)
    return o.reshape(n, L, h, dh), jnp.transpose(lse, (0, 1, 3, 2)).reshape(n, L, h)


def stride_gather(t, dil):
    b, s, h, dh = t.shape
    return t.reshape(b, s // dil, dil, h, dh).transpose(0, 2, 1, 3, 4).reshape(b * dil, s // dil, h, dh)


def dilated_attention(q, k, v):
    b, s, h, dh = q.shape
    outs, lses = [], []
    for window, dil in DILATED_PATTERNS:
        radius = (window // 2) // dil
        L = s // dil
        o, lse = band_attention(stride_gather(q, dil), stride_gather(k, dil), stride_gather(v, dil), radius)
        outs.append(o.reshape(b, dil, L, h, dh).transpose(0, 2, 1, 3, 4).reshape(b, s, h, dh))
        lses.append(lse.reshape(b, dil, L, h).transpose(0, 2, 1, 3).reshape(b, s, h))
    w = jax.nn.softmax(jnp.stack(lses, 0), axis=0)
    o = jnp.sum(w[..., None] * jnp.stack(outs, 0).astype(F32), 0)
    return o.astype(q.dtype)


def even_mixer(x, w_in, conv_w, w_out):
    b, s, _ = x.shape
    proj = x @ w_in
    cuts = [A_WIDTH, 2 * A_WIDTH, 3 * A_WIDTH, 3 * A_WIDTH + B_WIDTH, 3 * A_WIDTH + 2 * B_WIDTH]
    q, k, v, gate_b, gate_c, hv = jnp.split(proj, cuts, axis=-1)
    hs = (b, s, A_HEADS, A_HEAD_DIM)
    q = rope_partial(q.reshape(hs))
    k = rope_partial(k.reshape(hs))
    a_out = dilated_attention(q, k, v.reshape(hs)).reshape(b, s, A_WIDTH)
    u = gate_c * hv
    half = CONV_WIDTH // 2
    up = jnp.pad(u, ((0, 0), (half, half), (0, 0)))
    conv = sum(up[:, j:j + s] * conv_w[j] for j in range(CONV_WIDTH))
    b_out = gate_b * conv
    return jnp.concatenate([a_out, b_out], -1) @ w_out


def gla_chunk_scan(q, k, v, log_f):
    b, s, h, dk = q.shape
    dv = v.shape[-1]
    c = math.gcd(s, HG_CHUNK)
    nc = s // c

    def chunks(t):
        return t.reshape(b, nc, c, h, t.shape[-1]).transpose(1, 0, 3, 2, 4)

    causal = jnp.arange(c)[:, None] >= jnp.arange(c)[None, :]

    def step(state, inp):
        qc, kc, vc, lfc = inp
        cum = jnp.cumsum(lfc, axis=2)
        diff = cum[:, :, :, None, :] - cum[:, :, None, :, :]
        decay = jnp.exp(jnp.where(causal[:, :, None], diff, -jnp.inf))
        att = jnp.einsum('bhtk,bhsk,bhtsk->bhts', qc, kc, decay)
        o = jnp.einsum('bhts,bhsv->bhtv', att, vc) + jnp.einsum('bhtk,bhkv->bhtv', qc * jnp.exp(cum), state)
        last = cum[:, :, -1:, :]
        state = jnp.exp(last[:, :, 0, :])[..., None] * state + jnp.einsum('bhsk,bhsv->bhkv', kc * jnp.exp(last - cum), vc)
        return state, o

    state0 = jnp.zeros((b, h, dk, dv), F32)
    _, o = lax.scan(step, state0, (chunks(q), chunks(k), chunks(v), chunks(log_f)))
    return o.transpose(1, 0, 3, 2, 4).reshape(b, s, h, dv)


def odd_mixer(x, w_in, lb, norm_g, w_out):
    b, s, _ = x.shape
    proj = x @ w_in
    q, z_fwd, z_bwd, inp, g = jnp.split(proj, 5, axis=-1)
    shp = (b, s, HG_HEADS, HG_KEY_DIM)
    lbf = lb.astype(F32)

    def gates(z):
        f = lbf + (1.0 - lbf) * jax.nn.sigmoid(z.astype(F32))
        return jnp.log(f).reshape(shp), (1.0 - f).reshape(shp)

    lf_f, k_f = gates(z_fwd)
    lf_b, k_b = gates(z_bwd)
    qf = q.astype(F32).reshape(shp)
    vf = inp.astype(F32).reshape(b, s, HG_HEADS, HG_VAL_DIM)
    o_fwd = gla_chunk_scan(qf, k_f, vf, lf_f)
    rev = lambda t: jnp.flip(t, axis=1)
    o_bwd = rev(gla_chunk_scan(rev(qf), rev(k_b), rev(vf), rev(lf_b)))
    o = o_fwd + o_bwd
    o = o * lax.rsqrt(jnp.mean(jnp.square(o), -1, keepdims=True) + RMS_EPS)
    o = o.reshape(b, s, HG_WIDTH) * norm_g.astype(F32)
    o = (o * jax.nn.silu(g.astype(F32))).astype(x.dtype)
    return o @ w_out


def memory_cross_attention(x, mem, w_q, w_kv, w_out):
    b, s, _ = x.shape
    m = mem.shape[1]
    q = (x @ w_q).reshape(b, s, XA_HEADS, XA_HEAD_DIM)
    k, v = jnp.split(mem @ w_kv, 2, axis=-1)
    k = k.reshape(b, m, XA_HEADS, XA_HEAD_DIM)
    v = v.reshape(b, m, XA_HEADS, XA_HEAD_DIM)
    sc = jnp.einsum('bshd,bmhd->bhsm', q, k, preferred_element_type=F32) * (XA_HEAD_DIM ** -0.5)
    p = jax.nn.softmax(sc, axis=-1)
    o = jnp.einsum('bhsm,bmhd->bshd', p.astype(v.dtype), v).reshape(b, s, D_MODEL)
    return o @ w_out


def hierarchical_moe(x, w_group, b_group, w_expert, b_expert, w1, w3, w2):
    b, s, d = x.shape
    xf = x.reshape(b * s, d)
    g_logits = (xf @ w_group).astype(F32) + b_group.astype(F32)
    g_prob = jax.nn.softmax(g_logits, axis=-1)
    g_sel = jnp.argmax(g_logits, axis=-1)
    g_onehot = jax.nn.one_hot(g_sel, N_GROUPS, dtype=F32)
    g_w = jnp.sum(g_prob * g_onehot, -1)
    e_logits = ((xf @ w_expert).astype(F32) + b_expert.astype(F32)).reshape(-1, N_GROUPS, EXPERTS_PER_GROUP)
    e_logits = jnp.sum(e_logits * g_onehot[:, :, None], axis=1)
    top_v, top_i = lax.top_k(e_logits, TOP_K_IN_GROUP)
    top_w = jax.nn.softmax(top_v, axis=-1) * g_w[:, None]
    e_gate = jnp.sum(jax.nn.one_hot(top_i, EXPERTS_PER_GROUP, dtype=F32) * top_w[..., None], axis=1)
    gate = (g_onehot[:, :, None] * e_gate[:, None, :]).astype(x.dtype)
    y = jnp.zeros_like(xf)
    for grp in range(N_GROUPS):
        h1 = jnp.einsum('nd,edf->nef', xf, w1[grp])
        h3 = jnp.einsum('nd,edf->nef', xf, w3[grp])
        hid = jax.nn.silu(h1) * h3 * gate[:, grp, :, None]
        y = y + jnp.einsum('nef,efd->nd', hid, w2[grp])
    return y.reshape(b, s, d)


def encoder_trunk(x, mem, ev_w_in, ev_conv_w, ev_w_out, od_w_in, lb_logits, od_norm_g, od_w_out,
                  xa_w_q, xa_w_kv, xa_w_out, moe_w_group, moe_b_group, moe_w_expert, moe_b_expert,
                  moe_w1, moe_w3, moe_w2, ln_g, ln_b):
    p_layers = jax.nn.softmax(lb_logits.astype(F32), axis=0)
    lower_bounds = jnp.cumsum(p_layers, axis=0) - p_layers
    for layer in range(DEPTH):
        j = layer // 2
        if layer % 2 == 0:
            mix = even_mixer(x, ev_w_in[j], ev_conv_w[j], ev_w_out[j])
        else:
            mix = odd_mixer(x, od_w_in[j], lower_bounds[layer], od_norm_g[j], od_w_out[j])
        x = layer_norm(DEEPNORM_ALPHA * x + mix, ln_g[layer, 0], ln_b[layer, 0])
        xa = memory_cross_attention(x, mem, xa_w_q[layer], xa_w_kv[layer], xa_w_out[layer])
        x = layer_norm(DEEPNORM_ALPHA * x + xa, ln_g[layer, 1], ln_b[layer, 1])
        ff = hierarchical_moe(x, moe_w_group[layer], moe_b_group[layer], moe_w_expert[layer], moe_b_expert[layer],
                              moe_w1[layer], moe_w3[layer], moe_w2[layer])
        x = layer_norm(DEEPNORM_ALPHA * x + ff, ln_g[layer, 2], ln_b[layer, 2])
    return x


def setup_inputs(seed: int = 0) -> dict:
    key = jax.random.key(seed)
    ks = jax.random.split(key, 24)

    def nrm(k, shape, scale=1.0):
        return jax.random.normal(k, shape, F32) * scale

    def w(k, shape, fan_in, scale=1.0):
        return jax.random.normal(k, shape, F32) * (scale * fan_in ** -0.5)

    G, E, F = N_GROUPS, EXPERTS_PER_GROUP, EXPERT_FF
    return {
        "x_prompt": nrm(ks[0], (BATCH, SEQ, D_MODEL)),
        "x_sample": nrm(ks[1], (DEC_BATCH, DEC_SEQ, D_MODEL)),
        "mem_prompt": nrm(ks[2], (BATCH, MEM_LEN, D_MODEL)),
        "mem_sample": nrm(ks[3], (DEC_BATCH, MEM_LEN, D_MODEL)),
        "ev_w_in": w(ks[4], (N_EVEN, D_MODEL, 3 * A_WIDTH + 3 * B_WIDTH), D_MODEL),
        "ev_conv_w": w(ks[5], (N_EVEN, CONV_WIDTH, B_WIDTH), CONV_WIDTH),
        "ev_w_out": w(ks[6], (N_EVEN, A_WIDTH + B_WIDTH, D_MODEL), A_WIDTH + B_WIDTH, DEEPNORM_BETA),
        "od_w_in": w(ks[7], (N_ODD, D_MODEL, 5 * HG_WIDTH), D_MODEL),
        "lb_logits": nrm(ks[8], (DEPTH, HG_WIDTH), 0.5),
        "od_norm_g": 1.0 + nrm(ks[9], (N_ODD, HG_WIDTH), 0.02),
        "od_w_out": w(ks[10], (N_ODD, HG_WIDTH, D_MODEL), HG_WIDTH, DEEPNORM_BETA),
        "xa_w_q": w(ks[11], (DEPTH, D_MODEL, D_MODEL), D_MODEL),
        "xa_w_kv": w(ks[12], (DEPTH, D_MODEL, 2 * D_MODEL), D_MODEL),
        "xa_w_out": w(ks[13], (DEPTH, D_MODEL, D_MODEL), D_MODEL, DEEPNORM_BETA),
        "moe_w_group": w(ks[14], (DEPTH, D_MODEL, G), D_MODEL),
        "moe_b_group": nrm(ks[15], (DEPTH, G), 0.01),
        "moe_w_expert": w(ks[16], (DEPTH, D_MODEL, G * E), D_MODEL),
        "moe_b_expert": nrm(ks[17], (DEPTH, G * E), 0.01),
        "moe_w1": w(ks[18], (DEPTH, G, E, D_MODEL, F), D_MODEL),
        "moe_w3": w(ks[19], (DEPTH, G, E, D_MODEL, F), D_MODEL),
        "moe_w2": w(ks[20], (DEPTH, G, E, F, D_MODEL), F, DEEPNORM_BETA),
        "ln_g": 1.0 + nrm(ks[21], (DEPTH, 3, D_MODEL), 0.02),
        "ln_b": nrm(ks[22], (DEPTH, 3, D_MODEL), 0.02),
    }


def reference(x_prompt, x_sample, mem_prompt, mem_sample, ev_w_in, ev_conv_w, ev_w_out, od_w_in, lb_logits,
              od_norm_g, od_w_out, xa_w_q, xa_w_kv, xa_w_out, moe_w_group, moe_b_group, moe_w_expert,
              moe_b_expert, moe_w1, moe_w3, moe_w2, ln_g, ln_b):
    y_prompt = encoder_trunk(x_prompt, mem_prompt, ev_w_in, ev_conv_w, ev_w_out, od_w_in, lb_logits, od_norm_g,
                             od_w_out, xa_w_q, xa_w_kv, xa_w_out, moe_w_group, moe_b_group, moe_w_expert,
                             moe_b_expert, moe_w1, moe_w3, moe_w2, ln_g, ln_b)
    y_sample = encoder_trunk(x_sample, mem_sample, ev_w_in, ev_conv_w, ev_w_out, od_w_in, lb_logits, od_norm_g,
                             od_w_out, xa_w_q, xa_w_kv, xa_w_out, moe_w_group, moe_b_group, moe_w_expert,
                             moe_b_expert, moe_w1, moe_w3, moe_w2, ln_g, ln_b)
    return (y_prompt, y_sample)
```

```python
import functools
import math

import jax
import jax.numpy as jnp
from jax import lax
from jax.experimental import pallas as pl
from jax.experimental.pallas import tpu as pltpu

F32 = jnp.float32
BF16 = jnp.bfloat16

D_MODEL = 1024
DEPTH = 2
A_HEADS = 8
A_HEAD_DIM = 64
A_WIDTH = A_HEADS * A_HEAD_DIM
DILATED_PATTERNS = ((128, 1), (512, 4), (2048, 16))
ROPE_THETA = 500000.0
ROPE_DIM = A_HEAD_DIM // 4
B_WIDTH = D_MODEL // 2
CONV_WIDTH = 3
HG_HEADS = 8
HG_DIM = D_MODEL // HG_HEADS
XA_HEADS = 4
XA_HEAD_DIM = D_MODEL // XA_HEADS
N_GROUPS = 4
EXPERTS_PER_GROUP = 8
N_EXPERTS = N_GROUPS * EXPERTS_PER_GROUP
EXPERT_FF = D_MODEL // 4
LN_EPS = 1e-5
RMS_EPS = 1e-6
DEEPNORM_ALPHA = (2 * DEPTH) ** 0.25

LANES = 128
BAND_RADIUS = 64
ATTN_QBLOCK = 128
NEG_BIG = -1e30
GLA_CHUNK = 64
GLA_SUB = 16
GLA_EXP_CLAMP = 70.0
VMEM_LIMIT = 56 * 1024 * 1024


def _cparams(*sem):
    return pltpu.CompilerParams(dimension_semantics=sem, vmem_limit_bytes=VMEM_LIMIT)


def _layer_norm(y, g, b):
    mu = jnp.mean(y, axis=-1, keepdims=True)
    d = y - mu
    var = jnp.mean(d * d, axis=-1, keepdims=True)
    return d * lax.rsqrt(var + LN_EPS) * g + b


def _sigmoid(z):
    return 1.0 / (1.0 + jnp.exp(-z))


def _mm_kernel(x_ref, w_ref, o_ref, *, chunk):
    xb = x_ref[...].astype(BF16)
    for c in range(w_ref.shape[1] // chunk):
        cols = slice(c * chunk, (c + 1) * chunk)
        o_ref[:, cols] = jnp.dot(xb, w_ref[:, cols], preferred_element_type=F32).astype(o_ref.dtype)


def _matmul(x, w, out_dtype, tm, chunk):
    n, k = x.shape
    m = w.shape[1]
    return pl.pallas_call(
        functools.partial(_mm_kernel, chunk=chunk),
        grid=(n // tm,),
        in_specs=[pl.BlockSpec((tm, k), lambda i: (i, 0)), pl.BlockSpec((k, m), lambda i: (0, 0))],
        out_specs=pl.BlockSpec((tm, m), lambda i: (i, 0)),
        out_shape=jax.ShapeDtypeStruct((n, m), out_dtype),
        compiler_params=_cparams("parallel"),
        name="matmul",
    )(x, w)


def _rope_tables(seq):
    half = ROPE_DIM // 2
    inv_freq = jnp.exp(-math.log(ROPE_THETA) * jnp.arange(half, dtype=F32) * (2.0 / ROPE_DIM))
    ang = jnp.arange(seq, dtype=F32)[:, None] * inv_freq[None, :]
    cos, sin = jnp.cos(ang), jnp.sin(ang)
    ones = jnp.ones((seq, A_HEAD_DIM - ROPE_DIM), F32)
    zeros = jnp.zeros((seq, A_HEAD_DIM - ROPE_DIM), F32)
    zh = jnp.zeros((seq, half), F32)
    c = jnp.concatenate([cos, cos, ones], -1)
    s_up = jnp.concatenate([-sin, zh, zeros], -1)
    s_dn = jnp.concatenate([zh, sin, zeros], -1)
    rep = LANES // A_HEAD_DIM
    return tuple(jnp.tile(t, (1, rep)) for t in (c, s_up, s_dn))


def _even_proj_kernel(x_ref, w_ref, c_ref, su_ref, sd_ref, qkv_ref, gu_ref):
    xb = x_ref[...].astype(BF16)
    rep = A_WIDTH // LANES
    half = ROPE_DIM // 2
    c = jnp.tile(c_ref[...], (1, rep))
    su = jnp.tile(su_ref[...], (1, rep))
    sd = jnp.tile(sd_ref[...], (1, rep))

    def proj(j):
        return jnp.dot(xb, w_ref[:, j * A_WIDTH:(j + 1) * A_WIDTH], preferred_element_type=F32)

    def rope(t):
        up = pltpu.roll(t, A_WIDTH - half, axis=1)
        dn = pltpu.roll(t, half, axis=1)
        return t * c + up * su + dn * sd

    qkv_ref[:, 0:A_WIDTH] = (rope(proj(0)) * (A_HEAD_DIM ** -0.5)).astype(BF16)
    qkv_ref[:, A_WIDTH:2 * A_WIDTH] = rope(proj(1)).astype(BF16)
    qkv_ref[:, 2 * A_WIDTH:3 * A_WIDTH] = proj(2).astype(BF16)
    gu_ref[:, 0:B_WIDTH] = proj(3).astype(BF16)
    gu_ref[:, B_WIDTH:2 * B_WIDTH] = (proj(4) * proj(5)).astype(BF16)


def _even_proj(x, w_in, tables, seq, tm):
    n = x.shape[0]
    spt = seq // tm
    tab_spec = pl.BlockSpec((tm, LANES), lambda i: (i % spt, 0))
    return pl.pallas_call(
        _even_proj_kernel,
        grid=(n // tm,),
        in_specs=[pl.BlockSpec((tm, D_MODEL), lambda i: (i, 0)),
                  pl.BlockSpec(w_in.shape, lambda i: (0, 0)),
                  tab_spec, tab_spec, tab_spec],
        out_specs=[pl.BlockSpec((tm, 3 * A_WIDTH), lambda i: (i, 0)),
                   pl.BlockSpec((tm, 2 * B_WIDTH), lambda i: (i, 0))],
        out_shape=[jax.ShapeDtypeStruct((n, 3 * A_WIDTH), BF16),
                   jax.ShapeDtypeStruct((n, 2 * B_WIDTH), BF16)],
        compiler_params=_cparams("parallel"),
        name="even_proj",
    )(x, w_in, *tables)


def _band_attn_kernel(q_ref, kp_ref, km_ref, kn_ref, vp_ref, vm_ref, vn_ref, o_ref, lse_ref,
                      kbuf, vbuf, *, tq, length):
    i = pl.program_id(2)
    r = BAND_RADIUS
    kbuf[0:r] = kp_ref[0]
    kbuf[r:r + tq] = km_ref[0]
    kbuf[r + tq:r + tq + r] = kn_ref[0]
    vbuf[0:r] = vp_ref[0]
    vbuf[r:r + tq] = vm_ref[0]
    vbuf[r + tq:r + tq + r] = vn_ref[0]

    qb = ATTN_QBLOCK
    kw = qb + 2 * r
    qi = lax.broadcasted_iota(jnp.int32, (qb, kw), 0)
    kj = lax.broadcasted_iota(jnp.int32, (qb, kw), 1)
    rel = kj - qi
    band = (rel >= 0) & (rel <= 2 * r)
    lane = lax.broadcasted_iota(jnp.int32, (qb, LANES), 1)
    low = lane < A_HEAD_DIM
    nt = (((1,), (1,)), ((), ()))

    for s in range(tq // qb):
        kpos = i * tq + (s * qb - r) + kj
        valid = band & (kpos >= 0) & (kpos < length)
        bias = jnp.where(valid, 0.0, NEG_BIG)
        rows = slice(s * qb, (s + 1) * qb)
        wrows = slice(s * qb, s * qb + kw)
        for p in range(A_WIDTH // LANES):
            cols = slice(p * LANES, (p + 1) * LANES)
            qp = q_ref[0, rows, cols]
            kwin = kbuf[wrows, cols]
            vwin = vbuf[wrows, cols]
            outs, lses = [], []
            for sel in (low, jnp.logical_not(low)):
                qm = jnp.where(sel, qp, jnp.zeros_like(qp))
                sc = lax.dot_general(qm, kwin, nt, preferred_element_type=F32) + bias
                m = jnp.max(sc, axis=-1, keepdims=True)
                pe = jnp.exp(sc - m)
                l = jnp.sum(pe, axis=-1, keepdims=True)
                pv = jnp.dot(pe.astype(BF16), vwin, preferred_element_type=F32)
                outs.append(pv / l)
                lses.append(jnp.broadcast_to(m + jnp.log(l), (qb, LANES)))
            o_ref[0, rows, cols] = jnp.where(low, outs[0], outs[1]).astype(o_ref.dtype)
            lse_ref[0, rows, cols] = jnp.where(low, lses[0], lses[1])


def _band_attention(qkv, batch, seq, dil):
    length = seq // dil
    tq = min(512, length)
    r = BAND_RADIUS
    view = qkv.reshape(batch, length, dil * 3 * A_WIDTH)
    nblk_h = length // r
    per = tq // r

    def main(j):
        return pl.BlockSpec((1, tq, A_WIDTH), lambda b, rr, i: (b, i, rr * 3 + j))

    def prev(j):
        return pl.BlockSpec((1, r, A_WIDTH), lambda b, rr, i: (b, jnp.maximum(i * per - 1, 0), rr * 3 + j))

    def nxt(j):
        return pl.BlockSpec((1, r, A_WIDTH),
                            lambda b, rr, i: (b, jnp.minimum((i + 1) * per, nblk_h - 1), rr * 3 + j))

    out_spec = pl.BlockSpec((1, tq, A_WIDTH), lambda b, rr, i: (b, i, rr))
    o, lse = pl.pallas_call(
        functools.partial(_band_attn_kernel, tq=tq, length=length),
        grid=(batch, dil, length // tq),
        in_specs=[main(0), prev(1), main(1), nxt(1), prev(2), main(2), nxt(2)],
        out_specs=[out_spec, out_spec],
        out_shape=[jax.ShapeDtypeStruct((batch, length, dil * A_WIDTH), BF16),
                   jax.ShapeDtypeStruct((batch, length, dil * A_WIDTH), F32)],
        scratch_shapes=[pltpu.VMEM((tq + 2 * r, A_WIDTH), BF16), pltpu.VMEM((tq + 2 * r, A_WIDTH), BF16)],
        compiler_params=_cparams("parallel", "parallel", "parallel"),
        name=f"band_attn_d{dil}",
    )(view, view, view, view, view, view, view)
    return o.reshape(batch * seq, A_WIDTH), lse.reshape(batch * seq, A_WIDTH)


def _even_out_kernel(o1, o2, o3, l1, l2, l3, gu_ref, up_ref, un_ref, x_ref, w_ref, cw_ref, g_ref, b_ref,
                     y_ref, *, tm, seq):
    i = pl.program_id(0)
    pos = (i * tm) % seq
    la, lb, lc = l1[...], l2[...], l3[...]
    mx = jnp.maximum(jnp.maximum(la, lb), lc)
    ea, eb, ec = jnp.exp(la - mx), jnp.exp(lb - mx), jnp.exp(lc - mx)
    num = ea * o1[...].astype(F32) + eb * o2[...].astype(F32) + ec * o3[...].astype(F32)
    a_out = num / (ea + eb + ec)

    gate_b = gu_ref[:, 0:B_WIDTH].astype(F32)
    u = gu_ref[:, B_WIDTH:2 * B_WIDTH].astype(F32)
    hrows = up_ref.shape[0]
    u_before = jnp.where(pos > 0, up_ref[hrows - 1:hrows, :].astype(F32), 0.0)
    u_after = jnp.where(pos + tm < seq, un_ref[0:1, :].astype(F32), 0.0)
    row = lax.broadcasted_iota(jnp.int32, (tm, B_WIDTH), 0)
    u_prev = jnp.where(row == 0, u_before, pltpu.roll(u, 1, axis=0))
    u_next = jnp.where(row == tm - 1, u_after, pltpu.roll(u, tm - 1, axis=0))
    conv = u_prev * cw_ref[0:1, :] + u * cw_ref[1:2, :] + u_next * cw_ref[2:3, :]
    b_out = gate_b * conv

    mix = jnp.dot(a_out.astype(BF16), w_ref[0:A_WIDTH, :], preferred_element_type=F32)
    mix = mix + jnp.dot(b_out.astype(BF16), w_ref[A_WIDTH:A_WIDTH + B_WIDTH, :], preferred_element_type=F32)
    y_ref[...] = _layer_norm(DEEPNORM_ALPHA * x_ref[...] + mix, g_ref[...], b_ref[...])


def _even_out(outs, lses, gu, x, w_out, conv_w, g, b, seq, tm):
    n = x.shape[0]
    hrows = 16
    nh = n // hrows
    per = tm // hrows
    row_a = pl.BlockSpec((tm, A_WIDTH), lambda i: (i, 0))
    full = lambda a: pl.BlockSpec(a.shape, lambda i: (0, 0))
    return pl.pallas_call(
        functools.partial(_even_out_kernel, tm=tm, seq=seq),
        grid=(n // tm,),
        in_specs=[row_a] * 6 + [
            pl.BlockSpec((tm, 2 * B_WIDTH), lambda i: (i, 0)),
            pl.BlockSpec((hrows, B_WIDTH), lambda i: (jnp.maximum(i * per - 1, 0), 1)),
            pl.BlockSpec((hrows, B_WIDTH), lambda i: (jnp.minimum((i + 1) * per, nh - 1), 1)),
            pl.BlockSpec((tm, D_MODEL), lambda i: (i, 0)),
            full(w_out), full(conv_w), full(g), full(b)],
        out_specs=pl.BlockSpec((tm, D_MODEL), lambda i: (i, 0)),
        out_shape=jax.ShapeDtypeStruct((n, D_MODEL), F32),
        compiler_params=_cparams("parallel"),
        name="even_out",
    )(*outs, *lses, gu, gu, gu, x, w_out, conv_w, g, b)


def _split3(a):
    hi = a.astype(BF16)
    r1 = a - hi.astype(F32)
    mid = r1.astype(BF16)
    lo = (r1 - mid.astype(F32)).astype(BF16)
    return jnp.concatenate([hi, mid, lo], axis=1)


def _gla_chunk(q, z, v, lb, tri, state_t, *, reverse):
    c = GLA_CHUNK
    sb = GLA_SUB
    dk = q.shape[1]
    nt = (((1,), (1,)), ((), ()))
    f = lb + (1.0 - lb) * _sigmoid(z)
    lf = jnp.log(f)
    kk = 1.0 - f
    cum3 = jnp.dot(tri, _split3(lf), preferred_element_type=F32)
    cum = cum3[:, 0:dk] + cum3[:, dk:2 * dk] + cum3[:, 2 * dk:3 * dk]

    ti = lax.broadcasted_iota(jnp.int32, (c, c), 0)
    si = lax.broadcasted_iota(jnp.int32, (c, c), 1)
    causal = (si >= ti) if reverse else (ti >= si)

    parts = []
    for blk in range(c // sb):
        rows = slice(blk * sb, (blk + 1) * sb)
        if reverse:
            edge = (blk + 1) * sb
            ref = cum[edge:edge + 1, :] if edge < c else jnp.zeros((1, dk), F32)
        else:
            edge = blk * sb - 1
            ref = cum[edge:edge + 1, :] if edge >= 0 else jnp.zeros((1, dk), F32)
        qs = q[rows] * jnp.exp(cum[rows] - ref)
        ks = kk * jnp.exp(jnp.minimum(ref - cum, GLA_EXP_CLAMP))
        parts.append(lax.dot_general(qs.astype(BF16), ks.astype(BF16), nt, preferred_element_type=F32))
    att = jnp.where(causal, jnp.concatenate(parts, axis=0), 0.0)

    total = cum[0:1, :] if reverse else cum[c - 1:c, :]
    o = jnp.dot(att.astype(BF16), v, preferred_element_type=F32)
    o = o + lax.dot_general((q * jnp.exp(cum)).astype(BF16), state_t.astype(BF16), nt,
                            preferred_element_type=F32)
    kd = (kk * jnp.exp(total - cum)).astype(BF16)
    upd = lax.dot_general(v, kd, (((0,), (0,)), ((), ())), preferred_element_type=F32)
    return o, state_t * jnp.exp(total) + upd


def _gla_kernel(q_ref, zf_ref, zb_ref, v_ref, g_ref, lbl_ref, ng_ref, o_ref, accf, accb, *, seq):
    c = GLA_CHUNK
    nc = seq // c
    dk = HG_DIM
    l0 = lbl_ref[0:1, :]
    l1 = lbl_ref[1:2, :]
    mx = jnp.maximum(l0, l1)
    e0, e1 = jnp.exp(l0 - mx), jnp.exp(l1 - mx)
    lb = e0 / (e0 + e1)

    ti = lax.broadcasted_iota(jnp.int32, (c, c), 0)
    si = lax.broadcasted_iota(jnp.int32, (c, c), 1)
    tri_f = jnp.where(ti >= si, 1.0, 0.0).astype(BF16)
    tri_b = jnp.where(si >= ti, 1.0, 0.0).astype(BF16)

    def body(j, carry):
        sf, sbk = carry
        rf = pl.ds(pl.multiple_of(j * c, c), c)
        rb = pl.ds(pl.multiple_of((nc - 1 - j) * c, c), c)
        of, sf = _gla_chunk(q_ref[0, rf, :].astype(F32), zf_ref[0, rf, :].astype(F32), v_ref[0, rf, :],
                            lb, tri_f, sf, reverse=False)
        ob, sbk = _gla_chunk(q_ref[0, rb, :].astype(F32), zb_ref[0, rb, :].astype(F32), v_ref[0, rb, :],
                             lb, tri_b, sbk, reverse=True)
        accf[rf, :] = of
        accb[rb, :] = ob
        return sf, sbk

    zero = jnp.zeros((dk, dk), F32)
    lax.fori_loop(0, nc, body, (zero, zero))

    blk = 512
    ng = ng_ref[...]

    def fin(j, _):
        rows = pl.ds(pl.multiple_of(j * blk, blk), blk)
        o = accf[rows, :] + accb[rows, :]
        o = o * lax.rsqrt(jnp.mean(o * o, axis=-1, keepdims=True) + RMS_EPS) * ng
        g = g_ref[0, rows, :].astype(F32)
        o_ref[0, rows, :] = (o * (g * _sigmoid(g))).astype(o_ref.dtype)
        return 0

    lax.fori_loop(0, seq // blk, fin, 0)


def _gla(proj, lb_logits, norm_g, batch, seq):
    view = proj.reshape(batch, seq, 5 * D_MODEL)

    def col(seg):
        return pl.BlockSpec((1, seq, HG_DIM), lambda b, h: (b, 0, seg * HG_HEADS + h))

    return pl.pallas_call(
        functools.partial(_gla_kernel, seq=seq),
        grid=(batch, HG_HEADS),
        in_specs=[col(0), col(1), col(2), col(3), col(4),
                  pl.BlockSpec((DEPTH, HG_DIM), lambda b, h: (0, h)),
                  pl.BlockSpec((1, HG_DIM), lambda b, h: (0, h))],
        out_specs=pl.BlockSpec((1, seq, HG_DIM), lambda b, h: (b, 0, h)),
        out_shape=jax.ShapeDtypeStruct((batch, seq, D_MODEL), BF16),
        scratch_shapes=[pltpu.VMEM((seq, HG_DIM), F32), pltpu.VMEM((seq, HG_DIM), F32)],
        compiler_params=_cparams("parallel", "parallel"),
        name="gla",
    )(view, view, view, view, view, lb_logits, norm_g).reshape(batch * seq, D_MODEL)


def _proj_ln_kernel(a_ref, w_ref, x_ref, g_ref, b_ref, y_ref):
    mix = jnp.dot(a_ref[...], w_ref[...], preferred_element_type=F32)
    y_ref[...] = _layer_norm(DEEPNORM_ALPHA * x_ref[...] + mix, g_ref[...], b_ref[...])


def _proj_ln(a, w, x, g, b, tm):
    n = x.shape[0]
    full = lambda t: pl.BlockSpec(t.shape, lambda i: (0, 0))
    row = pl.BlockSpec((tm, D_MODEL), lambda i: (i, 0))
    return pl.pallas_call(
        _proj_ln_kernel,
        grid=(n // tm,),
        in_specs=[row, full(w), row, full(g), full(b)],
        out_specs=row,
        out_shape=jax.ShapeDtypeStruct((n, D_MODEL), F32),
        compiler_params=_cparams("parallel"),
        name="proj_ln",
    )(a, w, x, g, b)


ROUTER_ROWS = 64
ROUTE_OUT_ROWS = 8


def _first_index_of(vals, target, n_rows):
    idx = lax.broadcasted_iota(jnp.int32, vals.shape, 0)
    return jnp.min(jnp.where(vals == target, idx, n_rows), axis=0, keepdims=True)


def _xattn_kernel(x_ref, wq_ref, kv_ref, wo_ref, g_ref, b_ref, wr_ref, br_ref, y_ref, route_ref):
    x = x_ref[...]
    q = jnp.dot(x.astype(BF16), wq_ref[...], preferred_element_type=F32) * (XA_HEAD_DIM ** -0.5)
    qb = q.astype(BF16)
    nt = (((1,), (1,)), ((), ()))
    heads = []
    for h in range(XA_HEADS):
        cols = slice(h * XA_HEAD_DIM, (h + 1) * XA_HEAD_DIM)
        k = kv_ref[0, :, cols]
        v = kv_ref[0, :, D_MODEL + h * XA_HEAD_DIM:D_MODEL + (h + 1) * XA_HEAD_DIM]
        sc = lax.dot_general(qb[:, cols], k, nt, preferred_element_type=F32)
        m = jnp.max(sc, axis=-1, keepdims=True)
        pe = jnp.exp(sc - m)
        p = pe / jnp.sum(pe, axis=-1, keepdims=True)
        heads.append(jnp.dot(p.astype(BF16), v, preferred_element_type=F32).astype(BF16))
    o = jnp.concatenate(heads, axis=1)
    xa = jnp.dot(o, wo_ref[...], preferred_element_type=F32)
    y = _layer_norm(DEEPNORM_ALPHA * x + xa, g_ref[...], b_ref[...])
    y_ref[...] = y

    lg = lax.dot_general(wr_ref[...], y.astype(BF16), nt, preferred_element_type=F32) + br_ref[:, 0:1]
    gl = lg[0:N_GROUPS, :]
    gmax = jnp.max(gl, axis=0, keepdims=True)
    g_w = 1.0 / jnp.sum(jnp.exp(gl - gmax), axis=0, keepdims=True)
    g_sel = _first_index_of(gl, gmax, N_GROUPS)
    el = jnp.zeros((EXPERTS_PER_GROUP, gl.shape[1]), F32)
    for grp in range(N_GROUPS):
        rows = slice(8 + grp * EXPERTS_PER_GROUP, 8 + (grp + 1) * EXPERTS_PER_GROUP)
        el = el + jnp.where(g_sel == grp, lg[rows, :], 0.0)
    m1 = jnp.max(el, axis=0, keepdims=True)
    i1 = _first_index_of(el, m1, EXPERTS_PER_GROUP)
    eidx = lax.broadcasted_iota(jnp.int32, el.shape, 0)
    el2 = jnp.where(eidx == i1, -jnp.inf, el)
    m2 = jnp.max(el2, axis=0, keepdims=True)
    i2 = _first_index_of(el2, m2, EXPERTS_PER_GROUP)
    e2 = jnp.exp(m2 - m1)
    den = 1.0 + e2
    w1 = g_w / den
    w2 = g_w * e2 / den
    base = g_sel * EXPERTS_PER_GROUP
    zero = jnp.zeros_like(w1)
    route_ref[...] = jnp.concatenate(
        [(base + i1).astype(F32), (base + i2).astype(F32), w1, w2, zero, zero, zero, zero], axis=0)


def _xattn(x, kv, wq, wo, g, b, wr, br, seq, tm):
    n = x.shape[0]
    spt = seq // tm
    full = lambda t: pl.BlockSpec(t.shape, lambda i: (0, 0))
    row = pl.BlockSpec((tm, D_MODEL), lambda i: (i, 0))
    return pl.pallas_call(
        _xattn_kernel,
        grid=(n // tm,),
        in_specs=[row, full(wq),
                  pl.BlockSpec((1,) + kv.shape[1:], lambda i: (i // spt, 0, 0)),
                  full(wo), full(g), full(b), full(wr), full(br)],
        out_specs=[row, pl.BlockSpec((ROUTE_OUT_ROWS, tm), lambda i: (0, i))],
        out_shape=[jax.ShapeDtypeStruct((n, D_MODEL), F32),
                   jax.ShapeDtypeStruct((ROUTE_OUT_ROWS, n), F32)],
        compiler_params=_cparams("parallel"),
        name="xattn_router",
    )(x, wq, kv, wo, g, b, wr, br)


def _expert_kernel(te_ref, nu_ref, xs_ref, gate_ref, w13_ref, w2_ref, ys_ref):
    j = pl.program_id(0)

    @pl.when(j < nu_ref[0])
    def _():
        h = jnp.dot(xs_ref[...], w13_ref[0], preferred_element_type=F32)
        h1 = h[:, 0:EXPERT_FF]
        h3 = h[:, EXPERT_FF:2 * EXPERT_FF]
        hid = h1 * _sigmoid(h1) * h3 * gate_ref[...]
        ys_ref[...] = jnp.dot(hid.astype(BF16), w2_ref[0], preferred_element_type=F32).astype(ys_ref.dtype)

    @pl.when(j >= nu_ref[0])
    def _():
        ys_ref[...] = jnp.zeros_like(ys_ref)


def _experts(xs, gate, tile_expert, n_used, w13, w2, tm):
    mp = xs.shape[0]
    grid_spec = pltpu.PrefetchScalarGridSpec(
        num_scalar_prefetch=2,
        grid=(mp // tm,),
        in_specs=[pl.BlockSpec((tm, D_MODEL), lambda j, te, nu: (j, 0)),
                  pl.BlockSpec((tm, 1), lambda j, te, nu: (j, 0)),
                  pl.BlockSpec((1, D_MODEL, 2 * EXPERT_FF), lambda j, te, nu: (te[j], 0, 0)),
                  pl.BlockSpec((1, EXPERT_FF, D_MODEL), lambda j, te, nu: (te[j], 0, 0))],
        out_specs=pl.BlockSpec((tm, D_MODEL), lambda j, te, nu: (j, 0)),
    )
    return pl.pallas_call(
        _expert_kernel,
        grid_spec=grid_spec,
        out_shape=jax.ShapeDtypeStruct((mp, D_MODEL), BF16),
        compiler_params=_cparams("arbitrary"),
        name="experts",
    )(tile_expert, n_used, xs, gate, w13, w2)


def _moe_ln_kernel(x_ref, y0_ref, y1_ref, g_ref, b_ref, o_ref):
    ff = y0_ref[...].astype(F32) + y1_ref[...].astype(F32)
    o_ref[...] = _layer_norm(DEEPNORM_ALPHA * x_ref[...] + ff, g_ref[...], b_ref[...])


def _moe_ln(x, y0, y1, g, b, tm):
    n = x.shape[0]
    full = lambda t: pl.BlockSpec(t.shape, lambda i: (0, 0))
    row = pl.BlockSpec((tm, D_MODEL), lambda i: (i, 0))
    return pl.pallas_call(
        _moe_ln_kernel,
        grid=(n // tm,),
        in_specs=[row, row, row, full(g), full(b)],
        out_specs=row,
        out_shape=jax.ShapeDtypeStruct((n, D_MODEL), F32),
        compiler_params=_cparams("parallel"),
        name="moe_ln",
    )(x, y0, y1, g, b)


def _moe(x, route, w13, w2, g, b, tm_e, tm):
    n = x.shape[0]
    ids = route[0:2].astype(jnp.int32).reshape(2 * n)
    gates = route[2:4].reshape(2 * n)
    onehot = (ids[:, None] == jnp.arange(N_EXPERTS, dtype=jnp.int32)[None, :]).astype(jnp.int32)
    counts = jnp.sum(onehot, axis=0)
    padded = ((counts + tm_e - 1) // tm_e) * tm_e
    ends = jnp.cumsum(padded)
    starts = ends - padded
    rank = jnp.sum((jnp.cumsum(onehot, axis=0) - 1) * onehot, axis=1)
    slot = starts[ids] + rank
    mp = 2 * n + N_EXPERTS * tm_e
    tok = jnp.tile(jnp.arange(n, dtype=jnp.int32), 2)
    tok_of_slot = jnp.zeros((mp,), jnp.int32).at[slot].set(tok)
    gate_of_slot = jnp.zeros((mp,), F32).at[slot].set(gates)
    tile_start = jnp.arange(mp // tm_e, dtype=jnp.int32) * tm_e
    tile_expert = jnp.minimum(jnp.searchsorted(ends, tile_start, side="right"), N_EXPERTS - 1).astype(jnp.int32)
    n_used = (ends[-1] // tm_e).astype(jnp.int32).reshape(1)
    xs = x.astype(BF16)[tok_of_slot]
    ys = _experts(xs, gate_of_slot[:, None], tile_expert, n_used, w13, w2, tm_e)
    y0 = ys[slot[:n]]
    y1 = ys[slot[n:]]
    return _moe_ln(x, y0, y1, g, b, tm)


def _router_weights(w_group, b_group, w_expert, b_expert):
    wr = jnp.zeros((ROUTER_ROWS, D_MODEL), F32)
    wr = wr.at[0:N_GROUPS].set(w_group.T).at[8:8 + N_EXPERTS].set(w_expert.T)
    br = jnp.zeros((ROUTER_ROWS, LANES), F32)
    br = br.at[0:N_GROUPS, :].set(b_group[:, None]).at[8:8 + N_EXPERTS, :].set(b_expert[:, None])
    return wr.astype(BF16), br


def _trunk(x3, mem3, p):
    batch, seq, _ = x3.shape
    n = batch * seq
    x = x3.reshape(n, D_MODEL)
    mem = mem3.reshape(batch * mem3.shape[1], D_MODEL)
    tm = 512
    tables = _rope_tables(seq)
    for layer in range(DEPTH):
        j = layer // 2
        row = lambda a: a.reshape(1, D_MODEL)
        if layer % 2 == 0:
            qkv, gu = _even_proj(x, p["ev_w_in"][j], tables, seq, tm)
            res = [_band_attention(qkv, batch, seq, dil) for _, dil in DILATED_PATTERNS]
            x = _even_out([o for o, _ in res], [l for _, l in res], gu, x, p["ev_w_out"][j],
                          p["ev_conv_w"][j], row(p["ln_g"][layer, 0]), row(p["ln_b"][layer, 0]), seq, tm)
        else:
            proj = _matmul(x, p["od_w_in"][j], BF16, tm, D_MODEL)
            o = _gla(proj, p["lb_logits"], p["od_norm_g"][j].reshape(1, D_MODEL), batch, seq)
            x = _proj_ln(o, p["od_w_out"][j], x, row(p["ln_g"][layer, 0]), row(p["ln_b"][layer, 0]), tm)
        kv = _matmul(mem, p["xa_w_kv"][layer], BF16, 256, D_MODEL).reshape(batch, mem3.shape[1], 2 * D_MODEL)
        wr, br = _router_weights(p["moe_w_group"][layer], p["moe_b_group"][layer],
                                 p["moe_w_expert"][layer], p["moe_b_expert"][layer])
        x, route = _xattn(x, kv, p["xa_w_q"][layer], p["xa_w_out"][layer],
                          row(p["ln_g"][layer, 1]), row(p["ln_b"][layer, 1]), wr, br, seq, tm)
        x = _moe(x, route, p["moe_w13"][layer], p["moe_w2"][layer],
                 row(p["ln_g"][layer, 2]), row(p["ln_b"][layer, 2]), 512, tm)
    return x.reshape(batch, seq, D_MODEL)


def kernel(x_prompt, x_sample, mem_prompt, mem_sample, ev_w_in, ev_conv_w, ev_w_out, od_w_in, lb_logits,
           od_norm_g, od_w_out, xa_w_q, xa_w_kv, xa_w_out, moe_w_group, moe_b_group, moe_w_expert,
           moe_b_expert, moe_w1, moe_w3, moe_w2, ln_g, ln_b):
    ff = moe_w1.shape[-1]
    p = dict(
        ev_w_in=ev_w_in.astype(BF16), ev_conv_w=ev_conv_w, ev_w_out=ev_w_out.astype(BF16),
        od_w_in=od_w_in.astype(BF16), lb_logits=lb_logits, od_norm_g=od_norm_g,
        od_w_out=od_w_out.astype(BF16), xa_w_q=xa_w_q.astype(BF16), xa_w_kv=xa_w_kv.astype(BF16),
        xa_w_out=xa_w_out.astype(BF16), moe_w_group=moe_w_group, moe_b_group=moe_b_group,
        moe_w_expert=moe_w_expert, moe_b_expert=moe_b_expert,
        moe_w13=jnp.concatenate([moe_w1, moe_w3], axis=-1).astype(BF16).reshape(
            DEPTH, N_EXPERTS, D_MODEL, 2 * ff),
        moe_w2=moe_w2.astype(BF16).reshape(DEPTH, N_EXPERTS, ff, D_MODEL),
        ln_g=ln_g, ln_b=ln_b)
    return _trunk(x_prompt, mem_prompt, p), _trunk(x_sample, mem_sample, p)
```

```python
import functools
import math

import jax
import jax.numpy as jnp
from jax import lax
from jax.experimental import pallas as pl
from jax.experimental.pallas import tpu as pltpu

F32 = jnp.float32
BF16 = jnp.bfloat16

D_MODEL = 1024
DEPTH = 2
A_HEADS = 8
A_HEAD_DIM = 64
A_WIDTH = A_HEADS * A_HEAD_DIM
DILATED_PATTERNS = ((128, 1), (512, 4), (2048, 16))
ROPE_THETA = 500000.0
ROPE_DIM = A_HEAD_DIM // 4
B_WIDTH = D_MODEL // 2
CONV_WIDTH = 3
HG_HEADS = 8
HG_DIM = D_MODEL // HG_HEADS
XA_HEADS = 4
XA_HEAD_DIM = D_MODEL // XA_HEADS
N_GROUPS = 4
EXPERTS_PER_GROUP = 8
N_EXPERTS = N_GROUPS * EXPERTS_PER_GROUP
EXPERT_FF = D_MODEL // 4
LN_EPS = 1e-5
RMS_EPS = 1e-6
DEEPNORM_ALPHA = (2 * DEPTH) ** 0.25

LANES = 128
BAND_RADIUS = 64
ATTN_QBLOCK = 128
NEG_BIG = -1e30
GLA_CHUNK = 64
GLA_SUB = 16
GLA_EXP2_CLAMP = 100.0
GLA_INTRA_UNROLL = 4
GLA_SCAN_UNROLL = 8
LOG2_E = 1.4426950408889634
VMEM_LIMIT = 56 * 1024 * 1024


def _cparams(*sem):
    return pltpu.CompilerParams(dimension_semantics=sem, vmem_limit_bytes=VMEM_LIMIT)


def _layer_norm(y, g, b):
    mu = jnp.mean(y, axis=-1, keepdims=True)
    d = y - mu
    var = jnp.mean(d * d, axis=-1, keepdims=True)
    return d * lax.rsqrt(var + LN_EPS) * g + b


def _sigmoid(z):
    return 1.0 / (1.0 + jnp.exp(-z))


def _mm_kernel(x_ref, w_ref, o_ref, *, chunk):
    xb = x_ref[...].astype(BF16)
    for c in range(w_ref.shape[1] // chunk):
        cols = slice(c * chunk, (c + 1) * chunk)
        o_ref[:, cols] = jnp.dot(xb, w_ref[:, cols], preferred_element_type=F32).astype(o_ref.dtype)


def _matmul(x, w, out_dtype, tm, chunk):
    n, k = x.shape
    m = w.shape[1]
    return pl.pallas_call(
        functools.partial(_mm_kernel, chunk=chunk),
        grid=(n // tm,),
        in_specs=[pl.BlockSpec((tm, k), lambda i: (i, 0)), pl.BlockSpec((k, m), lambda i: (0, 0))],
        out_specs=pl.BlockSpec((tm, m), lambda i: (i, 0)),
        out_shape=jax.ShapeDtypeStruct((n, m), out_dtype),
        compiler_params=_cparams("parallel"),
        name="matmul",
    )(x, w)


def _rope_tables(seq):
    half = ROPE_DIM // 2
    inv_freq = jnp.exp(-math.log(ROPE_THETA) * jnp.arange(half, dtype=F32) * (2.0 / ROPE_DIM))
    ang = jnp.arange(seq, dtype=F32)[:, None] * inv_freq[None, :]
    cos, sin = jnp.cos(ang), jnp.sin(ang)
    ones = jnp.ones((seq, A_HEAD_DIM - ROPE_DIM), F32)
    zeros = jnp.zeros((seq, A_HEAD_DIM - ROPE_DIM), F32)
    zh = jnp.zeros((seq, half), F32)
    c = jnp.concatenate([cos, cos, ones], -1)
    s_up = jnp.concatenate([-sin, zh, zeros], -1)
    s_dn = jnp.concatenate([zh, sin, zeros], -1)
    rep = LANES // A_HEAD_DIM
    return tuple(jnp.tile(t, (1, rep)) for t in (c, s_up, s_dn))


def _even_proj_kernel(x_ref, w_ref, c_ref, su_ref, sd_ref, qkv_ref, gu_ref):
    xb = x_ref[...].astype(BF16)
    rep = A_WIDTH // LANES
    half = ROPE_DIM // 2
    c = jnp.tile(c_ref[...], (1, rep))
    su = jnp.tile(su_ref[...], (1, rep))
    sd = jnp.tile(sd_ref[...], (1, rep))

    def proj(j):
        return jnp.dot(xb, w_ref[:, j * A_WIDTH:(j + 1) * A_WIDTH], preferred_element_type=F32)

    def rope(t):
        up = pltpu.roll(t, A_WIDTH - half, axis=1)
        dn = pltpu.roll(t, half, axis=1)
        return t * c + up * su + dn * sd

    qkv_ref[:, 0:A_WIDTH] = (rope(proj(0)) * (A_HEAD_DIM ** -0.5)).astype(BF16)
    qkv_ref[:, A_WIDTH:2 * A_WIDTH] = rope(proj(1)).astype(BF16)
    qkv_ref[:, 2 * A_WIDTH:3 * A_WIDTH] = proj(2).astype(BF16)
    gu_ref[:, 0:B_WIDTH] = proj(3).astype(BF16)
    gu_ref[:, B_WIDTH:2 * B_WIDTH] = (proj(4) * proj(5)).astype(BF16)


def _even_proj(x, w_in, tables, seq, tm):
    n = x.shape[0]
    spt = seq // tm
    tab_spec = pl.BlockSpec((tm, LANES), lambda i: (i % spt, 0))
    return pl.pallas_call(
        _even_proj_kernel,
        grid=(n // tm,),
        in_specs=[pl.BlockSpec((tm, D_MODEL), lambda i: (i, 0)),
                  pl.BlockSpec(w_in.shape, lambda i: (0, 0)),
                  tab_spec, tab_spec, tab_spec],
        out_specs=[pl.BlockSpec((tm, 3 * A_WIDTH), lambda i: (i, 0)),
                   pl.BlockSpec((tm, 2 * B_WIDTH), lambda i: (i, 0))],
        out_shape=[jax.ShapeDtypeStruct((n, 3 * A_WIDTH), BF16),
                   jax.ShapeDtypeStruct((n, 2 * B_WIDTH), BF16)],
        compiler_params=_cparams("parallel"),
        name="even_proj",
    )(x, w_in, *tables)


def _band_attn_kernel(q_ref, kp_ref, km_ref, kn_ref, vp_ref, vm_ref, vn_ref, o_ref, lse_ref,
                      kbuf, vbuf, *, tq, length):
    i = pl.program_id(2)
    r = BAND_RADIUS
    kbuf[0:r] = kp_ref[0]
    kbuf[r:r + tq] = km_ref[0]
    kbuf[r + tq:r + tq + r] = kn_ref[0]
    vbuf[0:r] = vp_ref[0]
    vbuf[r:r + tq] = vm_ref[0]
    vbuf[r + tq:r + tq + r] = vn_ref[0]

    qb = ATTN_QBLOCK
    kw = qb + 2 * r
    qi = lax.broadcasted_iota(jnp.int32, (qb, kw), 0)
    kj = lax.broadcasted_iota(jnp.int32, (qb, kw), 1)
    rel = kj - qi
    band = (rel >= 0) & (rel <= 2 * r)
    lane = lax.broadcasted_iota(jnp.int32, (qb, LANES), 1)
    low = lane < A_HEAD_DIM
    nt = (((1,), (1,)), ((), ()))

    for s in range(tq // qb):
        kpos = i * tq + (s * qb - r) + kj
        valid = band & (kpos >= 0) & (kpos < length)
        bias = jnp.where(valid, 0.0, NEG_BIG)
        rows = slice(s * qb, (s + 1) * qb)
        wrows = slice(s * qb, s * qb + kw)
        for p in range(A_WIDTH // LANES):
            cols = slice(p * LANES, (p + 1) * LANES)
            qp = q_ref[0, rows, cols]
            kwin = kbuf[wrows, cols]
            vwin = vbuf[wrows, cols]
            outs, lses = [], []
            for sel in (low, jnp.logical_not(low)):
                qm = jnp.where(sel, qp, jnp.zeros_like(qp))
                sc = lax.dot_general(qm, kwin, nt, preferred_element_type=F32) + bias
                m = jnp.max(sc, axis=-1, keepdims=True)
                pe = jnp.exp(sc - m)
                l = jnp.sum(pe, axis=-1, keepdims=True)
                pv = jnp.dot(pe.astype(BF16), vwin, preferred_element_type=F32)
                outs.append(pv / l)
                lses.append(jnp.broadcast_to(m + jnp.log(l), (qb, LANES)))
            o_ref[0, rows, cols] = jnp.where(low, outs[0], outs[1]).astype(o_ref.dtype)
            lse_ref[0, rows, cols] = jnp.where(low, lses[0], lses[1])


def _band_attention(qkv, batch, seq, dil):
    length = seq // dil
    tq = min(512, length)
    r = BAND_RADIUS
    view = qkv.reshape(batch, length, dil * 3 * A_WIDTH)
    nblk_h = length // r
    per = tq // r

    def main(j):
        return pl.BlockSpec((1, tq, A_WIDTH), lambda b, rr, i: (b, i, rr * 3 + j))

    def prev(j):
        return pl.BlockSpec((1, r, A_WIDTH), lambda b, rr, i: (b, jnp.maximum(i * per - 1, 0), rr * 3 + j))

    def nxt(j):
        return pl.BlockSpec((1, r, A_WIDTH),
                            lambda b, rr, i: (b, jnp.minimum((i + 1) * per, nblk_h - 1), rr * 3 + j))

    out_spec = pl.BlockSpec((1, tq, A_WIDTH), lambda b, rr, i: (b, i, rr))
    o, lse = pl.pallas_call(
        functools.partial(_band_attn_kernel, tq=tq, length=length),
        grid=(batch, dil, length // tq),
        in_specs=[main(0), prev(1), main(1), nxt(1), prev(2), main(2), nxt(2)],
        out_specs=[out_spec, out_spec],
        out_shape=[jax.ShapeDtypeStruct((batch, length, dil * A_WIDTH), BF16),
                   jax.ShapeDtypeStruct((batch, length, dil * A_WIDTH), F32)],
        scratch_shapes=[pltpu.VMEM((tq + 2 * r, A_WIDTH), BF16), pltpu.VMEM((tq + 2 * r, A_WIDTH), BF16)],
        compiler_params=_cparams("parallel", "parallel", "parallel"),
        name=f"band_attn_d{dil}",
    )(view, view, view, view, view, view, view)
    return o.reshape(batch * seq, A_WIDTH), lse.reshape(batch * seq, A_WIDTH)


def _even_out_kernel(o1, o2, o3, l1, l2, l3, gu_ref, up_ref, un_ref, x_ref, w_ref, cw_ref, g_ref, b_ref,
                     y_ref, *, tm, seq):
    i = pl.program_id(0)
    pos = (i * tm) % seq
    la, lb, lc = l1[...], l2[...], l3[...]
    mx = jnp.maximum(jnp.maximum(la, lb), lc)
    ea, eb, ec = jnp.exp(la - mx), jnp.exp(lb - mx), jnp.exp(lc - mx)
    num = ea * o1[...].astype(F32) + eb * o2[...].astype(F32) + ec * o3[...].astype(F32)
    a_out = num / (ea + eb + ec)

    gate_b = gu_ref[:, 0:B_WIDTH].astype(F32)
    u = gu_ref[:, B_WIDTH:2 * B_WIDTH].astype(F32)
    hrows = up_ref.shape[0]
    u_before = jnp.where(pos > 0, up_ref[hrows - 1:hrows, :].astype(F32), 0.0)
    u_after = jnp.where(pos + tm < seq, un_ref[0:1, :].astype(F32), 0.0)
    row = lax.broadcasted_iota(jnp.int32, (tm, B_WIDTH), 0)
    u_prev = jnp.where(row == 0, u_before, pltpu.roll(u, 1, axis=0))
    u_next = jnp.where(row == tm - 1, u_after, pltpu.roll(u, tm - 1, axis=0))
    conv = u_prev * cw_ref[0:1, :] + u * cw_ref[1:2, :] + u_next * cw_ref[2:3, :]
    b_out = gate_b * conv

    mix = jnp.dot(a_out.astype(BF16), w_ref[0:A_WIDTH, :], preferred_element_type=F32)
    mix = mix + jnp.dot(b_out.astype(BF16), w_ref[A_WIDTH:A_WIDTH + B_WIDTH, :], preferred_element_type=F32)
    y_ref[...] = _layer_norm(DEEPNORM_ALPHA * x_ref[...] + mix, g_ref[...], b_ref[...])


def _even_out(outs, lses, gu, x, w_out, conv_w, g, b, seq, tm):
    n = x.shape[0]
    hrows = 16
    nh = n // hrows
    per = tm // hrows
    row_a = pl.BlockSpec((tm, A_WIDTH), lambda i: (i, 0))
    full = lambda a: pl.BlockSpec(a.shape, lambda i: (0, 0))
    return pl.pallas_call(
        functools.partial(_even_out_kernel, tm=tm, seq=seq),
        grid=(n // tm,),
        in_specs=[row_a] * 6 + [
            pl.BlockSpec((tm, 2 * B_WIDTH), lambda i: (i, 0)),
            pl.BlockSpec((hrows, B_WIDTH), lambda i: (jnp.maximum(i * per - 1, 0), 1)),
            pl.BlockSpec((hrows, B_WIDTH), lambda i: (jnp.minimum((i + 1) * per, nh - 1), 1)),
            pl.BlockSpec((tm, D_MODEL), lambda i: (i, 0)),
            full(w_out), full(conv_w), full(g), full(b)],
        out_specs=pl.BlockSpec((tm, D_MODEL), lambda i: (i, 0)),
        out_shape=jax.ShapeDtypeStruct((n, D_MODEL), F32),
        compiler_params=_cparams("parallel"),
        name="even_out",
    )(*outs, *lses, gu, gu, gu, x, w_out, conv_w, g, b)


def _gla_gates(z, lb, tri2):
    f = lb + (1.0 - lb) * _sigmoid(z)
    lf2 = jnp.log(f) * LOG2_E
    hi = lf2.astype(BF16)
    lo = (lf2 - hi.astype(F32)).astype(BF16)
    cum = jnp.dot(tri2, jnp.concatenate([hi, lo], axis=0), preferred_element_type=F32)
    return 1.0 - f, cum


def _gla_scores(q, kk, cum, *, reverse):
    c = GLA_CHUNK
    sb = GLA_SUB
    dk = q.shape[1]
    nt = (((1,), (1,)), ((), ()))
    ti = lax.broadcasted_iota(jnp.int32, (c, c), 0)
    si = lax.broadcasted_iota(jnp.int32, (c, c), 1)
    causal = (si >= ti) if reverse else (ti >= si)

    parts = []
    for blk in range(c // sb):
        rows = slice(blk * sb, (blk + 1) * sb)
        if reverse:
            edge = (blk + 1) * sb
            ref = cum[edge:edge + 1, :] if edge < c else jnp.zeros((1, dk), F32)
            other = slice((blk + 1) * sb, c)
        else:
            edge = blk * sb - 1
            ref = cum[edge:edge + 1, :] if edge >= 0 else jnp.zeros((1, dk), F32)
            other = slice(0, blk * sb)
        qs = q[rows] * jnp.exp2(cum[rows] - ref)
        k_own = (kk[rows] * jnp.exp2(jnp.minimum(ref - cum[rows], GLA_EXP2_CLAMP))).astype(BF16)
        pieces = [k_own]
        n_other = other.stop - other.start
        if n_other:
            k_other = (kk[other] * jnp.exp2(ref - cum[other])).astype(BF16)
            pieces = [k_own, k_other] if reverse else [k_other, k_own]
        if n_other + sb < c:
            pad = jnp.zeros((c - n_other - sb, dk), BF16)
            pieces = [pad] + pieces if reverse else pieces + [pad]
        ks = jnp.concatenate(pieces, axis=0) if len(pieces) > 1 else pieces[0]
        parts.append(lax.dot_general(qs.astype(BF16), ks, nt, preferred_element_type=F32))
    return jnp.where(causal, jnp.concatenate(parts, axis=0), 0.0).astype(BF16)


def _gla_state_terms(q, kk, cum, *, reverse):
    c = GLA_CHUNK
    total = cum[0:1, :] if reverse else cum[c - 1:c, :]
    qe = (q * jnp.exp2(cum)).astype(BF16)
    kd = (kk * jnp.exp2(total - cum)).astype(BF16)
    return qe, kd, jnp.exp2(total)


def _gla_kernel(q_ref, zf_ref, zb_ref, v_ref, g_ref, lbl_ref, ng_ref, o_ref,
                acc_f, acc_b, qe_f, qe_b, kd_f, kd_b, et_f, et_b, *, seq):
    c = GLA_CHUNK
    nc = seq // c
    dk = HG_DIM
    nt = (((1,), (1,)), ((), ()))
    tn = (((0,), (0,)), ((), ()))
    l0 = lbl_ref[0:1, :]
    l1 = lbl_ref[1:2, :]
    mx = jnp.maximum(l0, l1)
    e0, e1 = jnp.exp(l0 - mx), jnp.exp(l1 - mx)
    lb = e0 / (e0 + e1)

    ti = lax.broadcasted_iota(jnp.int32, (c, 2 * c), 0)
    si = lax.broadcasted_iota(jnp.int32, (c, 2 * c), 1) & (c - 1)
    tri_f = jnp.where(ti >= si, 1.0, 0.0).astype(BF16)
    tri_b = jnp.where(si >= ti, 1.0, 0.0).astype(BF16)

    dirs = ((zf_ref, tri_f, acc_f, qe_f, kd_f, et_f, False),
            (zb_ref, tri_b, acc_b, qe_b, kd_b, et_b, True))

    def intra(j, _):
        chunks = [j * GLA_INTRA_UNROLL + u for u in range(GLA_INTRA_UNROLL)]
        rows = [pl.ds(pl.multiple_of(ci * c, c), c) for ci in chunks]
        chains = [(u, d) for u in range(GLA_INTRA_UNROLL) for d in range(2)]
        gates = [_gla_gates(dirs[d][0][0, rows[u], :].astype(F32), lb, dirs[d][1]) for u, d in chains]
        qs = [q_ref[0, r, :].astype(F32) for r in rows]
        atts = [_gla_scores(qs[u], kk, cum, reverse=dirs[d][6]) for (u, d), (kk, cum) in zip(chains, gates)]
        for (u, d), att in zip(chains, atts):
            dirs[d][2][rows[u], :] = jnp.dot(att, v_ref[0, rows[u], :], preferred_element_type=F32)
        for (u, d), (kk, cum) in zip(chains, gates):
            qe, kd, et = _gla_state_terms(qs[u], kk, cum, reverse=dirs[d][6])
            dirs[d][3][rows[u], :] = qe
            dirs[d][4][rows[u], :] = kd
            dirs[d][5][pl.ds(chunks[u], 1), :] = et
        return 0

    lax.fori_loop(0, nc // GLA_INTRA_UNROLL, intra, 0)

    def scan(j, carry):
        sf, sr = carry
        cfs = [j * GLA_SCAN_UNROLL + u for u in range(GLA_SCAN_UNROLL)]
        crs = [nc - 1 - cf for cf in cfs]
        rfs = [pl.ds(pl.multiple_of(cf * c, c), c) for cf in cfs]
        rrs = [pl.ds(pl.multiple_of(cr * c, c), c) for cr in crs]
        upd_f = [lax.dot_general(v_ref[0, r, :], kd_f[r, :], tn, preferred_element_type=F32) for r in rfs]
        upd_r = [lax.dot_general(v_ref[0, r, :], kd_b[r, :], tn, preferred_element_type=F32) for r in rrs]
        sfs, srs = [sf], [sr]
        for u in range(GLA_SCAN_UNROLL):
            sfs.append(sfs[-1] * et_f[pl.ds(cfs[u], 1), :] + upd_f[u])
            srs.append(srs[-1] * et_b[pl.ds(crs[u], 1), :] + upd_r[u])
        for u in range(GLA_SCAN_UNROLL):
            acc_f[rfs[u], :] += lax.dot_general(qe_f[rfs[u], :], sfs[u].astype(BF16), nt,
                                                preferred_element_type=F32)
            acc_b[rrs[u], :] += lax.dot_general(qe_b[rrs[u], :], srs[u].astype(BF16), nt,
                                                preferred_element_type=F32)
        return sfs[-1], srs[-1]

    zero = jnp.zeros((dk, dk), F32)
    lax.fori_loop(0, nc // GLA_SCAN_UNROLL, scan, (zero, zero))

    blk = 512
    ng = ng_ref[...]

    def fin(j, _):
        rows = pl.ds(pl.multiple_of(j * blk, blk), blk)
        o = acc_f[rows, :] + acc_b[rows, :]
        o = o * lax.rsqrt(jnp.mean(o * o, axis=-1, keepdims=True) + RMS_EPS) * ng
        g = g_ref[0, rows, :].astype(F32)
        o_ref[0, rows, :] = (o * (g * _sigmoid(g))).astype(o_ref.dtype)
        return 0

    lax.fori_loop(0, seq // blk, fin, 0)


def _gla(proj, lb_logits, norm_g, batch, seq):
    view = proj.reshape(batch, seq, 5 * D_MODEL)

    def col(seg):
        return pl.BlockSpec((1, seq, HG_DIM), lambda b, h: (b, 0, seg * HG_HEADS + h))

    return pl.pallas_call(
        functools.partial(_gla_kernel, seq=seq),
        grid=(batch, HG_HEADS),
        in_specs=[col(0), col(1), col(2), col(3), col(4),
                  pl.BlockSpec((DEPTH, HG_DIM), lambda b, h: (0, h)),
                  pl.BlockSpec((1, HG_DIM), lambda b, h: (0, h))],
        out_specs=pl.BlockSpec((1, seq, HG_DIM), lambda b, h: (b, 0, h)),
        out_shape=jax.ShapeDtypeStruct((batch, seq, D_MODEL), BF16),
        scratch_shapes=[pltpu.VMEM((seq, HG_DIM), F32), pltpu.VMEM((seq, HG_DIM), F32),
                        pltpu.VMEM((seq, HG_DIM), BF16), pltpu.VMEM((seq, HG_DIM), BF16),
                        pltpu.VMEM((seq, HG_DIM), BF16), pltpu.VMEM((seq, HG_DIM), BF16),
                        pltpu.VMEM((seq // GLA_CHUNK, HG_DIM), F32),
                        pltpu.VMEM((seq // GLA_CHUNK, HG_DIM), F32)],
        compiler_params=_cparams("parallel", "parallel"),
        name="gla",
    )(view, view, view, view, view, lb_logits, norm_g).reshape(batch * seq, D_MODEL)


def _proj_ln_kernel(a_ref, w_ref, x_ref, g_ref, b_ref, y_ref):
    mix = jnp.dot(a_ref[...], w_ref[...], preferred_element_type=F32)
    y_ref[...] = _layer_norm(DEEPNORM_ALPHA * x_ref[...] + mix, g_ref[...], b_ref[...])


def _proj_ln(a, w, x, g, b, tm):
    n = x.shape[0]
    full = lambda t: pl.BlockSpec(t.shape, lambda i: (0, 0))
    row = pl.BlockSpec((tm, D_MODEL), lambda i: (i, 0))
    return pl.pallas_call(
        _proj_ln_kernel,
        grid=(n // tm,),
        in_specs=[row, full(w), row, full(g), full(b)],
        out_specs=row,
        out_shape=jax.ShapeDtypeStruct((n, D_MODEL), F32),
        compiler_params=_cparams("parallel"),
        name="proj_ln",
    )(a, w, x, g, b)


ROUTER_ROWS = 64
ROUTE_OUT_ROWS = 8


def _first_index_of(vals, target, n_rows):
    idx = lax.broadcasted_iota(jnp.int32, vals.shape, 0)
    return jnp.min(jnp.where(vals == target, idx, n_rows), axis=0, keepdims=True)


def _xattn_kernel(x_ref, wq_ref, kv_ref, wo_ref, g_ref, b_ref, wr_ref, br_ref, y_ref, yb_ref, route_ref):
    x = x_ref[...]
    q = jnp.dot(x.astype(BF16), wq_ref[...], preferred_element_type=F32) * (XA_HEAD_DIM ** -0.5)
    qb = q.astype(BF16)
    nt = (((1,), (1,)), ((), ()))
    heads = []
    for h in range(XA_HEADS):
        cols = slice(h * XA_HEAD_DIM, (h + 1) * XA_HEAD_DIM)
        k = kv_ref[0, :, cols]
        v = kv_ref[0, :, D_MODEL + h * XA_HEAD_DIM:D_MODEL + (h + 1) * XA_HEAD_DIM]
        sc = lax.dot_general(qb[:, cols], k, nt, preferred_element_type=F32)
        m = jnp.max(sc, axis=-1, keepdims=True)
        pe = jnp.exp(sc - m)
        p = pe / jnp.sum(pe, axis=-1, keepdims=True)
        heads.append(jnp.dot(p.astype(BF16), v, preferred_element_type=F32).astype(BF16))
    o = jnp.concatenate(heads, axis=1)
    xa = jnp.dot(o, wo_ref[...], preferred_element_type=F32)
    y = _layer_norm(DEEPNORM_ALPHA * x + xa, g_ref[...], b_ref[...])
    y_ref[...] = y
    yb_ref[...] = y.astype(BF16)

    lg = lax.dot_general(wr_ref[...], y.astype(BF16), nt, preferred_element_type=F32) + br_ref[:, 0:1]
    gl = lg[0:N_GROUPS, :]
    gmax = jnp.max(gl, axis=0, keepdims=True)
    g_w = 1.0 / jnp.sum(jnp.exp(gl - gmax), axis=0, keepdims=True)
    g_sel = _first_index_of(gl, gmax, N_GROUPS)
    el = jnp.zeros((EXPERTS_PER_GROUP, gl.shape[1]), F32)
    for grp in range(N_GROUPS):
        rows = slice(8 + grp * EXPERTS_PER_GROUP, 8 + (grp + 1) * EXPERTS_PER_GROUP)
        el = el + jnp.where(g_sel == grp, lg[rows, :], 0.0)
    m1 = jnp.max(el, axis=0, keepdims=True)
    i1 = _first_index_of(el, m1, EXPERTS_PER_GROUP)
    eidx = lax.broadcasted_iota(jnp.int32, el.shape, 0)
    el2 = jnp.where(eidx == i1, -jnp.inf, el)
    m2 = jnp.max(el2, axis=0, keepdims=True)
    i2 = _first_index_of(el2, m2, EXPERTS_PER_GROUP)
    e2 = jnp.exp(m2 - m1)
    den = 1.0 + e2
    w1 = g_w / den
    w2 = g_w * e2 / den
    base = g_sel * EXPERTS_PER_GROUP
    zero = jnp.zeros_like(w1)
    route_ref[...] = jnp.concatenate(
        [(base + i1).astype(F32), (base + i2).astype(F32), w1, w2, zero, zero, zero, zero], axis=0)


def _xattn(x, kv, wq, wo, g, b, wr, br, seq, tm):
    n = x.shape[0]
    spt = seq // tm
    full = lambda t: pl.BlockSpec(t.shape, lambda i: (0, 0))
    row = pl.BlockSpec((tm, D_MODEL), lambda i: (i, 0))
    return pl.pallas_call(
        _xattn_kernel,
        grid=(n // tm,),
        in_specs=[row, full(wq),
                  pl.BlockSpec((1,) + kv.shape[1:], lambda i: (i // spt, 0, 0)),
                  full(wo), full(g), full(b), full(wr), full(br)],
        out_specs=[row, row, pl.BlockSpec((ROUTE_OUT_ROWS, tm), lambda i: (0, i))],
        out_shape=[jax.ShapeDtypeStruct((n, D_MODEL), F32),
                   jax.ShapeDtypeStruct((n, D_MODEL), BF16),
                   jax.ShapeDtypeStruct((ROUTE_OUT_ROWS, n), F32)],
        compiler_params=_cparams("parallel"),
        name="xattn_router",
    )(x, wq, kv, wo, g, b, wr, br)


def _expert_kernel(te_ref, nu_ref, xs_ref, w1_ref, w3_ref, w2_ref, ys_ref):
    j = pl.program_id(0)

    @pl.when(j < nu_ref[0])
    def _():
        xs = xs_ref[...]
        h1 = jnp.dot(xs, w1_ref[0].astype(BF16), preferred_element_type=F32)
        h3 = jnp.dot(xs, w3_ref[0].astype(BF16), preferred_element_type=F32)
        hid = h1 * _sigmoid(h1) * h3
        ys_ref[...] = jnp.dot(hid.astype(BF16), w2_ref[0].astype(BF16),
                              preferred_element_type=F32).astype(ys_ref.dtype)

    @pl.when(j >= nu_ref[0])
    def _():
        ys_ref[...] = jnp.zeros_like(ys_ref)


def _experts(xs, tile_expert, n_used, w1, w3, w2, tm):
    mp = xs.shape[0]
    w_in = pl.BlockSpec((1, D_MODEL, EXPERT_FF), lambda j, te, nu: (te[j], 0, 0))
    grid_spec = pltpu.PrefetchScalarGridSpec(
        num_scalar_prefetch=2,
        grid=(mp // tm,),
        in_specs=[pl.BlockSpec((tm, D_MODEL), lambda j, te, nu: (j, 0)), w_in, w_in,
                  pl.BlockSpec((1, EXPERT_FF, D_MODEL), lambda j, te, nu: (te[j], 0, 0))],
        out_specs=pl.BlockSpec((tm, D_MODEL), lambda j, te, nu: (j, 0)),
    )
    return pl.pallas_call(
        _expert_kernel,
        grid_spec=grid_spec,
        out_shape=jax.ShapeDtypeStruct((mp, D_MODEL), BF16),
        compiler_params=_cparams("arbitrary"),
        name="experts",
    )(tile_expert, n_used, xs, w1, w3, w2)


def _moe_ln_kernel(x_ref, y0_ref, y1_ref, gate_ref, g_ref, b_ref, o_ref):
    ff = gate_ref[:, 0:1] * y0_ref[...].astype(F32) + gate_ref[:, 1:2] * y1_ref[...].astype(F32)
    o_ref[...] = _layer_norm(DEEPNORM_ALPHA * x_ref[...] + ff, g_ref[...], b_ref[...])


def _moe_ln(x, y0, y1, gates, g, b, tm):
    n = x.shape[0]
    full = lambda t: pl.BlockSpec(t.shape, lambda i: (0, 0))
    row = pl.BlockSpec((tm, D_MODEL), lambda i: (i, 0))
    return pl.pallas_call(
        _moe_ln_kernel,
        grid=(n // tm,),
        in_specs=[row, row, row, pl.BlockSpec((tm, 2), lambda i: (i, 0)), full(g), full(b)],
        out_specs=row,
        out_shape=jax.ShapeDtypeStruct((n, D_MODEL), F32),
        compiler_params=_cparams("parallel"),
        name="moe_ln",
    )(x, y0, y1, gates, g, b)


def _dispatch_plan(ids, n, tm_e):
    n_asg = 2 * n
    mp = n_asg + N_EXPERTS * tm_e
    order = jnp.argsort(ids, stable=True).astype(jnp.int32)
    pos = jnp.argsort(order).astype(jnp.int32)
    onehot = (ids[:, None] == jnp.arange(N_EXPERTS, dtype=jnp.int32)[None, :]).astype(jnp.int32)
    counts = jnp.sum(onehot, axis=0)
    dense_start = jnp.cumsum(counts) - counts
    padded = ((counts + tm_e - 1) // tm_e) * tm_e
    row_end = jnp.cumsum(padded)
    row_start = row_end - padded
    row_of_asg = pos + jnp.sum(onehot * (row_start - dense_start)[None, :], axis=1)
    tile_start = jnp.arange(mp // tm_e, dtype=jnp.int32) * tm_e
    tile_expert = jnp.minimum(jnp.sum((tile_start[:, None] >= row_end[None, :]).astype(jnp.int32), axis=1),
                              N_EXPERTS - 1)
    shift = (dense_start - row_start)[tile_expert]
    src = (tile_start + shift)[:, None] + jnp.arange(tm_e, dtype=jnp.int32)[None, :]
    asg_of_row = order[jnp.clip(src.reshape(mp), 0, n_asg - 1)]
    tok_of_row = jnp.where(asg_of_row >= n, asg_of_row - n, asg_of_row)
    n_used = (row_end[-1] // tm_e).astype(jnp.int32).reshape(1)
    return tok_of_row, row_of_asg, tile_expert.astype(jnp.int32), n_used


def _moe(x, xb, route, w1, w3, w2, g, b, tm_e, tm):
    n = x.shape[0]
    ids = route[0:2].astype(jnp.int32).reshape(2 * n)
    tok_of_row, row_of_asg, tile_expert, n_used = _dispatch_plan(ids, n, tm_e)
    xs = xb[tok_of_row]
    ys = _experts(xs, tile_expert, n_used, w1, w3, w2, tm_e)
    y0 = ys[row_of_asg[:n]]
    y1 = ys[row_of_asg[n:]]
    return _moe_ln(x, y0, y1, route[2:4].T, g, b, tm)


def _router_weights(w_group, b_group, w_expert, b_expert):
    wr = jnp.zeros((ROUTER_ROWS, D_MODEL), F32)
    wr = wr.at[0:N_GROUPS].set(w_group.T).at[8:8 + N_EXPERTS].set(w_expert.T)
    br = jnp.zeros((ROUTER_ROWS, LANES), F32)
    br = br.at[0:N_GROUPS, :].set(b_group[:, None]).at[8:8 + N_EXPERTS, :].set(b_expert[:, None])
    return wr.astype(BF16), br


def _trunk(x3, mem3, p):
    batch, seq, _ = x3.shape
    n = batch * seq
    x = x3.reshape(n, D_MODEL)
    mem = mem3.reshape(batch * mem3.shape[1], D_MODEL)
    tm = 512
    tables = _rope_tables(seq)
    for layer in range(DEPTH):
        j = layer // 2
        row = lambda a: a.reshape(1, D_MODEL)
        if layer % 2 == 0:
            qkv, gu = _even_proj(x, p["ev_w_in"][j], tables, seq, tm)
            res = [_band_attention(qkv, batch, seq, dil) for _, dil in DILATED_PATTERNS]
            x = _even_out([o for o, _ in res], [l for _, l in res], gu, x, p["ev_w_out"][j],
                          p["ev_conv_w"][j], row(p["ln_g"][layer, 0]), row(p["ln_b"][layer, 0]), seq, tm)
        else:
            proj = _matmul(x, p["od_w_in"][j], BF16, tm, D_MODEL)
            o = _gla(proj, p["lb_logits"], p["od_norm_g"][j].reshape(1, D_MODEL), batch, seq)
            x = _proj_ln(o, p["od_w_out"][j], x, row(p["ln_g"][layer, 0]), row(p["ln_b"][layer, 0]), tm)
        kv = _matmul(mem, p["xa_w_kv"][layer], BF16, 256, D_MODEL).reshape(batch, mem3.shape[1], 2 * D_MODEL)
        wr, br = _router_weights(p["moe_w_group"][layer], p["moe_b_group"][layer],
                                 p["moe_w_expert"][layer], p["moe_b_expert"][layer])
        x, xb, route = _xattn(x, kv, p["xa_w_q"][layer], p["xa_w_out"][layer],
                              row(p["ln_g"][layer, 1]), row(p["ln_b"][layer, 1]), wr, br, seq, tm)
        x = _moe(x, xb, route, p["moe_w1"][layer], p["moe_w3"][layer], p["moe_w2"][layer],
                 row(p["ln_g"][layer, 2]), row(p["ln_b"][layer, 2]), 512, tm)
    return x.reshape(batch, seq, D_MODEL)


def kernel(x_prompt, x_sample, mem_prompt, mem_sample, ev_w_in, ev_conv_w, ev_w_out, od_w_in, lb_logits,
           od_norm_g, od_w_out, xa_w_q, xa_w_kv, xa_w_out, moe_w_group, moe_b_group, moe_w_expert,
           moe_b_expert, moe_w1, moe_w3, moe_w2, ln_g, ln_b):
    ff = moe_w1.shape[-1]
    p = dict(
        ev_w_in=ev_w_in.astype(BF16), ev_conv_w=ev_conv_w, ev_w_out=ev_w_out.astype(BF16),
        od_w_in=od_w_in.astype(BF16), lb_logits=lb_logits, od_norm_g=od_norm_g,
        od_w_out=od_w_out.astype(BF16), xa_w_q=xa_w_q.astype(BF16), xa_w_kv=xa_w_kv.astype(BF16),
        xa_w_out=xa_w_out.astype(BF16), moe_w_group=moe_w_group, moe_b_group=moe_b_group,
        moe_w_expert=moe_w_expert, moe_b_expert=moe_b_expert,
        moe_w1=moe_w1.reshape(DEPTH, N_EXPERTS, D_MODEL, ff),
        moe_w3=moe_w3.reshape(DEPTH, N_EXPERTS, D_MODEL, ff),
        moe_w2=moe_w2.reshape(DEPTH, N_EXPERTS, ff, D_MODEL),
        ln_g=ln_g, ln_b=ln_b)
    return _trunk(x_prompt, mem_prompt, p), _trunk(x_sample, mem_sample, p)
```

```python
import functools
import math

import jax
import jax.numpy as jnp
from jax import lax
from jax.experimental import pallas as pl
from jax.experimental.pallas import tpu as pltpu

F32 = jnp.float32
BF16 = jnp.bfloat16

D_MODEL = 1024
DEPTH = 2
A_HEADS = 8
A_HEAD_DIM = 64
A_WIDTH = A_HEADS * A_HEAD_DIM
DILATED_PATTERNS = ((128, 1), (512, 4), (2048, 16))
ROPE_THETA = 500000.0
ROPE_DIM = A_HEAD_DIM // 4
B_WIDTH = D_MODEL // 2
CONV_WIDTH = 3
HG_HEADS = 8
HG_DIM = D_MODEL // HG_HEADS
XA_HEADS = 4
XA_HEAD_DIM = D_MODEL // XA_HEADS
N_GROUPS = 4
EXPERTS_PER_GROUP = 8
N_EXPERTS = N_GROUPS * EXPERTS_PER_GROUP
EXPERT_FF = D_MODEL // 4
LN_EPS = 1e-5
RMS_EPS = 1e-6
DEEPNORM_ALPHA = (2 * DEPTH) ** 0.25

LANES = 128
BAND_RADIUS = 64
ATTN_QBLOCK = 128
NEG_BIG = -1e30
GLA_CHUNK = 64
GLA_SUB = 16
GLA_EXP2_CLAMP = 100.0
GLA_INTRA_UNROLL = 2
GLA_SCAN_UNROLL = 8
LOG2_E = 1.4426950408889634
VMEM_LIMIT = 56 * 1024 * 1024


def _cparams(*sem):
    return pltpu.CompilerParams(dimension_semantics=sem, vmem_limit_bytes=VMEM_LIMIT)


def _layer_norm(y, g, b):
    mu = jnp.mean(y, axis=-1, keepdims=True)
    d = y - mu
    var = jnp.mean(d * d, axis=-1, keepdims=True)
    return d * lax.rsqrt(var + LN_EPS) * g + b


def _sigmoid(z):
    return 1.0 / (1.0 + jnp.exp(-z))


def _mm_kernel(x_ref, w_ref, o_ref, *, chunk):
    xb = x_ref[...].astype(BF16)
    for c in range(w_ref.shape[1] // chunk):
        cols = slice(c * chunk, (c + 1) * chunk)
        o_ref[:, cols] = jnp.dot(xb, w_ref[:, cols], preferred_element_type=F32).astype(o_ref.dtype)


def _matmul(x, w, out_dtype, tm, chunk):
    n, k = x.shape
    m = w.shape[1]
    return pl.pallas_call(
        functools.partial(_mm_kernel, chunk=chunk),
        grid=(n // tm,),
        in_specs=[pl.BlockSpec((tm, k), lambda i: (i, 0)), pl.BlockSpec((k, m), lambda i: (0, 0))],
        out_specs=pl.BlockSpec((tm, m), lambda i: (i, 0)),
        out_shape=jax.ShapeDtypeStruct((n, m), out_dtype),
        compiler_params=_cparams("parallel"),
        name="matmul",
    )(x, w)


def _rope_tables(seq):
    half = ROPE_DIM // 2
    inv_freq = jnp.exp(-math.log(ROPE_THETA) * jnp.arange(half, dtype=F32) * (2.0 / ROPE_DIM))
    ang = jnp.arange(seq, dtype=F32)[:, None] * inv_freq[None, :]
    cos, sin = jnp.cos(ang), jnp.sin(ang)
    ones = jnp.ones((seq, A_HEAD_DIM - ROPE_DIM), F32)
    zeros = jnp.zeros((seq, A_HEAD_DIM - ROPE_DIM), F32)
    zh = jnp.zeros((seq, half), F32)
    c = jnp.concatenate([cos, cos, ones], -1)
    s_up = jnp.concatenate([-sin, zh, zeros], -1)
    s_dn = jnp.concatenate([zh, sin, zeros], -1)
    rep = LANES // A_HEAD_DIM
    return tuple(jnp.tile(t, (1, rep)) for t in (c, s_up, s_dn))


def _even_proj_kernel(x_ref, w_ref, c_ref, su_ref, sd_ref, qkv_ref, qkv4_ref, qkv16_ref, gu_ref, slab_ref):
    tm = x_ref.shape[0]
    xb = x_ref[...].astype(BF16)
    rep = A_WIDTH // LANES
    half = ROPE_DIM // 2
    c = jnp.tile(c_ref[...], (1, rep))
    su = jnp.tile(su_ref[...], (1, rep))
    sd = jnp.tile(sd_ref[...], (1, rep))

    def proj(j):
        return jnp.dot(xb, w_ref[:, j * A_WIDTH:(j + 1) * A_WIDTH], preferred_element_type=F32)

    def rope(t):
        up = pltpu.roll(t, A_WIDTH - half, axis=1)
        dn = pltpu.roll(t, half, axis=1)
        return t * c + up * su + dn * sd

    qkv = (rope(proj(0)) * (A_HEAD_DIM ** -0.5), rope(proj(1)), proj(2))
    per = A_WIDTH // LANES
    for j, part in enumerate(qkv):
        qkv_ref[:, j * A_WIDTH:(j + 1) * A_WIDTH] = part.astype(BF16)
        for s in range(per):
            slab_ref[j * per + s] = part[:, s * LANES:(s + 1) * LANES]
    for dil, out_ref in ((DILATED_PATTERNS[1][1], qkv4_ref), (DILATED_PATTERNS[2][1], qkv16_ref)):
        for r in range(dil):
            for s in range(3 * per):
                val = slab_ref[s, pl.ds(r, tm // dil, stride=dil), :]
                col = r * 3 * A_WIDTH + s * LANES
                out_ref[:, col:col + LANES] = val.astype(BF16)
    gu_ref[:, 0:B_WIDTH] = proj(3).astype(BF16)
    gu_ref[:, B_WIDTH:2 * B_WIDTH] = (proj(4) * proj(5)).astype(BF16)


def _even_proj(x, w_in, tables, seq, tm):
    n = x.shape[0]
    spt = seq // tm
    tab_spec = pl.BlockSpec((tm, LANES), lambda i: (i % spt, 0))
    d4, d16 = DILATED_PATTERNS[1][1], DILATED_PATTERNS[2][1]
    width = 3 * A_WIDTH
    return pl.pallas_call(
        _even_proj_kernel,
        grid=(n // tm,),
        in_specs=[pl.BlockSpec((tm, D_MODEL), lambda i: (i, 0)),
                  pl.BlockSpec(w_in.shape, lambda i: (0, 0)),
                  tab_spec, tab_spec, tab_spec],
        out_specs=[pl.BlockSpec((tm, width), lambda i: (i, 0)),
                   pl.BlockSpec((tm // d4, d4 * width), lambda i: (i, 0)),
                   pl.BlockSpec((tm // d16, d16 * width), lambda i: (i, 0)),
                   pl.BlockSpec((tm, 2 * B_WIDTH), lambda i: (i, 0))],
        out_shape=[jax.ShapeDtypeStruct((n, width), BF16),
                   jax.ShapeDtypeStruct((n // d4, d4 * width), BF16),
                   jax.ShapeDtypeStruct((n // d16, d16 * width), BF16),
                   jax.ShapeDtypeStruct((n, 2 * B_WIDTH), BF16)],
        scratch_shapes=[pltpu.VMEM((width // LANES, tm, LANES), F32)],
        compiler_params=_cparams("parallel"),
        name="even_proj",
    )(x, w_in, *tables)


def _band_attn_kernel(q_ref, kp_ref, km_ref, kn_ref, vp_ref, vm_ref, vn_ref, o_ref, lse_ref,
                      kbuf, vbuf, *, tq, length):
    i = pl.program_id(2)
    r = BAND_RADIUS
    kbuf[0:r] = kp_ref[0]
    kbuf[r:r + tq] = km_ref[0]
    kbuf[r + tq:r + tq + r] = kn_ref[0]
    vbuf[0:r] = vp_ref[0]
    vbuf[r:r + tq] = vm_ref[0]
    vbuf[r + tq:r + tq + r] = vn_ref[0]

    qb = ATTN_QBLOCK
    kw = qb + 2 * r
    qi = lax.broadcasted_iota(jnp.int32, (qb, kw), 0)
    kj = lax.broadcasted_iota(jnp.int32, (qb, kw), 1)
    rel = kj - qi
    band = (rel >= 0) & (rel <= 2 * r)
    lane = lax.broadcasted_iota(jnp.int32, (qb, LANES), 1)
    low = lane < A_HEAD_DIM
    nt = (((1,), (1,)), ((), ()))

    for s in range(tq // qb):
        kpos = i * tq + (s * qb - r) + kj
        valid = band & (kpos >= 0) & (kpos < length)
        bias = jnp.where(valid, 0.0, NEG_BIG)
        rows = slice(s * qb, (s + 1) * qb)
        wrows = slice(s * qb, s * qb + kw)
        for p in range(A_WIDTH // LANES):
            cols = slice(p * LANES, (p + 1) * LANES)
            qp = q_ref[0, rows, cols]
            kwin = kbuf[wrows, cols]
            vwin = vbuf[wrows, cols]
            outs, lses = [], []
            for sel in (low, jnp.logical_not(low)):
                qm = jnp.where(sel, qp, jnp.zeros_like(qp))
                sc = lax.dot_general(qm, kwin, nt, preferred_element_type=F32) + bias
                m = jnp.max(sc, axis=-1, keepdims=True)
                pe = jnp.exp(sc - m)
                l = jnp.sum(pe, axis=-1, keepdims=True)
                pv = jnp.dot(pe.astype(BF16), vwin, preferred_element_type=F32)
                outs.append(pv / l)
                lses.append(jnp.broadcast_to(m + jnp.log(l), (qb, LANES)))
            o_ref[0, rows, cols] = jnp.where(low, outs[0], outs[1]).astype(o_ref.dtype)
            lse_ref[0, rows, cols] = jnp.where(low, lses[0], lses[1])


def _band_attention(qkv_view, batch, seq, dil):
    length = seq // dil
    tq = min(512, length)
    r = BAND_RADIUS
    view = qkv_view.reshape(batch, length, dil * 3 * A_WIDTH)
    nblk_h = length // r
    per = tq // r

    def main(j):
        return pl.BlockSpec((1, tq, A_WIDTH), lambda b, rr, i: (b, i, rr * 3 + j))

    def prev(j):
        return pl.BlockSpec((1, r, A_WIDTH), lambda b, rr, i: (b, jnp.maximum(i * per - 1, 0), rr * 3 + j))

    def nxt(j):
        return pl.BlockSpec((1, r, A_WIDTH),
                            lambda b, rr, i: (b, jnp.minimum((i + 1) * per, nblk_h - 1), rr * 3 + j))

    out_spec = pl.BlockSpec((1, tq, A_WIDTH), lambda b, rr, i: (b, i, rr))
    o, lse = pl.pallas_call(
        functools.partial(_band_attn_kernel, tq=tq, length=length),
        grid=(batch, dil, length // tq),
        in_specs=[main(0), prev(1), main(1), nxt(1), prev(2), main(2), nxt(2)],
        out_specs=[out_spec, out_spec],
        out_shape=[jax.ShapeDtypeStruct((batch, length, dil * A_WIDTH), BF16),
                   jax.ShapeDtypeStruct((batch, length, dil * A_WIDTH), F32)],
        scratch_shapes=[pltpu.VMEM((tq + 2 * r, A_WIDTH), BF16), pltpu.VMEM((tq + 2 * r, A_WIDTH), BF16)],
        compiler_params=_cparams("parallel", "parallel", "parallel"),
        name=f"band_attn_d{dil}",
    )(view, view, view, view, view, view, view)
    return o.reshape(batch * length, dil * A_WIDTH), lse.reshape(batch * length, dil * A_WIDTH)


def _even_out_kernel(o1, o4, o16, l1, l4, l16, gu_ref, up_ref, un_ref, x_ref, w_ref, cw_ref, g_ref, b_ref,
                     y_ref, so4, sl4, so16, sl16, *, tm, seq):
    i = pl.program_id(0)
    pos = (i * tm) % seq
    per = A_WIDTH // LANES
    for dil, o_ref, l_ref, so, sl in ((DILATED_PATTERNS[1][1], o4, l4, so4, sl4),
                                      (DILATED_PATTERNS[2][1], o16, l16, so16, sl16)):
        for r in range(dil):
            for s in range(per):
                cols = slice(r * A_WIDTH + s * LANES, r * A_WIDTH + (s + 1) * LANES)
                so[s, pl.ds(r, tm // dil, stride=dil), :] = o_ref[:, cols].astype(F32)
                sl[s, pl.ds(r, tm // dil, stride=dil), :] = l_ref[:, cols]
    slabs = []
    for s in range(per):
        cols = slice(s * LANES, (s + 1) * LANES)
        la, lb, lc = l1[:, cols], sl4[s], sl16[s]
        mx = jnp.maximum(jnp.maximum(la, lb), lc)
        ea, eb, ec = jnp.exp(la - mx), jnp.exp(lb - mx), jnp.exp(lc - mx)
        num = ea * o1[:, cols].astype(F32) + eb * so4[s] + ec * so16[s]
        slabs.append(num / (ea + eb + ec))
    a_out = jnp.concatenate(slabs, axis=1)

    gate_b = gu_ref[:, 0:B_WIDTH].astype(F32)
    u = gu_ref[:, B_WIDTH:2 * B_WIDTH].astype(F32)
    hrows = up_ref.shape[0]
    u_before = jnp.where(pos > 0, up_ref[hrows - 1:hrows, :].astype(F32), 0.0)
    u_after = jnp.where(pos + tm < seq, un_ref[0:1, :].astype(F32), 0.0)
    row = lax.broadcasted_iota(jnp.int32, (tm, B_WIDTH), 0)
    u_prev = jnp.where(row == 0, u_before, pltpu.roll(u, 1, axis=0))
    u_next = jnp.where(row == tm - 1, u_after, pltpu.roll(u, tm - 1, axis=0))
    conv = u_prev * cw_ref[0:1, :] + u * cw_ref[1:2, :] + u_next * cw_ref[2:3, :]
    b_out = gate_b * conv

    mix = jnp.dot(a_out.astype(BF16), w_ref[0:A_WIDTH, :], preferred_element_type=F32)
    mix = mix + jnp.dot(b_out.astype(BF16), w_ref[A_WIDTH:A_WIDTH + B_WIDTH, :], preferred_element_type=F32)
    y_ref[...] = _layer_norm(DEEPNORM_ALPHA * x_ref[...] + mix, g_ref[...], b_ref[...])


def _even_out(outs, lses, gu, x, w_out, conv_w, g, b, seq, tm):
    n = x.shape[0]
    hrows = 16
    nh = n // hrows
    per = tm // hrows
    d4, d16 = DILATED_PATTERNS[1][1], DILATED_PATTERNS[2][1]
    views = [pl.BlockSpec((tm // d, d * A_WIDTH), lambda i: (i, 0)) for d in (1, d4, d16)]
    full = lambda a: pl.BlockSpec(a.shape, lambda i: (0, 0))
    slab = pltpu.VMEM((A_WIDTH // LANES, tm, LANES), F32)
    return pl.pallas_call(
        functools.partial(_even_out_kernel, tm=tm, seq=seq),
        grid=(n // tm,),
        scratch_shapes=[slab, slab, slab, slab],
        in_specs=views + views + [
            pl.BlockSpec((tm, 2 * B_WIDTH), lambda i: (i, 0)),
            pl.BlockSpec((hrows, B_WIDTH), lambda i: (jnp.maximum(i * per - 1, 0), 1)),
            pl.BlockSpec((hrows, B_WIDTH), lambda i: (jnp.minimum((i + 1) * per, nh - 1), 1)),
            pl.BlockSpec((tm, D_MODEL), lambda i: (i, 0)),
            full(w_out), full(conv_w), full(g), full(b)],
        out_specs=pl.BlockSpec((tm, D_MODEL), lambda i: (i, 0)),
        out_shape=jax.ShapeDtypeStruct((n, D_MODEL), F32),
        compiler_params=_cparams("parallel"),
        name="even_out",
    )(*outs, *lses, gu, gu, gu, x, w_out, conv_w, g, b)


def _gla_gates(z, lb, tri2):
    f = lb + (1.0 - lb) * _sigmoid(z)
    lf2 = jnp.log(f) * LOG2_E
    hi = lf2.astype(BF16)
    lo = (lf2 - hi.astype(F32)).astype(BF16)
    cum = jnp.dot(tri2, jnp.concatenate([hi, lo], axis=0), preferred_element_type=F32)
    return 1.0 - f, cum


def _gla_scores(q, kk, cum, *, reverse):
    c = GLA_CHUNK
    sb = GLA_SUB
    dk = q.shape[1]
    nt = (((1,), (1,)), ((), ()))
    ti = lax.broadcasted_iota(jnp.int32, (c, c), 0)
    si = lax.broadcasted_iota(jnp.int32, (c, c), 1)
    causal = (si >= ti) if reverse else (ti >= si)

    parts = []
    for blk in range(c // sb):
        rows = slice(blk * sb, (blk + 1) * sb)
        if reverse:
            edge = (blk + 1) * sb
            ref = cum[edge:edge + 1, :] if edge < c else jnp.zeros((1, dk), F32)
            other = slice((blk + 1) * sb, c)
        else:
            edge = blk * sb - 1
            ref = cum[edge:edge + 1, :] if edge >= 0 else jnp.zeros((1, dk), F32)
            other = slice(0, blk * sb)
        qs = q[rows] * jnp.exp2(cum[rows] - ref)
        k_own = (kk[rows] * jnp.exp2(jnp.minimum(ref - cum[rows], GLA_EXP2_CLAMP))).astype(BF16)
        pieces = [k_own]
        n_other = other.stop - other.start
        if n_other:
            k_other = (kk[other] * jnp.exp2(ref - cum[other])).astype(BF16)
            pieces = [k_own, k_other] if reverse else [k_other, k_own]
        if n_other + sb < c:
            pad = jnp.zeros((c - n_other - sb, dk), BF16)
            pieces = [pad] + pieces if reverse else pieces + [pad]
        ks = jnp.concatenate(pieces, axis=0) if len(pieces) > 1 else pieces[0]
        parts.append(lax.dot_general(qs.astype(BF16), ks, nt, preferred_element_type=F32))
    return jnp.where(causal, jnp.concatenate(parts, axis=0), 0.0).astype(BF16)


def _gla_state_terms(q, kk, cum, *, reverse):
    c = GLA_CHUNK
    total = cum[0:1, :] if reverse else cum[c - 1:c, :]
    qe = (q * jnp.exp2(cum)).astype(BF16)
    kd = (kk * jnp.exp2(total - cum)).astype(BF16)
    return qe, kd, jnp.exp2(total)


def _gla_kernel(q_ref, zf_ref, zb_ref, v_ref, g_ref, lbl_ref, ng_ref, o_ref,
                acc_f, acc_b, qe_f, qe_b, kd_f, kd_b, et_f, et_b, att_f, att_b, ring_a, ring_b, *, seq):
    c = GLA_CHUNK
    nc = seq // c
    dk = HG_DIM
    nt = (((1,), (1,)), ((), ()))
    tn = (((0,), (0,)), ((), ()))
    l0 = lbl_ref[0:1, :]
    l1 = lbl_ref[1:2, :]
    mx = jnp.maximum(l0, l1)
    e0, e1 = jnp.exp(l0 - mx), jnp.exp(l1 - mx)
    lb = e0 / (e0 + e1)

    ti = lax.broadcasted_iota(jnp.int32, (c, 2 * c), 0)
    si = lax.broadcasted_iota(jnp.int32, (c, 2 * c), 1) & (c - 1)
    tri_f = jnp.where(ti >= si, 1.0, 0.0).astype(BF16)
    tri_b = jnp.where(si >= ti, 1.0, 0.0).astype(BF16)

    dirs = ((zf_ref, tri_f, att_f, qe_f, kd_f, et_f, False),
            (zb_ref, tri_b, att_b, qe_b, kd_b, et_b, True))
    grp = GLA_INTRA_UNROLL
    n_groups = nc // grp
    chains = [(u, d) for u in range(grp) for d in range(2)]

    def park_gates(group, ring):
        for k, (u, d) in enumerate(chains):
            rows = pl.ds(pl.multiple_of((group * grp + u) * c, c), c)
            kk, cum = _gla_gates(dirs[d][0][0, rows, :].astype(F32), lb, dirs[d][1])
            ring[0, k * c:(k + 1) * c, :] = kk
            ring[1, k * c:(k + 1) * c, :] = cum

    def scores_from_ring(group, ring):
        for k, (u, d) in enumerate(chains):
            ci = group * grp + u
            rows = pl.ds(pl.multiple_of(ci * c, c), c)
            q = q_ref[0, rows, :].astype(F32)
            kk = ring[0, k * c:(k + 1) * c, :]
            cum = ring[1, k * c:(k + 1) * c, :]
            dirs[d][2][rows, :] = _gla_scores(q, kk, cum, reverse=dirs[d][6])
            qe, kd, et = _gla_state_terms(q, kk, cum, reverse=dirs[d][6])
            dirs[d][3][rows, :] = qe
            dirs[d][4][rows, :] = kd
            dirs[d][5][pl.ds(ci, 1), :] = et

    park_gates(0, ring_a)
    park_gates(1, ring_b)

    def intra(j, _):
        scores_from_ring(2 * j, ring_a)
        scores_from_ring(2 * j + 1, ring_b)
        park_gates(jnp.minimum(2 * j + 2, n_groups - 1), ring_a)
        park_gates(jnp.minimum(2 * j + 3, n_groups - 1), ring_b)
        return 0

    lax.fori_loop(0, n_groups // 2, intra, 0)

    def scan(j, carry):
        sf, sr = carry
        cfs = [j * GLA_SCAN_UNROLL + u for u in range(GLA_SCAN_UNROLL)]
        crs = [nc - 1 - cf for cf in cfs]
        rfs = [pl.ds(pl.multiple_of(cf * c, c), c) for cf in cfs]
        rrs = [pl.ds(pl.multiple_of(cr * c, c), c) for cr in crs]
        upd_f = [lax.dot_general(v_ref[0, r, :], kd_f[r, :], tn, preferred_element_type=F32) for r in rfs]
        upd_r = [lax.dot_general(v_ref[0, r, :], kd_b[r, :], tn, preferred_element_type=F32) for r in rrs]
        loc_f = [jnp.dot(att_f[r, :], v_ref[0, r, :], preferred_element_type=F32) for r in rfs]
        loc_r = [jnp.dot(att_b[r, :], v_ref[0, r, :], preferred_element_type=F32) for r in rrs]
        sfs, srs = [sf], [sr]
        for u in range(GLA_SCAN_UNROLL):
            sfs.append(sfs[-1] * et_f[pl.ds(cfs[u], 1), :] + upd_f[u])
            srs.append(srs[-1] * et_b[pl.ds(crs[u], 1), :] + upd_r[u])
        for u in range(GLA_SCAN_UNROLL):
            acc_f[rfs[u], :] = loc_f[u] + lax.dot_general(qe_f[rfs[u], :], sfs[u].astype(BF16), nt,
                                                          preferred_element_type=F32)
            acc_b[rrs[u], :] = loc_r[u] + lax.dot_general(qe_b[rrs[u], :], srs[u].astype(BF16), nt,
                                                          preferred_element_type=F32)
        return sfs[-1], srs[-1]

    zero = jnp.zeros((dk, dk), F32)
    lax.fori_loop(0, nc // GLA_SCAN_UNROLL, scan, (zero, zero))

    blk = 512
    ng = ng_ref[...]

    def fin(j, _):
        rows = pl.ds(pl.multiple_of(j * blk, blk), blk)
        o = acc_f[rows, :] + acc_b[rows, :]
        o = o * lax.rsqrt(jnp.mean(o * o, axis=-1, keepdims=True) + RMS_EPS) * ng
        g = g_ref[0, rows, :].astype(F32)
        o_ref[0, rows, :] = (o * (g * _sigmoid(g))).astype(o_ref.dtype)
        return 0

    lax.fori_loop(0, seq // blk, fin, 0)


def _gla(proj, lb_logits, norm_g, batch, seq):
    view = proj.reshape(batch, seq, 5 * D_MODEL)

    def col(seg):
        return pl.BlockSpec((1, seq, HG_DIM), lambda b, h: (b, 0, seg * HG_HEADS + h))

    return pl.pallas_call(
        functools.partial(_gla_kernel, seq=seq),
        grid=(batch, HG_HEADS),
        in_specs=[col(0), col(1), col(2), col(3), col(4),
                  pl.BlockSpec((DEPTH, HG_DIM), lambda b, h: (0, h)),
                  pl.BlockSpec((1, HG_DIM), lambda b, h: (0, h))],
        out_specs=pl.BlockSpec((1, seq, HG_DIM), lambda b, h: (b, 0, h)),
        out_shape=jax.ShapeDtypeStruct((batch, seq, D_MODEL), BF16),
        scratch_shapes=[pltpu.VMEM((seq, HG_DIM), F32), pltpu.VMEM((seq, HG_DIM), F32),
                        pltpu.VMEM((seq, HG_DIM), BF16), pltpu.VMEM((seq, HG_DIM), BF16),
                        pltpu.VMEM((seq, HG_DIM), BF16), pltpu.VMEM((seq, HG_DIM), BF16),
                        pltpu.VMEM((seq // GLA_CHUNK, HG_DIM), F32),
                        pltpu.VMEM((seq // GLA_CHUNK, HG_DIM), F32),
                        pltpu.VMEM((seq, GLA_CHUNK), BF16), pltpu.VMEM((seq, GLA_CHUNK), BF16),
                        pltpu.VMEM((2, 2 * GLA_INTRA_UNROLL * GLA_CHUNK, HG_DIM), F32),
                        pltpu.VMEM((2, 2 * GLA_INTRA_UNROLL * GLA_CHUNK, HG_DIM), F32)],
        compiler_params=_cparams("parallel", "parallel"),
        name="gla",
    )(view, view, view, view, view, lb_logits, norm_g).reshape(batch * seq, D_MODEL)


def _proj_ln_kernel(a_ref, w_ref, x_ref, g_ref, b_ref, y_ref):
    mix = jnp.dot(a_ref[...], w_ref[...], preferred_element_type=F32)
    y_ref[...] = _layer_norm(DEEPNORM_ALPHA * x_ref[...] + mix, g_ref[...], b_ref[...])


def _proj_ln(a, w, x, g, b, tm):
    n = x.shape[0]
    full = lambda t: pl.BlockSpec(t.shape, lambda i: (0, 0))
    row = pl.BlockSpec((tm, D_MODEL), lambda i: (i, 0))
    return pl.pallas_call(
        _proj_ln_kernel,
        grid=(n // tm,),
        in_specs=[row, full(w), row, full(g), full(b)],
        out_specs=row,
        out_shape=jax.ShapeDtypeStruct((n, D_MODEL), F32),
        compiler_params=_cparams("parallel"),
        name="proj_ln",
    )(a, w, x, g, b)


ROUTER_ROWS = 64
ROUTE_OUT_ROWS = 8


def _first_index_of(vals, target, n_rows):
    idx = lax.broadcasted_iota(jnp.int32, vals.shape, 0)
    return jnp.min(jnp.where(vals == target, idx, n_rows), axis=0, keepdims=True)


def _xattn_kernel(x_ref, wq_ref, kv_ref, wo_ref, g_ref, b_ref, wr_ref, br_ref, y_ref, yb_ref, route_ref):
    x = x_ref[...]
    q = jnp.dot(x.astype(BF16), wq_ref[...], preferred_element_type=F32) * (XA_HEAD_DIM ** -0.5)
    qb = q.astype(BF16)
    nt = (((1,), (1,)), ((), ()))
    heads = []
    for h in range(XA_HEADS):
        cols = slice(h * XA_HEAD_DIM, (h + 1) * XA_HEAD_DIM)
        k = kv_ref[0, :, cols]
        v = kv_ref[0, :, D_MODEL + h * XA_HEAD_DIM:D_MODEL + (h + 1) * XA_HEAD_DIM]
        sc = lax.dot_general(qb[:, cols], k, nt, preferred_element_type=F32)
        m = jnp.max(sc, axis=-1, keepdims=True)
        pe = jnp.exp(sc - m)
        p = pe / jnp.sum(pe, axis=-1, keepdims=True)
        heads.append(jnp.dot(p.astype(BF16), v, preferred_element_type=F32).astype(BF16))
    o = jnp.concatenate(heads, axis=1)
    xa = jnp.dot(o, wo_ref[...], preferred_element_type=F32)
    y = _layer_norm(DEEPNORM_ALPHA * x + xa, g_ref[...], b_ref[...])
    y_ref[...] = y
    yb_ref[...] = y.astype(BF16)

    lg = lax.dot_general(wr_ref[...], y.astype(BF16), nt, preferred_element_type=F32) + br_ref[:, 0:1]
    gl = lg[0:N_GROUPS, :]
    gmax = jnp.max(gl, axis=0, keepdims=True)
    g_w = 1.0 / jnp.sum(jnp.exp(gl - gmax), axis=0, keepdims=True)
    g_sel = _first_index_of(gl, gmax, N_GROUPS)
    el = jnp.zeros((EXPERTS_PER_GROUP, gl.shape[1]), F32)
    for grp in range(N_GROUPS):
        rows = slice(8 + grp * EXPERTS_PER_GROUP, 8 + (grp + 1) * EXPERTS_PER_GROUP)
        el = el + jnp.where(g_sel == grp, lg[rows, :], 0.0)
    m1 = jnp.max(el, axis=0, keepdims=True)
    i1 = _first_index_of(el, m1, EXPERTS_PER_GROUP)
    eidx = lax.broadcasted_iota(jnp.int32, el.shape, 0)
    el2 = jnp.where(eidx == i1, -jnp.inf, el)
    m2 = jnp.max(el2, axis=0, keepdims=True)
    i2 = _first_index_of(el2, m2, EXPERTS_PER_GROUP)
    e2 = jnp.exp(m2 - m1)
    den = 1.0 + e2
    w1 = g_w / den
    w2 = g_w * e2 / den
    base = g_sel * EXPERTS_PER_GROUP
    zero = jnp.zeros_like(w1)
    route_ref[...] = jnp.concatenate(
        [(base + i1).astype(F32), (base + i2).astype(F32), w1, w2, zero, zero, zero, zero], axis=0)


def _xattn(x, kv, wq, wo, g, b, wr, br, seq, tm):
    n = x.shape[0]
    spt = seq // tm
    full = lambda t: pl.BlockSpec(t.shape, lambda i: (0, 0))
    row = pl.BlockSpec((tm, D_MODEL), lambda i: (i, 0))
    return pl.pallas_call(
        _xattn_kernel,
        grid=(n // tm,),
        in_specs=[row, full(wq),
                  pl.BlockSpec((1,) + kv.shape[1:], lambda i: (i // spt, 0, 0)),
                  full(wo), full(g), full(b), full(wr), full(br)],
        out_specs=[row, row, pl.BlockSpec((ROUTE_OUT_ROWS, tm), lambda i: (0, i))],
        out_shape=[jax.ShapeDtypeStruct((n, D_MODEL), F32),
                   jax.ShapeDtypeStruct((n, D_MODEL), BF16),
                   jax.ShapeDtypeStruct((ROUTE_OUT_ROWS, n), F32)],
        compiler_params=_cparams("parallel"),
        name="xattn_router",
    )(x, wq, kv, wo, g, b, wr, br)


def _expert_kernel(te_ref, nu_ref, xs_ref, w1_ref, w3_ref, w2_ref, ys_ref):
    j = pl.program_id(0)

    @pl.when(j < nu_ref[0])
    def _():
        xs = xs_ref[...]
        h1 = jnp.dot(xs, w1_ref[0].astype(BF16), preferred_element_type=F32)
        h3 = jnp.dot(xs, w3_ref[0].astype(BF16), preferred_element_type=F32)
        hid = h1 * _sigmoid(h1) * h3
        ys_ref[...] = jnp.dot(hid.astype(BF16), w2_ref[0].astype(BF16),
                              preferred_element_type=F32).astype(ys_ref.dtype)

    @pl.when(j >= nu_ref[0])
    def _():
        ys_ref[...] = jnp.zeros_like(ys_ref)


def _experts(xs, tile_expert, n_used, w1, w3, w2, tm):
    mp = xs.shape[0]
    w_in = pl.BlockSpec((1, D_MODEL, EXPERT_FF), lambda j, te, nu: (te[j], 0, 0))
    grid_spec = pltpu.PrefetchScalarGridSpec(
        num_scalar_prefetch=2,
        grid=(mp // tm,),
        in_specs=[pl.BlockSpec((tm, D_MODEL), lambda j, te, nu: (j, 0)), w_in, w_in,
                  pl.BlockSpec((1, EXPERT_FF, D_MODEL), lambda j, te, nu: (te[j], 0, 0))],
        out_specs=pl.BlockSpec((tm, D_MODEL), lambda j, te, nu: (j, 0)),
    )
    return pl.pallas_call(
        _expert_kernel,
        grid_spec=grid_spec,
        out_shape=jax.ShapeDtypeStruct((mp, D_MODEL), BF16),
        compiler_params=_cparams("arbitrary"),
        name="experts",
    )(tile_expert, n_used, xs, w1, w3, w2)


def _moe_ln_kernel(x_ref, y0_ref, y1_ref, gate_ref, g_ref, b_ref, o_ref):
    ff = gate_ref[:, 0:1] * y0_ref[...].astype(F32) + gate_ref[:, 1:2] * y1_ref[...].astype(F32)
    o_ref[...] = _layer_norm(DEEPNORM_ALPHA * x_ref[...] + ff, g_ref[...], b_ref[...])


def _moe_ln(x, y0, y1, gates, g, b, tm):
    n = x.shape[0]
    full = lambda t: pl.BlockSpec(t.shape, lambda i: (0, 0))
    row = pl.BlockSpec((tm, D_MODEL), lambda i: (i, 0))
    return pl.pallas_call(
        _moe_ln_kernel,
        grid=(n // tm,),
        in_specs=[row, row, row, pl.BlockSpec((tm, 2), lambda i: (i, 0)), full(g), full(b)],
        out_specs=row,
        out_shape=jax.ShapeDtypeStruct((n, D_MODEL), F32),
        compiler_params=_cparams("parallel"),
        name="moe_ln",
    )(x, y0, y1, gates, g, b)


def _dispatch_plan(ids, n, tm_e):
    n_asg = 2 * n
    mp = n_asg + N_EXPERTS * tm_e
    order = jnp.argsort(ids, stable=True).astype(jnp.int32)
    pos = jnp.argsort(order).astype(jnp.int32)
    onehot = (ids[:, None] == jnp.arange(N_EXPERTS, dtype=jnp.int32)[None, :]).astype(jnp.int32)
    counts = jnp.sum(onehot, axis=0)
    dense_start = jnp.cumsum(counts) - counts
    padded = ((counts + tm_e - 1) // tm_e) * tm_e
    row_end = jnp.cumsum(padded)
    row_start = row_end - padded
    row_of_asg = pos + jnp.sum(onehot * (row_start - dense_start)[None, :], axis=1)
    tile_start = jnp.arange(mp // tm_e, dtype=jnp.int32) * tm_e
    tile_expert = jnp.minimum(jnp.sum((tile_start[:, None] >= row_end[None, :]).astype(jnp.int32), axis=1),
                              N_EXPERTS - 1)
    shift = (dense_start - row_start)[tile_expert]
    src = (tile_start + shift)[:, None] + jnp.arange(tm_e, dtype=jnp.int32)[None, :]
    asg_of_row = order[jnp.clip(src.reshape(mp), 0, n_asg - 1)]
    tok_of_row = jnp.where(asg_of_row >= n, asg_of_row - n, asg_of_row)
    n_used = (row_end[-1] // tm_e).astype(jnp.int32).reshape(1)
    return tok_of_row, row_of_asg, tile_expert.astype(jnp.int32), n_used


def _moe_steps(x, xb, route, w1, w3, w2, g, b, tm_e, tm):
    n = x.shape[0]
    ids = route[0:2].astype(jnp.int32).reshape(2 * n)
    tok_of_row, row_of_asg, tile_expert, n_used = _dispatch_plan(ids, n, tm_e)
    xs = xb[tok_of_row]
    yield
    ys = _experts(xs, tile_expert, n_used, w1, w3, w2, tm_e)
    y0 = ys[row_of_asg[:n]]
    y1 = ys[row_of_asg[n:]]
    yield
    return _moe_ln(x, y0, y1, route[2:4].T, g, b, tm)


def _router_weights(w_group, b_group, w_expert, b_expert):
    wr = jnp.zeros((ROUTER_ROWS, D_MODEL), F32)
    wr = wr.at[0:N_GROUPS].set(w_group.T).at[8:8 + N_EXPERTS].set(w_expert.T)
    br = jnp.zeros((ROUTER_ROWS, LANES), F32)
    br = br.at[0:N_GROUPS, :].set(b_group[:, None]).at[8:8 + N_EXPERTS, :].set(b_expert[:, None])
    return wr.astype(BF16), br


def _trunk_steps(x3, mem3, p):
    batch, seq, _ = x3.shape
    n = batch * seq
    x = x3.reshape(n, D_MODEL)
    mem = mem3.reshape(batch * mem3.shape[1], D_MODEL)
    tm = 512
    tables = _rope_tables(seq)
    for layer in range(DEPTH):
        j = layer // 2
        row = lambda a: a.reshape(1, D_MODEL)
        if layer % 2 == 0:
            *views, gu = _even_proj(x, p["ev_w_in"][j], tables, seq, tm)
            res = [_band_attention(v, batch, seq, dil) for v, (_, dil) in zip(views, DILATED_PATTERNS)]
            x = _even_out([o for o, _ in res], [l for _, l in res], gu, x, p["ev_w_out"][j],
                          p["ev_conv_w"][j], row(p["ln_g"][layer, 0]), row(p["ln_b"][layer, 0]), seq, tm)
        else:
            proj = _matmul(x, p["od_w_in"][j], BF16, tm, D_MODEL)
            o = _gla(proj, p["lb_logits"], p["od_norm_g"][j].reshape(1, D_MODEL), batch, seq)
            x = _proj_ln(o, p["od_w_out"][j], x, row(p["ln_g"][layer, 0]), row(p["ln_b"][layer, 0]), tm)
        kv = _matmul(mem, p["xa_w_kv"][layer], BF16, 256, D_MODEL).reshape(batch, mem3.shape[1], 2 * D_MODEL)
        wr, br = _router_weights(p["moe_w_group"][layer], p["moe_b_group"][layer],
                                 p["moe_w_expert"][layer], p["moe_b_expert"][layer])
        x, xb, route = _xattn(x, kv, p["xa_w_q"][layer], p["xa_w_out"][layer],
                              row(p["ln_g"][layer, 1]), row(p["ln_b"][layer, 1]), wr, br, seq, tm)
        x = yield from _moe_steps(x, xb, route, p["moe_w1"][layer], p["moe_w3"][layer], p["moe_w2"][layer],
                                  row(p["ln_g"][layer, 2]), row(p["ln_b"][layer, 2]), 512, tm)
    return x.reshape(batch, seq, D_MODEL)


def _run_interleaved(generators):
    results = [None] * len(generators)
    live = list(range(len(generators)))
    while live:
        for k in list(live):
            try:
                next(generators[k])
            except StopIteration as stop:
                results[k] = stop.value
                live.remove(k)
    return results


def _trunk(x3, mem3, p):
    return _run_interleaved([_trunk_steps(x3, mem3, p)])[0]


def kernel(x_prompt, x_sample, mem_prompt, mem_sample, ev_w_in, ev_conv_w, ev_w_out, od_w_in, lb_logits,
           od_norm_g, od_w_out, xa_w_q, xa_w_kv, xa_w_out, moe_w_group, moe_b_group, moe_w_expert,
           moe_b_expert, moe_w1, moe_w3, moe_w2, ln_g, ln_b):
    ff = moe_w1.shape[-1]
    p = dict(
        ev_w_in=ev_w_in.astype(BF16), ev_conv_w=ev_conv_w, ev_w_out=ev_w_out.astype(BF16),
        od_w_in=od_w_in.astype(BF16), lb_logits=lb_logits, od_norm_g=od_norm_g,
        od_w_out=od_w_out.astype(BF16), xa_w_q=xa_w_q.astype(BF16), xa_w_kv=xa_w_kv.astype(BF16),
        xa_w_out=xa_w_out.astype(BF16), moe_w_group=moe_w_group, moe_b_group=moe_b_group,
        moe_w_expert=moe_w_expert, moe_b_expert=moe_b_expert,
        moe_w1=moe_w1.reshape(DEPTH, N_EXPERTS, D_MODEL, ff),
        moe_w3=moe_w3.reshape(DEPTH, N_EXPERTS, D_MODEL, ff),
        moe_w2=moe_w2.reshape(DEPTH, N_EXPERTS, ff, D_MODEL),
        ln_g=ln_g, ln_b=ln_b)
    y_prompt, y_sample = _run_interleaved([_trunk_steps(x_prompt, mem_prompt, p),
                                           _trunk_steps(x_sample, mem_sample, p)])
    return y_prompt, y_sample
```

```python
import functools
import math

import jax
import jax.numpy as jnp
from jax import lax
from jax.experimental import pallas as pl
from jax.experimental.pallas import tpu as pltpu
from jax.experimental.pallas import tpu_sc as plsc

F32 = jnp.float32
BF16 = jnp.bfloat16

D_MODEL = 1024
DEPTH = 2
A_HEADS = 8
A_HEAD_DIM = 64
A_WIDTH = A_HEADS * A_HEAD_DIM
DILATED_PATTERNS = ((128, 1), (512, 4), (2048, 16))
ROPE_THETA = 500000.0
ROPE_DIM = A_HEAD_DIM // 4
B_WIDTH = D_MODEL // 2
CONV_WIDTH = 3
HG_HEADS = 8
HG_DIM = D_MODEL // HG_HEADS
XA_HEADS = 4
XA_HEAD_DIM = D_MODEL // XA_HEADS
N_GROUPS = 4
EXPERTS_PER_GROUP = 8
N_EXPERTS = N_GROUPS * EXPERTS_PER_GROUP
EXPERT_FF = D_MODEL // 4
LN_EPS = 1e-5
RMS_EPS = 1e-6
DEEPNORM_ALPHA = (2 * DEPTH) ** 0.25

LANES = 128
BAND_RADIUS = 64
ATTN_QBLOCK = 128
NEG_BIG = -1e30
GLA_CHUNK = 64
GLA_SUB = 16
GLA_EXP2_CLAMP = 100.0
GLA_INTRA_UNROLL = 2
GLA_SCAN_UNROLL = 8
LOG2_E = 1.4426950408889634
VMEM_LIMIT = 56 * 1024 * 1024


def _cparams(*sem):
    return pltpu.CompilerParams(dimension_semantics=sem, vmem_limit_bytes=VMEM_LIMIT)


def _layer_norm(y, g, b):
    mu = jnp.mean(y, axis=-1, keepdims=True)
    d = y - mu
    var = jnp.mean(d * d, axis=-1, keepdims=True)
    return d * lax.rsqrt(var + LN_EPS) * g + b


def _sigmoid(z):
    return 1.0 / (1.0 + jnp.exp(-z))


def _pack_bf16_pair(y):
    w = y.shape[1] // 2
    hi = lax.bitcast_convert_type(y[:, :w].astype(BF16).astype(F32), jnp.int32)
    lo = lax.bitcast_convert_type(y[:, w:].astype(BF16).astype(F32), jnp.int32)
    return hi | lax.shift_right_logical(lo, 16)


def _unpack_bf16_pair(p):
    hi = lax.bitcast_convert_type(p & jnp.int32(-65536), F32)
    lo = lax.bitcast_convert_type(lax.shift_left(p, 16), F32)
    return hi, lo


def _mm_kernel(x_ref, w_ref, o_ref, *, chunk):
    xb = x_ref[...].astype(BF16)
    for c in range(w_ref.shape[1] // chunk):
        cols = slice(c * chunk, (c + 1) * chunk)
        o_ref[:, cols] = jnp.dot(xb, w_ref[:, cols], preferred_element_type=F32).astype(o_ref.dtype)


def _matmul(x, w, out_dtype, tm, chunk):
    n, k = x.shape
    m = w.shape[1]
    return pl.pallas_call(
        functools.partial(_mm_kernel, chunk=chunk),
        grid=(n // tm,),
        in_specs=[pl.BlockSpec((tm, k), lambda i: (i, 0)), pl.BlockSpec((k, m), lambda i: (0, 0))],
        out_specs=pl.BlockSpec((tm, m), lambda i: (i, 0)),
        out_shape=jax.ShapeDtypeStruct((n, m), out_dtype),
        compiler_params=_cparams("parallel"),
        name="matmul",
    )(x, w)


def _rope_tables(seq):
    half = ROPE_DIM // 2
    inv_freq = jnp.exp(-math.log(ROPE_THETA) * jnp.arange(half, dtype=F32) * (2.0 / ROPE_DIM))
    ang = jnp.arange(seq, dtype=F32)[:, None] * inv_freq[None, :]
    cos, sin = jnp.cos(ang), jnp.sin(ang)
    ones = jnp.ones((seq, A_HEAD_DIM - ROPE_DIM), F32)
    zeros = jnp.zeros((seq, A_HEAD_DIM - ROPE_DIM), F32)
    zh = jnp.zeros((seq, half), F32)
    c = jnp.concatenate([cos, cos, ones], -1)
    s_up = jnp.concatenate([-sin, zh, zeros], -1)
    s_dn = jnp.concatenate([zh, sin, zeros], -1)
    rep = LANES // A_HEAD_DIM
    return tuple(jnp.tile(t, (1, rep)) for t in (c, s_up, s_dn))


def _even_proj_kernel(x_ref, w_ref, c_ref, su_ref, sd_ref, qkv_ref, qkv4_ref, qkv16_ref, gu_ref, slab_ref):
    tm = x_ref.shape[0]
    xb = x_ref[...].astype(BF16)
    rep = A_WIDTH // LANES
    half = ROPE_DIM // 2
    c = jnp.tile(c_ref[...], (1, rep))
    su = jnp.tile(su_ref[...], (1, rep))
    sd = jnp.tile(sd_ref[...], (1, rep))

    def proj(j):
        return jnp.dot(xb, w_ref[:, j * A_WIDTH:(j + 1) * A_WIDTH], preferred_element_type=F32)

    def rope(t):
        up = pltpu.roll(t, A_WIDTH - half, axis=1)
        dn = pltpu.roll(t, half, axis=1)
        return t * c + up * su + dn * sd

    qkv = (rope(proj(0)) * (A_HEAD_DIM ** -0.5), rope(proj(1)), proj(2))
    per = A_WIDTH // LANES
    for j, part in enumerate(qkv):
        qkv_ref[:, j * A_WIDTH:(j + 1) * A_WIDTH] = part.astype(BF16)
        for s in range(per):
            slab_ref[j * per + s] = part[:, s * LANES:(s + 1) * LANES]
    for dil, out_ref in ((DILATED_PATTERNS[1][1], qkv4_ref), (DILATED_PATTERNS[2][1], qkv16_ref)):
        for r in range(dil):
            for s in range(3 * per):
                val = slab_ref[s, pl.ds(r, tm // dil, stride=dil), :]
                col = r * 3 * A_WIDTH + s * LANES
                out_ref[:, col:col + LANES] = val.astype(BF16)
    gu_ref[:, 0:B_WIDTH] = proj(3).astype(BF16)
    gu_ref[:, B_WIDTH:2 * B_WIDTH] = (proj(4) * proj(5)).astype(BF16)


def _even_proj(x, w_in, tables, seq, tm):
    n = x.shape[0]
    spt = seq // tm
    tab_spec = pl.BlockSpec((tm, LANES), lambda i: (i % spt, 0))
    d4, d16 = DILATED_PATTERNS[1][1], DILATED_PATTERNS[2][1]
    width = 3 * A_WIDTH
    return pl.pallas_call(
        _even_proj_kernel,
        grid=(n // tm,),
        in_specs=[pl.BlockSpec((tm, D_MODEL), lambda i: (i, 0)),
                  pl.BlockSpec(w_in.shape, lambda i: (0, 0)),
                  tab_spec, tab_spec, tab_spec],
        out_specs=[pl.BlockSpec((tm, width), lambda i: (i, 0)),
                   pl.BlockSpec((tm // d4, d4 * width), lambda i: (i, 0)),
                   pl.BlockSpec((tm // d16, d16 * width), lambda i: (i, 0)),
                   pl.BlockSpec((tm, 2 * B_WIDTH), lambda i: (i, 0))],
        out_shape=[jax.ShapeDtypeStruct((n, width), BF16),
                   jax.ShapeDtypeStruct((n // d4, d4 * width), BF16),
                   jax.ShapeDtypeStruct((n // d16, d16 * width), BF16),
                   jax.ShapeDtypeStruct((n, 2 * B_WIDTH), BF16)],
        scratch_shapes=[pltpu.VMEM((width // LANES, tm, LANES), F32)],
        compiler_params=_cparams("parallel"),
        name="even_proj",
    )(x, w_in, *tables)


def _band_attn_kernel(q_ref, kp_ref, km_ref, kn_ref, vp_ref, vm_ref, vn_ref, o_ref, lse_ref,
                      kbuf, vbuf, *, tq, length):
    i = pl.program_id(2)
    r = BAND_RADIUS
    kbuf[0:r] = kp_ref[0]
    kbuf[r:r + tq] = km_ref[0]
    kbuf[r + tq:r + tq + r] = kn_ref[0]
    vbuf[0:r] = vp_ref[0]
    vbuf[r:r + tq] = vm_ref[0]
    vbuf[r + tq:r + tq + r] = vn_ref[0]

    qb = ATTN_QBLOCK
    kw = qb + 2 * r
    qi = lax.broadcasted_iota(jnp.int32, (qb, kw), 0)
    kj = lax.broadcasted_iota(jnp.int32, (qb, kw), 1)
    rel = kj - qi
    band = (rel >= 0) & (rel <= 2 * r)
    lane = lax.broadcasted_iota(jnp.int32, (qb, LANES), 1)
    low = lane < A_HEAD_DIM
    nt = (((1,), (1,)), ((), ()))

    for s in range(tq // qb):
        kpos = i * tq + (s * qb - r) + kj
        valid = band & (kpos >= 0) & (kpos < length)
        bias = jnp.where(valid, 0.0, NEG_BIG)
        rows = slice(s * qb, (s + 1) * qb)
        wrows = slice(s * qb, s * qb + kw)
        for p in range(A_WIDTH // LANES):
            cols = slice(p * LANES, (p + 1) * LANES)
            qp = q_ref[0, rows, cols]
            kwin = kbuf[wrows, cols]
            vwin = vbuf[wrows, cols]
            outs, lses = [], []
            for sel in (low, jnp.logical_not(low)):
                qm = jnp.where(sel, qp, jnp.zeros_like(qp))
                sc = lax.dot_general(qm, kwin, nt, preferred_element_type=F32) + bias
                m = jnp.max(sc, axis=-1, keepdims=True)
                pe = jnp.exp(sc - m)
                l = jnp.sum(pe, axis=-1, keepdims=True)
                pv = jnp.dot(pe.astype(BF16), vwin, preferred_element_type=F32)
                outs.append(pv / l)
                lses.append(jnp.broadcast_to(m + jnp.log(l), (qb, LANES)))
            o_ref[0, rows, cols] = jnp.where(low, outs[0], outs[1]).astype(o_ref.dtype)
            lse_ref[0, rows, cols] = jnp.where(low, lses[0], lses[1])


def _band_attention(qkv_view, batch, seq, dil):
    length = seq // dil
    tq = min(512, length)
    r = BAND_RADIUS
    view = qkv_view.reshape(batch, length, dil * 3 * A_WIDTH)
    nblk_h = length // r
    per = tq // r

    def main(j):
        return pl.BlockSpec((1, tq, A_WIDTH), lambda b, rr, i: (b, i, rr * 3 + j))

    def prev(j):
        return pl.BlockSpec((1, r, A_WIDTH), lambda b, rr, i: (b, jnp.maximum(i * per - 1, 0), rr * 3 + j))

    def nxt(j):
        return pl.BlockSpec((1, r, A_WIDTH),
                            lambda b, rr, i: (b, jnp.minimum((i + 1) * per, nblk_h - 1), rr * 3 + j))

    out_spec = pl.BlockSpec((1, tq, A_WIDTH), lambda b, rr, i: (b, i, rr))
    o, lse = pl.pallas_call(
        functools.partial(_band_attn_kernel, tq=tq, length=length),
        grid=(batch, dil, length // tq),
        in_specs=[main(0), prev(1), main(1), nxt(1), prev(2), main(2), nxt(2)],
        out_specs=[out_spec, out_spec],
        out_shape=[jax.ShapeDtypeStruct((batch, length, dil * A_WIDTH), BF16),
                   jax.ShapeDtypeStruct((batch, length, dil * A_WIDTH), F32)],
        scratch_shapes=[pltpu.VMEM((tq + 2 * r, A_WIDTH), BF16), pltpu.VMEM((tq + 2 * r, A_WIDTH), BF16)],
        compiler_params=_cparams("parallel", "parallel", "parallel"),
        name=f"band_attn_d{dil}",
    )(view, view, view, view, view, view, view)
    return o.reshape(batch * length, dil * A_WIDTH), lse.reshape(batch * length, dil * A_WIDTH)


def _even_out_kernel(o1, o4, o16, l1, l4, l16, gu_ref, up_ref, un_ref, x_ref, w_ref, cw_ref, g_ref, b_ref,
                     y_ref, so4, sl4, so16, sl16, *, tm, seq):
    i = pl.program_id(0)
    pos = (i * tm) % seq
    per = A_WIDTH // LANES
    for dil, o_ref, l_ref, so, sl in ((DILATED_PATTERNS[1][1], o4, l4, so4, sl4),
                                      (DILATED_PATTERNS[2][1], o16, l16, so16, sl16)):
        for r in range(dil):
            for s in range(per):
                cols = slice(r * A_WIDTH + s * LANES, r * A_WIDTH + (s + 1) * LANES)
                so[s, pl.ds(r, tm // dil, stride=dil), :] = o_ref[:, cols].astype(F32)
                sl[s, pl.ds(r, tm // dil, stride=dil), :] = l_ref[:, cols]
    slabs = []
    for s in range(per):
        cols = slice(s * LANES, (s + 1) * LANES)
        la, lb, lc = l1[:, cols], sl4[s], sl16[s]
        mx = jnp.maximum(jnp.maximum(la, lb), lc)
        ea, eb, ec = jnp.exp(la - mx), jnp.exp(lb - mx), jnp.exp(lc - mx)
        num = ea * o1[:, cols].astype(F32) + eb * so4[s] + ec * so16[s]
        slabs.append(num / (ea + eb + ec))
    a_out = jnp.concatenate(slabs, axis=1)

    gate_b = gu_ref[:, 0:B_WIDTH].astype(F32)
    u = gu_ref[:, B_WIDTH:2 * B_WIDTH].astype(F32)
    hrows = up_ref.shape[0]
    u_before = jnp.where(pos > 0, up_ref[hrows - 1:hrows, :].astype(F32), 0.0)
    u_after = jnp.where(pos + tm < seq, un_ref[0:1, :].astype(F32), 0.0)
    row = lax.broadcasted_iota(jnp.int32, (tm, B_WIDTH), 0)
    u_prev = jnp.where(row == 0, u_before, pltpu.roll(u, 1, axis=0))
    u_next = jnp.where(row == tm - 1, u_after, pltpu.roll(u, tm - 1, axis=0))
    conv = u_prev * cw_ref[0:1, :] + u * cw_ref[1:2, :] + u_next * cw_ref[2:3, :]
    b_out = gate_b * conv

    mix = jnp.dot(a_out.astype(BF16), w_ref[0:A_WIDTH, :], preferred_element_type=F32)
    mix = mix + jnp.dot(b_out.astype(BF16), w_ref[A_WIDTH:A_WIDTH + B_WIDTH, :], preferred_element_type=F32)
    y_ref[...] = _layer_norm(DEEPNORM_ALPHA * x_ref[...] + mix, g_ref[...], b_ref[...])


def _even_out(outs, lses, gu, x, w_out, conv_w, g, b, seq, tm):
    n = x.shape[0]
    hrows = 16
    nh = n // hrows
    per = tm // hrows
    d4, d16 = DILATED_PATTERNS[1][1], DILATED_PATTERNS[2][1]
    views = [pl.BlockSpec((tm // d, d * A_WIDTH), lambda i: (i, 0)) for d in (1, d4, d16)]
    full = lambda a: pl.BlockSpec(a.shape, lambda i: (0, 0))
    slab = pltpu.VMEM((A_WIDTH // LANES, tm, LANES), F32)
    return pl.pallas_call(
        functools.partial(_even_out_kernel, tm=tm, seq=seq),
        grid=(n // tm,),
        scratch_shapes=[slab, slab, slab, slab],
        in_specs=views + views + [
            pl.BlockSpec((tm, 2 * B_WIDTH), lambda i: (i, 0)),
            pl.BlockSpec((hrows, B_WIDTH), lambda i: (jnp.maximum(i * per - 1, 0), 1)),
            pl.BlockSpec((hrows, B_WIDTH), lambda i: (jnp.minimum((i + 1) * per, nh - 1), 1)),
            pl.BlockSpec((tm, D_MODEL), lambda i: (i, 0)),
            full(w_out), full(conv_w), full(g), full(b)],
        out_specs=pl.BlockSpec((tm, D_MODEL), lambda i: (i, 0)),
        out_shape=jax.ShapeDtypeStruct((n, D_MODEL), F32),
        compiler_params=_cparams("parallel"),
        name="even_out",
    )(*outs, *lses, gu, gu, gu, x, w_out, conv_w, g, b)


def _gla_gates(z, lb, tri2):
    f = lb + (1.0 - lb) * _sigmoid(z)
    lf2 = jnp.log(f) * LOG2_E
    hi = lf2.astype(BF16)
    lo = (lf2 - hi.astype(F32)).astype(BF16)
    cum = jnp.dot(tri2, jnp.concatenate([hi, lo], axis=0), preferred_element_type=F32)
    return 1.0 - f, cum


def _gla_scores(q, kk, cum, *, reverse):
    c = GLA_CHUNK
    sb = GLA_SUB
    dk = q.shape[1]
    nt = (((1,), (1,)), ((), ()))
    ti = lax.broadcasted_iota(jnp.int32, (c, c), 0)
    si = lax.broadcasted_iota(jnp.int32, (c, c), 1)
    causal = (si >= ti) if reverse else (ti >= si)

    parts = []
    for blk in range(c // sb):
        rows = slice(blk * sb, (blk + 1) * sb)
        if reverse:
            edge = (blk + 1) * sb
            ref = cum[edge:edge + 1, :] if edge < c else jnp.zeros((1, dk), F32)
            other = slice((blk + 1) * sb, c)
        else:
            edge = blk * sb - 1
            ref = cum[edge:edge + 1, :] if edge >= 0 else jnp.zeros((1, dk), F32)
            other = slice(0, blk * sb)
        qs = q[rows] * jnp.exp2(cum[rows] - ref)
        k_own = (kk[rows] * jnp.exp2(jnp.minimum(ref - cum[rows], GLA_EXP2_CLAMP))).astype(BF16)
        pieces = [k_own]
        n_other = other.stop - other.start
        if n_other:
            k_other = (kk[other] * jnp.exp2(ref - cum[other])).astype(BF16)
            pieces = [k_own, k_other] if reverse else [k_other, k_own]
        if n_other + sb < c:
            pad = jnp.zeros((c - n_other - sb, dk), BF16)
            pieces = [pad] + pieces if reverse else pieces + [pad]
        ks = jnp.concatenate(pieces, axis=0) if len(pieces) > 1 else pieces[0]
        parts.append(lax.dot_general(qs.astype(BF16), ks, nt, preferred_element_type=F32))
    return jnp.where(causal, jnp.concatenate(parts, axis=0), 0.0).astype(BF16)


def _gla_state_terms(q, kk, cum, *, reverse):
    c = GLA_CHUNK
    total = cum[0:1, :] if reverse else cum[c - 1:c, :]
    qe = (q * jnp.exp2(cum)).astype(BF16)
    kd = (kk * jnp.exp2(total - cum)).astype(BF16)
    return qe, kd, jnp.exp2(total)


def _gla_kernel(q_ref, zf_ref, zb_ref, v_ref, g_ref, lbl_ref, ng_ref, o_ref,
                acc_f, acc_b, qe_f, qe_b, kd_f, kd_b, et_f, et_b, att_f, att_b, ring_a, ring_b, *, seq):
    c = GLA_CHUNK
    nc = seq // c
    dk = HG_DIM
    nt = (((1,), (1,)), ((), ()))
    tn = (((0,), (0,)), ((), ()))
    l0 = lbl_ref[0:1, :]
    l1 = lbl_ref[1:2, :]
    mx = jnp.maximum(l0, l1)
    e0, e1 = jnp.exp(l0 - mx), jnp.exp(l1 - mx)
    lb = e0 / (e0 + e1)

    ti = lax.broadcasted_iota(jnp.int32, (c, 2 * c), 0)
    si = lax.broadcasted_iota(jnp.int32, (c, 2 * c), 1) & (c - 1)
    tri_f = jnp.where(ti >= si, 1.0, 0.0).astype(BF16)
    tri_b = jnp.where(si >= ti, 1.0, 0.0).astype(BF16)

    dirs = ((zf_ref, tri_f, att_f, qe_f, kd_f, et_f, False),
            (zb_ref, tri_b, att_b, qe_b, kd_b, et_b, True))
    grp = GLA_INTRA_UNROLL
    n_groups = nc // grp
    chains = [(u, d) for u in range(grp) for d in range(2)]

    def park_gates(group, ring):
        for k, (u, d) in enumerate(chains):
            rows = pl.ds(pl.multiple_of((group * grp + u) * c, c), c)
            kk, cum = _gla_gates(dirs[d][0][0, rows, :].astype(F32), lb, dirs[d][1])
            ring[0, k * c:(k + 1) * c, :] = kk
            ring[1, k * c:(k + 1) * c, :] = cum

    def scores_from_ring(group, ring):
        for k, (u, d) in enumerate(chains):
            ci = group * grp + u
            rows = pl.ds(pl.multiple_of(ci * c, c), c)
            q = q_ref[0, rows, :].astype(F32)
            kk = ring[0, k * c:(k + 1) * c, :]
            cum = ring[1, k * c:(k + 1) * c, :]
            dirs[d][2][rows, :] = _gla_scores(q, kk, cum, reverse=dirs[d][6])
            qe, kd, et = _gla_state_terms(q, kk, cum, reverse=dirs[d][6])
            dirs[d][3][rows, :] = qe
            dirs[d][4][rows, :] = kd
            dirs[d][5][pl.ds(ci, 1), :] = et

    park_gates(0, ring_a)
    park_gates(1, ring_b)

    def intra(j, _):
        scores_from_ring(2 * j, ring_a)
        scores_from_ring(2 * j + 1, ring_b)
        park_gates(jnp.minimum(2 * j + 2, n_groups - 1), ring_a)
        park_gates(jnp.minimum(2 * j + 3, n_groups - 1), ring_b)
        return 0

    lax.fori_loop(0, n_groups // 2, intra, 0)

    def scan(j, carry):
        sf, sr = carry
        cfs = [j * GLA_SCAN_UNROLL + u for u in range(GLA_SCAN_UNROLL)]
        crs = [nc - 1 - cf for cf in cfs]
        rfs = [pl.ds(pl.multiple_of(cf * c, c), c) for cf in cfs]
        rrs = [pl.ds(pl.multiple_of(cr * c, c), c) for cr in crs]
        upd_f = [lax.dot_general(v_ref[0, r, :], kd_f[r, :], tn, preferred_element_type=F32) for r in rfs]
        upd_r = [lax.dot_general(v_ref[0, r, :], kd_b[r, :], tn, preferred_element_type=F32) for r in rrs]
        loc_f = [jnp.dot(att_f[r, :], v_ref[0, r, :], preferred_element_type=F32) for r in rfs]
        loc_r = [jnp.dot(att_b[r, :], v_ref[0, r, :], preferred_element_type=F32) for r in rrs]
        sfs, srs = [sf], [sr]
        for u in range(GLA_SCAN_UNROLL):
            sfs.append(sfs[-1] * et_f[pl.ds(cfs[u], 1), :] + upd_f[u])
            srs.append(srs[-1] * et_b[pl.ds(crs[u], 1), :] + upd_r[u])
        for u in range(GLA_SCAN_UNROLL):
            acc_f[rfs[u], :] = loc_f[u] + lax.dot_general(qe_f[rfs[u], :], sfs[u].astype(BF16), nt,
                                                          preferred_element_type=F32)
            acc_b[rrs[u], :] = loc_r[u] + lax.dot_general(qe_b[rrs[u], :], srs[u].astype(BF16), nt,
                                                          preferred_element_type=F32)
        return sfs[-1], srs[-1]

    zero = jnp.zeros((dk, dk), F32)
    lax.fori_loop(0, nc // GLA_SCAN_UNROLL, scan, (zero, zero))

    blk = 512
    ng = ng_ref[...]

    def fin(j, _):
        rows = pl.ds(pl.multiple_of(j * blk, blk), blk)
        o = acc_f[rows, :] + acc_b[rows, :]
        o = o * lax.rsqrt(jnp.mean(o * o, axis=-1, keepdims=True) + RMS_EPS) * ng
        g = g_ref[0, rows, :].astype(F32)
        o_ref[0, rows, :] = (o * (g * _sigmoid(g))).astype(o_ref.dtype)
        return 0

    lax.fori_loop(0, seq // blk, fin, 0)


def _gla(proj, lb_logits, norm_g, batch, seq):
    view = proj.reshape(batch, seq, 5 * D_MODEL)

    def col(seg):
        return pl.BlockSpec((1, seq, HG_DIM), lambda b, h: (b, 0, seg * HG_HEADS + h))

    return pl.pallas_call(
        functools.partial(_gla_kernel, seq=seq),
        grid=(batch, HG_HEADS),
        in_specs=[col(0), col(1), col(2), col(3), col(4),
                  pl.BlockSpec((DEPTH, HG_DIM), lambda b, h: (0, h)),
                  pl.BlockSpec((1, HG_DIM), lambda b, h: (0, h))],
        out_specs=pl.BlockSpec((1, seq, HG_DIM), lambda b, h: (b, 0, h)),
        out_shape=jax.ShapeDtypeStruct((batch, seq, D_MODEL), BF16),
        scratch_shapes=[pltpu.VMEM((seq, HG_DIM), F32), pltpu.VMEM((seq, HG_DIM), F32),
                        pltpu.VMEM((seq, HG_DIM), BF16), pltpu.VMEM((seq, HG_DIM), BF16),
                        pltpu.VMEM((seq, HG_DIM), BF16), pltpu.VMEM((seq, HG_DIM), BF16),
                        pltpu.VMEM((seq // GLA_CHUNK, HG_DIM), F32),
                        pltpu.VMEM((seq // GLA_CHUNK, HG_DIM), F32),
                        pltpu.VMEM((seq, GLA_CHUNK), BF16), pltpu.VMEM((seq, GLA_CHUNK), BF16),
                        pltpu.VMEM((2, 2 * GLA_INTRA_UNROLL * GLA_CHUNK, HG_DIM), F32),
                        pltpu.VMEM((2, 2 * GLA_INTRA_UNROLL * GLA_CHUNK, HG_DIM), F32)],
        compiler_params=_cparams("parallel", "parallel"),
        name="gla",
    )(view, view, view, view, view, lb_logits, norm_g).reshape(batch * seq, D_MODEL)


def _proj_ln_kernel(a_ref, w_ref, x_ref, g_ref, b_ref, y_ref):
    mix = jnp.dot(a_ref[...], w_ref[...], preferred_element_type=F32)
    y_ref[...] = _layer_norm(DEEPNORM_ALPHA * x_ref[...] + mix, g_ref[...], b_ref[...])


def _proj_ln(a, w, x, g, b, tm):
    n = x.shape[0]
    full = lambda t: pl.BlockSpec(t.shape, lambda i: (0, 0))
    row = pl.BlockSpec((tm, D_MODEL), lambda i: (i, 0))
    return pl.pallas_call(
        _proj_ln_kernel,
        grid=(n // tm,),
        in_specs=[row, full(w), row, full(g), full(b)],
        out_specs=row,
        out_shape=jax.ShapeDtypeStruct((n, D_MODEL), F32),
        compiler_params=_cparams("parallel"),
        name="proj_ln",
    )(a, w, x, g, b)


ROUTER_ROWS = 64
ROUTE_OUT_ROWS = 8


def _first_index_of(vals, target, n_rows):
    idx = lax.broadcasted_iota(jnp.int32, vals.shape, 0)
    return jnp.min(jnp.where(vals == target, idx, n_rows), axis=0, keepdims=True)


def _xattn_kernel(x_ref, wq_ref, kv_ref, wo_ref, g_ref, b_ref, wr_ref, br_ref, y_ref, yb_ref, route_ref):
    x = x_ref[...]
    q = jnp.dot(x.astype(BF16), wq_ref[...], preferred_element_type=F32) * (XA_HEAD_DIM ** -0.5)
    qb = q.astype(BF16)
    nt = (((1,), (1,)), ((), ()))
    heads = []
    for h in range(XA_HEADS):
        cols = slice(h * XA_HEAD_DIM, (h + 1) * XA_HEAD_DIM)
        k = kv_ref[0, :, cols]
        v = kv_ref[0, :, D_MODEL + h * XA_HEAD_DIM:D_MODEL + (h + 1) * XA_HEAD_DIM]
        sc = lax.dot_general(qb[:, cols], k, nt, preferred_element_type=F32)
        m = jnp.max(sc, axis=-1, keepdims=True)
        pe = jnp.exp(sc - m)
        p = pe / jnp.sum(pe, axis=-1, keepdims=True)
        heads.append(jnp.dot(p.astype(BF16), v, preferred_element_type=F32).astype(BF16))
    o = jnp.concatenate(heads, axis=1)
    xa = jnp.dot(o, wo_ref[...], preferred_element_type=F32)
    y = _layer_norm(DEEPNORM_ALPHA * x + xa, g_ref[...], b_ref[...])
    y_ref[...] = y
    yb_ref[...] = _pack_bf16_pair(y)

    lg = lax.dot_general(wr_ref[...], y.astype(BF16), nt, preferred_element_type=F32) + br_ref[:, 0:1]
    gl = lg[0:N_GROUPS, :]
    gmax = jnp.max(gl, axis=0, keepdims=True)
    g_w = 1.0 / jnp.sum(jnp.exp(gl - gmax), axis=0, keepdims=True)
    g_sel = _first_index_of(gl, gmax, N_GROUPS)
    el = jnp.zeros((EXPERTS_PER_GROUP, gl.shape[1]), F32)
    for grp in range(N_GROUPS):
        rows = slice(8 + grp * EXPERTS_PER_GROUP, 8 + (grp + 1) * EXPERTS_PER_GROUP)
        el = el + jnp.where(g_sel == grp, lg[rows, :], 0.0)
    m1 = jnp.max(el, axis=0, keepdims=True)
    i1 = _first_index_of(el, m1, EXPERTS_PER_GROUP)
    eidx = lax.broadcasted_iota(jnp.int32, el.shape, 0)
    el2 = jnp.where(eidx == i1, -jnp.inf, el)
    m2 = jnp.max(el2, axis=0, keepdims=True)
    i2 = _first_index_of(el2, m2, EXPERTS_PER_GROUP)
    e2 = jnp.exp(m2 - m1)
    den = 1.0 + e2
    w1 = g_w / den
    w2 = g_w * e2 / den
    base = g_sel * EXPERTS_PER_GROUP
    zero = jnp.zeros_like(w1)
    route_ref[...] = jnp.concatenate(
        [(base + i1).astype(F32), (base + i2).astype(F32), w1, w2, zero, zero, zero, zero], axis=0)


def _xattn(x, kv, wq, wo, g, b, wr, br, seq, tm):
    n = x.shape[0]
    spt = seq // tm
    full = lambda t: pl.BlockSpec(t.shape, lambda i: (0, 0))
    row = pl.BlockSpec((tm, D_MODEL), lambda i: (i, 0))
    return pl.pallas_call(
        _xattn_kernel,
        grid=(n // tm,),
        in_specs=[row, full(wq),
                  pl.BlockSpec((1,) + kv.shape[1:], lambda i: (i // spt, 0, 0)),
                  full(wo), full(g), full(b), full(wr), full(br)],
        out_specs=[row, pl.BlockSpec((tm, D_MODEL // 2), lambda i: (i, 0)),
                   pl.BlockSpec((ROUTE_OUT_ROWS, tm), lambda i: (0, i))],
        out_shape=[jax.ShapeDtypeStruct((n, D_MODEL), F32),
                   jax.ShapeDtypeStruct((n, D_MODEL // 2), jnp.int32),
                   jax.ShapeDtypeStruct((ROUTE_OUT_ROWS, n), F32)],
        compiler_params=_cparams("parallel"),
        name="xattn_router",
    )(x, wq, kv, wo, g, b, wr, br)


def _expert_kernel(te_ref, nu_ref, xs_ref, w1_ref, w3_ref, w2_ref, ys_ref):
    j = pl.program_id(0)

    @pl.when(j < nu_ref[0])
    def _():
        half = D_MODEL // 2
        x_hi, x_lo = _unpack_bf16_pair(xs_ref[...])
        x_hi, x_lo = x_hi.astype(BF16), x_lo.astype(BF16)

        def up(w_ref):
            return (jnp.dot(x_hi, w_ref[0, 0:half, :].astype(BF16), preferred_element_type=F32)
                    + jnp.dot(x_lo, w_ref[0, half:D_MODEL, :].astype(BF16), preferred_element_type=F32))

        h1 = up(w1_ref)
        hid = h1 * _sigmoid(h1) * up(w3_ref)
        ys_ref[...] = _pack_bf16_pair(
            jnp.dot(hid.astype(BF16), w2_ref[0].astype(BF16), preferred_element_type=F32))

    @pl.when(j >= nu_ref[0])
    def _():
        ys_ref[...] = jnp.zeros_like(ys_ref)


def _experts(xs, tile_expert, n_used, w1, w3, w2, tm):
    mp = xs.shape[0]
    w_in = pl.BlockSpec((1, D_MODEL, EXPERT_FF), lambda j, te, nu: (te[j], 0, 0))
    rows = pl.BlockSpec((tm, D_MODEL // 2), lambda j, te, nu: (j, 0))
    grid_spec = pltpu.PrefetchScalarGridSpec(
        num_scalar_prefetch=2,
        grid=(mp // tm,),
        in_specs=[rows, w_in, w_in,
                  pl.BlockSpec((1, EXPERT_FF, D_MODEL), lambda j, te, nu: (te[j], 0, 0))],
        out_specs=rows,
    )
    return pl.pallas_call(
        _expert_kernel,
        grid_spec=grid_spec,
        out_shape=jax.ShapeDtypeStruct((mp, D_MODEL // 2), jnp.int32),
        compiler_params=_cparams("arbitrary"),
        name="experts",
    )(tile_expert, n_used, xs, w1, w3, w2)


def _moe_ln_kernel(x_ref, y0_ref, y1_ref, gate_ref, g_ref, b_ref, o_ref):
    g0, g1 = gate_ref[:, 0:1], gate_ref[:, 1:2]
    hi0, lo0 = _unpack_bf16_pair(y0_ref[...])
    hi1, lo1 = _unpack_bf16_pair(y1_ref[...])
    ff = jnp.concatenate([g0 * hi0 + g1 * hi1, g0 * lo0 + g1 * lo1], axis=1)
    o_ref[...] = _layer_norm(DEEPNORM_ALPHA * x_ref[...] + ff, g_ref[...], b_ref[...])


def _moe_ln(x, y01, gates, g, b, tm):
    n = x.shape[0]
    full = lambda t: pl.BlockSpec(t.shape, lambda i: (0, 0))
    row = pl.BlockSpec((tm, D_MODEL), lambda i: (i, 0))
    second = n // tm
    return pl.pallas_call(
        _moe_ln_kernel,
        grid=(n // tm,),
        in_specs=[row,
                  pl.BlockSpec((tm, D_MODEL // 2), lambda i: (i, 0)),
                  pl.BlockSpec((tm, D_MODEL // 2), lambda i: (i + second, 0)),
                  pl.BlockSpec((tm, 2), lambda i: (i, 0)), full(g), full(b)],
        out_specs=row,
        out_shape=jax.ShapeDtypeStruct((n, D_MODEL), F32),
        compiler_params=_cparams("parallel"),
        name="moe_ln",
    )(x, y01, y01, gates, g, b)


SC_CORES = 2
SC_SUBCORES = 16
SC_GATHER_ROWS = 32


def _sc_gather_rows(table, idx):
    n_out = idx.shape[0]
    width = table.shape[1]
    workers = SC_CORES * SC_SUBCORES
    ch = SC_GATHER_ROWS
    per_w = n_out // workers
    steps = per_w // ch
    assert per_w * workers == n_out and steps * ch == per_w and steps % 2 == 0
    mesh = plsc.VectorSubcoreMesh(core_axis_name="c", subcore_axis_name="s")

    @functools.partial(
        pl.kernel, mesh=mesh,
        out_type=jax.ShapeDtypeStruct((n_out, width), table.dtype),
        scratch_types=[pltpu.VMEM((per_w,), jnp.int32),
                       pltpu.VMEM((ch, width), table.dtype), pltpu.VMEM((ch, width), table.dtype),
                       pltpu.SemaphoreType.DMA, pltpu.SemaphoreType.DMA,
                       pltpu.SemaphoreType.DMA, pltpu.SemaphoreType.DMA],
    )
    def gather_kernel(table_hbm, idx_hbm, out_hbm, idx_v, rows0, rows1, g0, g1, w0, w1):
        wid = lax.axis_index("s") * SC_CORES + lax.axis_index("c")
        base = wid * per_w
        pltpu.sync_copy(idx_hbm.at[pl.ds(base, per_w)], idx_v)
        bufs = ((rows0, g0, w0), (rows1, g1, w1))

        def gather(i, b):
            return pltpu.make_async_copy(table_hbm.at[idx_v.at[pl.ds(i * ch, ch)]], bufs[b][0], bufs[b][1])

        def write(i, b):
            return pltpu.make_async_copy(bufs[b][0], out_hbm.at[pl.ds(base + i * ch, ch)], bufs[b][2])

        gather(0, 0).start()

        @pl.loop(0, steps, step=2)
        def _(i):
            for b in range(2):
                ii = i + b
                gather(ii, b).wait()

                @pl.when(ii >= 1)
                def _():
                    write(ii - 1, 1 - b).wait()

                @pl.when(ii + 1 < steps)
                def _():
                    gather(ii + 1, 1 - b).start()

                write(ii, b).start()

        write(steps - 1, 1).wait()

    return gather_kernel(table, idx)


def _dispatch_plan(ids, n, tm_e):
    n_asg = 2 * n
    mp = n_asg + N_EXPERTS * tm_e
    order = jnp.argsort(ids, stable=True).astype(jnp.int32)
    pos = jnp.argsort(order).astype(jnp.int32)
    onehot = (ids[:, None] == jnp.arange(N_EXPERTS, dtype=jnp.int32)[None, :]).astype(jnp.int32)
    counts = jnp.sum(onehot, axis=0)
    dense_start = jnp.cumsum(counts) - counts
    padded = ((counts + tm_e - 1) // tm_e) * tm_e
    row_end = jnp.cumsum(padded)
    row_start = row_end - padded
    row_of_asg = pos + jnp.sum(onehot * (row_start - dense_start)[None, :], axis=1)
    tile_start = jnp.arange(mp // tm_e, dtype=jnp.int32) * tm_e
    tile_expert = jnp.minimum(jnp.sum((tile_start[:, None] >= row_end[None, :]).astype(jnp.int32), axis=1),
                              N_EXPERTS - 1)
    shift = (dense_start - row_start)[tile_expert]
    src = (tile_start + shift)[:, None] + jnp.arange(tm_e, dtype=jnp.int32)[None, :]
    asg_of_row = order[jnp.clip(src.reshape(mp), 0, n_asg - 1)]
    tok_of_row = jnp.where(asg_of_row >= n, asg_of_row - n, asg_of_row)
    n_used = (row_end[-1] // tm_e).astype(jnp.int32).reshape(1)
    return tok_of_row, row_of_asg, tile_expert.astype(jnp.int32), n_used


def _moe_steps(x, xp, route, w1, w3, w2, g, b, tm_e, tm):
    n = x.shape[0]
    ids = route[0:2].astype(jnp.int32).reshape(2 * n)
    tok_of_row, row_of_asg, tile_expert, n_used = _dispatch_plan(ids, n, tm_e)
    xs = _sc_gather_rows(xp, tok_of_row)
    yield
    ys = _experts(xs, tile_expert, n_used, w1, w3, w2, tm_e)
    y01 = _sc_gather_rows(ys, row_of_asg)
    yield
    return _moe_ln(x, y01, route[2:4].T, g, b, tm)


def _router_weights(w_group, b_group, w_expert, b_expert):
    wr = jnp.zeros((ROUTER_ROWS, D_MODEL), F32)
    wr = wr.at[0:N_GROUPS].set(w_group.T).at[8:8 + N_EXPERTS].set(w_expert.T)
    br = jnp.zeros((ROUTER_ROWS, LANES), F32)
    br = br.at[0:N_GROUPS, :].set(b_group[:, None]).at[8:8 + N_EXPERTS, :].set(b_expert[:, None])
    return wr.astype(BF16), br


def _trunk_steps(x3, mem3, p):
    batch, seq, _ = x3.shape
    n = batch * seq
    x = x3.reshape(n, D_MODEL)
    mem = mem3.reshape(batch * mem3.shape[1], D_MODEL)
    tm = 512
    tables = _rope_tables(seq)
    for layer in range(DEPTH):
        j = layer // 2
        row = lambda a: a.reshape(1, D_MODEL)
        if layer % 2 == 0:
            *views, gu = _even_proj(x, p["ev_w_in"][j], tables, seq, tm)
            res = [_band_attention(v, batch, seq, dil) for v, (_, dil) in zip(views, DILATED_PATTERNS)]
            x = _even_out([o for o, _ in res], [l for _, l in res], gu, x, p["ev_w_out"][j],
                          p["ev_conv_w"][j], row(p["ln_g"][layer, 0]), row(p["ln_b"][layer, 0]), seq, tm)
        else:
            proj = _matmul(x, p["od_w_in"][j], BF16, tm, D_MODEL)
            o = _gla(proj, p["lb_logits"], p["od_norm_g"][j].reshape(1, D_MODEL), batch, seq)
            x = _proj_ln(o, p["od_w_out"][j], x, row(p["ln_g"][layer, 0]), row(p["ln_b"][layer, 0]), tm)
        kv = _matmul(mem, p["xa_w_kv"][layer], BF16, 256, D_MODEL).reshape(batch, mem3.shape[1], 2 * D_MODEL)
        wr, br = _router_weights(p["moe_w_group"][layer], p["moe_b_group"][layer],
                                 p["moe_w_expert"][layer], p["moe_b_expert"][layer])
        x, xp, route = _xattn(x, kv, p["xa_w_q"][layer], p["xa_w_out"][layer],
                              row(p["ln_g"][layer, 1]), row(p["ln_b"][layer, 1]), wr, br, seq, tm)
        x = yield from _moe_steps(x, xp, route, p["moe_w1"][layer], p["moe_w3"][layer], p["moe_w2"][layer],
                                  row(p["ln_g"][layer, 2]), row(p["ln_b"][layer, 2]), 512, tm)
    return x.reshape(batch, seq, D_MODEL)


def _run_interleaved(generators):
    results = [None] * len(generators)
    live = list(range(len(generators)))
    while live:
        for k in list(live):
            try:
                next(generators[k])
            except StopIteration as stop:
                results[k] = stop.value
                live.remove(k)
    return results


def _trunk(x3, mem3, p):
    return _run_interleaved([_trunk_steps(x3, mem3, p)])[0]


def kernel(x_prompt, x_sample, mem_prompt, mem_sample, ev_w_in, ev_conv_w, ev_w_out, od_w_in, lb_logits,
           od_norm_g, od_w_out, xa_w_q, xa_w_kv, xa_w_out, moe_w_group, moe_b_group, moe_w_expert,
           moe_b_expert, moe_w1, moe_w3, moe_w2, ln_g, ln_b):
    ff = moe_w1.shape[-1]
    p = dict(
        ev_w_in=ev_w_in.astype(BF16), ev_conv_w=ev_conv_w, ev_w_out=ev_w_out.astype(BF16),
        od_w_in=od_w_in.astype(BF16), lb_logits=lb_logits, od_norm_g=od_norm_g,
        od_w_out=od_w_out.astype(BF16), xa_w_q=xa_w_q.astype(BF16), xa_w_kv=xa_w_kv.astype(BF16),
        xa_w_out=xa_w_out.astype(BF16), moe_w_group=moe_w_group, moe_b_group=moe_b_group,
        moe_w_expert=moe_w_expert, moe_b_expert=moe_b_expert,
        moe_w1=moe_w1.reshape(DEPTH, N_EXPERTS, D_MODEL, ff),
        moe_w3=moe_w3.reshape(DEPTH, N_EXPERTS, D_MODEL, ff),
        moe_w2=moe_w2.reshape(DEPTH, N_EXPERTS, ff, D_MODEL),
        ln_g=ln_g, ln_b=ln_b)
    y_prompt, y_sample = _run_interleaved([_trunk_steps(x_prompt, mem_prompt, p),
                                           _trunk_steps(x_sample, mem_sample, p)])
    return y_prompt, y_sample
```

```python
import functools
import math

import jax
import jax.numpy as jnp
from jax import lax
from jax.experimental import pallas as pl
from jax.experimental.pallas import tpu as pltpu
from jax.experimental.pallas import tpu_sc as plsc

F32 = jnp.float32
BF16 = jnp.bfloat16

D_MODEL = 1024
DEPTH = 2
A_HEADS = 8
A_HEAD_DIM = 64
A_WIDTH = A_HEADS * A_HEAD_DIM
DILATED_PATTERNS = ((128, 1), (512, 4), (2048, 16))
ROPE_THETA = 500000.0
ROPE_DIM = A_HEAD_DIM // 4
B_WIDTH = D_MODEL // 2
CONV_WIDTH = 3
HG_HEADS = 8
HG_DIM = D_MODEL // HG_HEADS
XA_HEADS = 4
XA_HEAD_DIM = D_MODEL // XA_HEADS
N_GROUPS = 4
EXPERTS_PER_GROUP = 8
N_EXPERTS = N_GROUPS * EXPERTS_PER_GROUP
EXPERT_FF = D_MODEL // 4
LN_EPS = 1e-5
RMS_EPS = 1e-6
DEEPNORM_ALPHA = (2 * DEPTH) ** 0.25

LANES = 128
BAND_RADIUS = 64
ATTN_QBLOCK = 128
NEG_BIG = -1e30
GLA_CHUNK = 64
GLA_SUB = 16
GLA_EXP2_CLAMP = 100.0
GLA_INTRA_UNROLL = 2
LOG2_E = 1.4426950408889634
VMEM_LIMIT = 56 * 1024 * 1024


def _cparams(*sem):
    return pltpu.CompilerParams(dimension_semantics=sem, vmem_limit_bytes=VMEM_LIMIT)


def _layer_norm(y, g, b):
    mu = jnp.mean(y, axis=-1, keepdims=True)
    d = y - mu
    var = jnp.mean(d * d, axis=-1, keepdims=True)
    return d * lax.rsqrt(var + LN_EPS) * g + b


def _sigmoid(z):
    return 1.0 / (1.0 + jnp.exp(-z))


def _pack_bf16_pair(y):
    w = y.shape[1] // 2
    hi = lax.bitcast_convert_type(y[:, :w].astype(BF16).astype(F32), jnp.int32)
    lo = lax.bitcast_convert_type(y[:, w:].astype(BF16).astype(F32), jnp.int32)
    return hi | lax.shift_right_logical(lo, 16)


def _unpack_bf16_pair(p):
    hi = lax.bitcast_convert_type(p & jnp.int32(-65536), F32)
    lo = lax.bitcast_convert_type(lax.shift_left(p, 16), F32)
    return hi, lo


def _mm_kernel(x_ref, w_ref, o_ref, *, chunk):
    xb = x_ref[...].astype(BF16)
    for c in range(w_ref.shape[1] // chunk):
        cols = slice(c * chunk, (c + 1) * chunk)
        o_ref[:, cols] = jnp.dot(xb, w_ref[:, cols], preferred_element_type=F32).astype(o_ref.dtype)


def _matmul(x, w, out_dtype, tm, chunk):
    n, k = x.shape
    m = w.shape[1]
    return pl.pallas_call(
        functools.partial(_mm_kernel, chunk=chunk),
        grid=(n // tm,),
        in_specs=[pl.BlockSpec((tm, k), lambda i: (i, 0)), pl.BlockSpec((k, m), lambda i: (0, 0))],
        out_specs=pl.BlockSpec((tm, m), lambda i: (i, 0)),
        out_shape=jax.ShapeDtypeStruct((n, m), out_dtype),
        compiler_params=_cparams("parallel"),
        name="matmul",
    )(x, w)


def _rope_tables(seq):
    half = ROPE_DIM // 2
    inv_freq = jnp.exp(-math.log(ROPE_THETA) * jnp.arange(half, dtype=F32) * (2.0 / ROPE_DIM))
    ang = jnp.arange(seq, dtype=F32)[:, None] * inv_freq[None, :]
    cos, sin = jnp.cos(ang), jnp.sin(ang)
    ones = jnp.ones((seq, A_HEAD_DIM - ROPE_DIM), F32)
    zeros = jnp.zeros((seq, A_HEAD_DIM - ROPE_DIM), F32)
    zh = jnp.zeros((seq, half), F32)
    c = jnp.concatenate([cos, cos, ones], -1)
    s_up = jnp.concatenate([-sin, zh, zeros], -1)
    s_dn = jnp.concatenate([zh, sin, zeros], -1)
    rep = LANES // A_HEAD_DIM
    return tuple(jnp.tile(t, (1, rep)) for t in (c, s_up, s_dn))


def _even_proj_kernel(x_ref, w_ref, c_ref, su_ref, sd_ref, qkv_ref, qkv4_ref, qkv16_ref, gu_ref, slab_ref):
    tm = x_ref.shape[0]
    xb = x_ref[...].astype(BF16)
    rep = A_WIDTH // LANES
    half = ROPE_DIM // 2
    c = jnp.tile(c_ref[...], (1, rep))
    su = jnp.tile(su_ref[...], (1, rep))
    sd = jnp.tile(sd_ref[...], (1, rep))

    def proj(j):
        return jnp.dot(xb, w_ref[:, j * A_WIDTH:(j + 1) * A_WIDTH], preferred_element_type=F32)

    def rope(t):
        up = pltpu.roll(t, A_WIDTH - half, axis=1)
        dn = pltpu.roll(t, half, axis=1)
        return t * c + up * su + dn * sd

    qkv = (rope(proj(0)) * (A_HEAD_DIM ** -0.5), rope(proj(1)), proj(2))
    per = A_WIDTH // LANES
    for j, part in enumerate(qkv):
        qkv_ref[:, j * A_WIDTH:(j + 1) * A_WIDTH] = part.astype(BF16)
        for s in range(per):
            slab_ref[j * per + s] = part[:, s * LANES:(s + 1) * LANES]
    for dil, out_ref in ((DILATED_PATTERNS[1][1], qkv4_ref), (DILATED_PATTERNS[2][1], qkv16_ref)):
        for r in range(dil):
            for s in range(3 * per):
                val = slab_ref[s, pl.ds(r, tm // dil, stride=dil), :]
                col = r * 3 * A_WIDTH + s * LANES
                out_ref[:, col:col + LANES] = val.astype(BF16)
    gu_ref[:, 0:B_WIDTH] = proj(3).astype(BF16)
    gu_ref[:, B_WIDTH:2 * B_WIDTH] = (proj(4) * proj(5)).astype(BF16)


def _even_proj(x, w_in, tables, seq, tm):
    n = x.shape[0]
    spt = seq // tm
    tab_spec = pl.BlockSpec((tm, LANES), lambda i: (i % spt, 0))
    d4, d16 = DILATED_PATTERNS[1][1], DILATED_PATTERNS[2][1]
    width = 3 * A_WIDTH
    return pl.pallas_call(
        _even_proj_kernel,
        grid=(n // tm,),
        in_specs=[pl.BlockSpec((tm, D_MODEL), lambda i: (i, 0)),
                  pl.BlockSpec(w_in.shape, lambda i: (0, 0)),
                  tab_spec, tab_spec, tab_spec],
        out_specs=[pl.BlockSpec((tm, width), lambda i: (i, 0)),
                   pl.BlockSpec((tm // d4, d4 * width), lambda i: (i, 0)),
                   pl.BlockSpec((tm // d16, d16 * width), lambda i: (i, 0)),
                   pl.BlockSpec((tm, 2 * B_WIDTH), lambda i: (i, 0))],
        out_shape=[jax.ShapeDtypeStruct((n, width), BF16),
                   jax.ShapeDtypeStruct((n // d4, d4 * width), BF16),
                   jax.ShapeDtypeStruct((n // d16, d16 * width), BF16),
                   jax.ShapeDtypeStruct((n, 2 * B_WIDTH), BF16)],
        scratch_shapes=[pltpu.VMEM((width // LANES, tm, LANES), F32)],
        compiler_params=_cparams("parallel"),
        name="even_proj",
    )(x, w_in, *tables)


def _band_attn_kernel(q_ref, kp_ref, km_ref, kn_ref, vp_ref, vm_ref, vn_ref, o_ref, lse_ref,
                      kbuf, vbuf, *, tq, length):
    i = pl.program_id(2)
    r = BAND_RADIUS
    kbuf[0:r] = kp_ref[0]
    kbuf[r:r + tq] = km_ref[0]
    kbuf[r + tq:r + tq + r] = kn_ref[0]
    vbuf[0:r] = vp_ref[0]
    vbuf[r:r + tq] = vm_ref[0]
    vbuf[r + tq:r + tq + r] = vn_ref[0]

    qb = ATTN_QBLOCK
    kw = qb + 2 * r
    qi = lax.broadcasted_iota(jnp.int32, (qb, kw), 0)
    kj = lax.broadcasted_iota(jnp.int32, (qb, kw), 1)
    rel = kj - qi
    band = (rel >= 0) & (rel <= 2 * r)
    lane = lax.broadcasted_iota(jnp.int32, (qb, LANES), 1)
    low = lane < A_HEAD_DIM
    nt = (((1,), (1,)), ((), ()))

    for s in range(tq // qb):
        kpos = i * tq + (s * qb - r) + kj
        valid = band & (kpos >= 0) & (kpos < length)
        bias = jnp.where(valid, 0.0, NEG_BIG)
        rows = slice(s * qb, (s + 1) * qb)
        wrows = slice(s * qb, s * qb + kw)
        for p in range(A_WIDTH // LANES):
            cols = slice(p * LANES, (p + 1) * LANES)
            qp = q_ref[0, rows, cols]
            kwin = kbuf[wrows, cols]
            vwin = vbuf[wrows, cols]
            outs, lses = [], []
            for sel in (low, jnp.logical_not(low)):
                qm = jnp.where(sel, qp, jnp.zeros_like(qp))
                sc = lax.dot_general(qm, kwin, nt, preferred_element_type=F32) + bias
                m = jnp.max(sc, axis=-1, keepdims=True)
                pe = jnp.exp(sc - m)
                l = jnp.sum(pe, axis=-1, keepdims=True)
                pv = jnp.dot(pe.astype(BF16), vwin, preferred_element_type=F32)
                outs.append(pv / l)
                lses.append(jnp.broadcast_to(m + jnp.log(l), (qb, LANES)))
            o_ref[0, rows, cols] = jnp.where(low, outs[0], outs[1]).astype(o_ref.dtype)
            lse_ref[0, rows, cols] = jnp.where(low, lses[0], lses[1])


def _band_attention(qkv_view, batch, seq, dil):
    length = seq // dil
    tq = min(512, length)
    r = BAND_RADIUS
    view = qkv_view.reshape(batch, length, dil * 3 * A_WIDTH)
    nblk_h = length // r
    per = tq // r

    def main(j):
        return pl.BlockSpec((1, tq, A_WIDTH), lambda b, rr, i: (b, i, rr * 3 + j))

    def prev(j):
        return pl.BlockSpec((1, r, A_WIDTH), lambda b, rr, i: (b, jnp.maximum(i * per - 1, 0), rr * 3 + j))

    def nxt(j):
        return pl.BlockSpec((1, r, A_WIDTH),
                            lambda b, rr, i: (b, jnp.minimum((i + 1) * per, nblk_h - 1), rr * 3 + j))

    out_spec = pl.BlockSpec((1, tq, A_WIDTH), lambda b, rr, i: (b, i, rr))
    o, lse = pl.pallas_call(
        functools.partial(_band_attn_kernel, tq=tq, length=length),
        grid=(batch, dil, length // tq),
        in_specs=[main(0), prev(1), main(1), nxt(1), prev(2), main(2), nxt(2)],
        out_specs=[out_spec, out_spec],
        out_shape=[jax.ShapeDtypeStruct((batch, length, dil * A_WIDTH), BF16),
                   jax.ShapeDtypeStruct((batch, length, dil * A_WIDTH), F32)],
        scratch_shapes=[pltpu.VMEM((tq + 2 * r, A_WIDTH), BF16), pltpu.VMEM((tq + 2 * r, A_WIDTH), BF16)],
        compiler_params=_cparams("parallel", "parallel", "parallel"),
        name=f"band_attn_d{dil}",
    )(view, view, view, view, view, view, view)
    return o.reshape(batch * length, dil * A_WIDTH), lse.reshape(batch * length, dil * A_WIDTH)


def _even_out_kernel(o1, o4, o16, l1, l4, l16, gu_ref, up_ref, un_ref, x_ref, w_ref, cw_ref, g_ref, b_ref,
                     y_ref, so4, sl4, so16, sl16, *, tm, seq):
    i = pl.program_id(0)
    pos = (i * tm) % seq
    per = A_WIDTH // LANES
    for dil, o_ref, l_ref, so, sl in ((DILATED_PATTERNS[1][1], o4, l4, so4, sl4),
                                      (DILATED_PATTERNS[2][1], o16, l16, so16, sl16)):
        for r in range(dil):
            for s in range(per):
                cols = slice(r * A_WIDTH + s * LANES, r * A_WIDTH + (s + 1) * LANES)
                so[s, pl.ds(r, tm // dil, stride=dil), :] = o_ref[:, cols].astype(F32)
                sl[s, pl.ds(r, tm // dil, stride=dil), :] = l_ref[:, cols]
    slabs = []
    for s in range(per):
        cols = slice(s * LANES, (s + 1) * LANES)
        la, lb, lc = l1[:, cols], sl4[s], sl16[s]
        mx = jnp.maximum(jnp.maximum(la, lb), lc)
        ea, eb, ec = jnp.exp(la - mx), jnp.exp(lb - mx), jnp.exp(lc - mx)
        num = ea * o1[:, cols].astype(F32) + eb * so4[s] + ec * so16[s]
        slabs.append(num / (ea + eb + ec))
    a_out = jnp.concatenate(slabs, axis=1)

    gate_b = gu_ref[:, 0:B_WIDTH].astype(F32)
    u = gu_ref[:, B_WIDTH:2 * B_WIDTH].astype(F32)
    hrows = up_ref.shape[0]
    u_before = jnp.where(pos > 0, up_ref[hrows - 1:hrows, :].astype(F32), 0.0)
    u_after = jnp.where(pos + tm < seq, un_ref[0:1, :].astype(F32), 0.0)
    row = lax.broadcasted_iota(jnp.int32, (tm, B_WIDTH), 0)
    u_prev = jnp.where(row == 0, u_before, pltpu.roll(u, 1, axis=0))
    u_next = jnp.where(row == tm - 1, u_after, pltpu.roll(u, tm - 1, axis=0))
    conv = u_prev * cw_ref[0:1, :] + u * cw_ref[1:2, :] + u_next * cw_ref[2:3, :]
    b_out = gate_b * conv

    mix = jnp.dot(a_out.astype(BF16), w_ref[0:A_WIDTH, :], preferred_element_type=F32)
    mix = mix + jnp.dot(b_out.astype(BF16), w_ref[A_WIDTH:A_WIDTH + B_WIDTH, :], preferred_element_type=F32)
    y_ref[...] = _layer_norm(DEEPNORM_ALPHA * x_ref[...] + mix, g_ref[...], b_ref[...])


def _even_out(outs, lses, gu, x, w_out, conv_w, g, b, seq, tm):
    n = x.shape[0]
    hrows = 16
    nh = n // hrows
    per = tm // hrows
    d4, d16 = DILATED_PATTERNS[1][1], DILATED_PATTERNS[2][1]
    views = [pl.BlockSpec((tm // d, d * A_WIDTH), lambda i: (i, 0)) for d in (1, d4, d16)]
    full = lambda a: pl.BlockSpec(a.shape, lambda i: (0, 0))
    slab = pltpu.VMEM((A_WIDTH // LANES, tm, LANES), F32)
    return pl.pallas_call(
        functools.partial(_even_out_kernel, tm=tm, seq=seq),
        grid=(n // tm,),
        scratch_shapes=[slab, slab, slab, slab],
        in_specs=views + views + [
            pl.BlockSpec((tm, 2 * B_WIDTH), lambda i: (i, 0)),
            pl.BlockSpec((hrows, B_WIDTH), lambda i: (jnp.maximum(i * per - 1, 0), 1)),
            pl.BlockSpec((hrows, B_WIDTH), lambda i: (jnp.minimum((i + 1) * per, nh - 1), 1)),
            pl.BlockSpec((tm, D_MODEL), lambda i: (i, 0)),
            full(w_out), full(conv_w), full(g), full(b)],
        out_specs=pl.BlockSpec((tm, D_MODEL), lambda i: (i, 0)),
        out_shape=jax.ShapeDtypeStruct((n, D_MODEL), F32),
        compiler_params=_cparams("parallel"),
        name="even_out",
    )(*outs, *lses, gu, gu, gu, x, w_out, conv_w, g, b)


def _gla_gates(z, lb, tri2):
    f = lb + (1.0 - lb) * _sigmoid(z)
    lf2 = jnp.log(f) * LOG2_E
    hi = lf2.astype(BF16)
    lo = (lf2 - hi.astype(F32)).astype(BF16)
    cum = jnp.dot(tri2, jnp.concatenate([hi, lo], axis=0), preferred_element_type=F32)
    return 1.0 - f, cum


def _gla_scores(q, kk, cum, *, reverse):
    c = GLA_CHUNK
    sb = GLA_SUB
    dk = q.shape[1]
    nt = (((1,), (1,)), ((), ()))
    ti = lax.broadcasted_iota(jnp.int32, (c, c), 0)
    si = lax.broadcasted_iota(jnp.int32, (c, c), 1)
    causal = (si >= ti) if reverse else (ti >= si)

    parts = []
    for blk in range(c // sb):
        rows = slice(blk * sb, (blk + 1) * sb)
        if reverse:
            edge = (blk + 1) * sb
            ref = cum[edge:edge + 1, :] if edge < c else jnp.zeros((1, dk), F32)
            other = slice((blk + 1) * sb, c)
        else:
            edge = blk * sb - 1
            ref = cum[edge:edge + 1, :] if edge >= 0 else jnp.zeros((1, dk), F32)
            other = slice(0, blk * sb)
        qs = q[rows] * jnp.exp2(cum[rows] - ref)
        k_own = (kk[rows] * jnp.exp2(jnp.minimum(ref - cum[rows], GLA_EXP2_CLAMP))).astype(BF16)
        pieces = [k_own]
        n_other = other.stop - other.start
        if n_other:
            k_other = (kk[other] * jnp.exp2(ref - cum[other])).astype(BF16)
            pieces = [k_own, k_other] if reverse else [k_other, k_own]
        if n_other + sb < c:
            pad = jnp.zeros((c - n_other - sb, dk), BF16)
            pieces = [pad] + pieces if reverse else pieces + [pad]
        ks = jnp.concatenate(pieces, axis=0) if len(pieces) > 1 else pieces[0]
        parts.append(lax.dot_general(qs.astype(BF16), ks, nt, preferred_element_type=F32))
    return jnp.where(causal, jnp.concatenate(parts, axis=0), 0.0).astype(BF16)


def _gla_state_terms(q, kk, cum, *, reverse):
    c = GLA_CHUNK
    total = cum[0:1, :] if reverse else cum[c - 1:c, :]
    qe = (q * jnp.exp2(cum)).astype(BF16)
    kd = (kk * jnp.exp2(total - cum)).astype(BF16)
    return qe, kd, jnp.exp2(total)


def _gla_kernel(q_ref, zf_ref, zb_ref, v_ref, g_ref, lbl_ref, ng_ref, o_ref,
                acc_f, acc_b, qe_f, qe_b, kd_f, kd_b, et_f, et_b, att_f, att_b, ring_a, ring_b, *, seq):
    c = GLA_CHUNK
    nc = seq // c
    dk = HG_DIM
    nt = (((1,), (1,)), ((), ()))
    tn = (((0,), (0,)), ((), ()))
    l0 = lbl_ref[0:1, :]
    l1 = lbl_ref[1:2, :]
    mx = jnp.maximum(l0, l1)
    e0, e1 = jnp.exp(l0 - mx), jnp.exp(l1 - mx)
    lb = e0 / (e0 + e1)

    ti = lax.broadcasted_iota(jnp.int32, (c, 2 * c), 0)
    si = lax.broadcasted_iota(jnp.int32, (c, 2 * c), 1) & (c - 1)
    tri_f = jnp.where(ti >= si, 1.0, 0.0).astype(BF16)
    tri_b = jnp.where(si >= ti, 1.0, 0.0).astype(BF16)

    dirs = ((zf_ref, tri_f, att_f, qe_f, kd_f, et_f, False),
            (zb_ref, tri_b, att_b, qe_b, kd_b, et_b, True))
    grp = GLA_INTRA_UNROLL
    per_step = 2 * grp
    n_steps = nc // per_step
    chains = [(u, d) for u in range(grp) for d in range(2)]

    def chunk_of(step, half, u, d):
        ci = step * per_step + half * grp + u
        return (nc - 1 - ci) if d else ci

    def park_gates(step, half, ring):
        for k, (u, d) in enumerate(chains):
            rows = pl.ds(pl.multiple_of(chunk_of(step, half, u, d) * c, c), c)
            kk, cum = _gla_gates(dirs[d][0][0, rows, :].astype(F32), lb, dirs[d][1])
            ring[0, k * c:(k + 1) * c, :] = kk
            ring[1, k * c:(k + 1) * c, :] = cum

    def scores_from_ring(step, half, ring):
        for k, (u, d) in enumerate(chains):
            ci = chunk_of(step, half, u, d)
            rows = pl.ds(pl.multiple_of(ci * c, c), c)
            q = q_ref[0, rows, :].astype(F32)
            kk = ring[0, k * c:(k + 1) * c, :]
            cum = ring[1, k * c:(k + 1) * c, :]
            dirs[d][2][rows, :] = _gla_scores(q, kk, cum, reverse=dirs[d][6])
            qe, kd, et = _gla_state_terms(q, kk, cum, reverse=dirs[d][6])
            dirs[d][3][rows, :] = qe
            dirs[d][4][rows, :] = kd
            dirs[d][5][pl.ds(ci, 1), :] = et

    def intra(step):
        scores_from_ring(step, 0, ring_a)
        scores_from_ring(step, 1, ring_b)
        nxt = jnp.minimum(step + 1, n_steps - 1)
        park_gates(nxt, 0, ring_a)
        park_gates(nxt, 1, ring_b)

    def scan(j, carry):
        sf, sr = carry
        cfs = [j * per_step + u for u in range(per_step)]
        crs = [nc - 1 - cf for cf in cfs]
        rfs = [pl.ds(pl.multiple_of(cf * c, c), c) for cf in cfs]
        rrs = [pl.ds(pl.multiple_of(cr * c, c), c) for cr in crs]
        upd_f = [lax.dot_general(v_ref[0, r, :], kd_f[r, :], tn, preferred_element_type=F32) for r in rfs]
        upd_r = [lax.dot_general(v_ref[0, r, :], kd_b[r, :], tn, preferred_element_type=F32) for r in rrs]
        loc_f = [jnp.dot(att_f[r, :], v_ref[0, r, :], preferred_element_type=F32) for r in rfs]
        loc_r = [jnp.dot(att_b[r, :], v_ref[0, r, :], preferred_element_type=F32) for r in rrs]
        sfs, srs = [sf], [sr]
        for u in range(per_step):
            sfs.append(sfs[-1] * et_f[pl.ds(cfs[u], 1), :] + upd_f[u])
            srs.append(srs[-1] * et_b[pl.ds(crs[u], 1), :] + upd_r[u])
        for u in range(per_step):
            acc_f[rfs[u], :] = loc_f[u] + lax.dot_general(qe_f[rfs[u], :], sfs[u].astype(BF16), nt,
                                                          preferred_element_type=F32)
            acc_b[rrs[u], :] = loc_r[u] + lax.dot_general(qe_b[rrs[u], :], srs[u].astype(BF16), nt,
                                                          preferred_element_type=F32)
        return sfs[-1], srs[-1]

    park_gates(0, 0, ring_a)
    park_gates(0, 1, ring_b)
    intra(0)

    def step(j, carry):
        carry = scan(j - 1, carry)
        intra(j)
        return carry

    zero = jnp.zeros((dk, dk), F32)
    carry = lax.fori_loop(1, n_steps, step, (zero, zero))
    scan(n_steps - 1, carry)

    blk = 512
    ng = ng_ref[...]

    def fin(j, _):
        rows = pl.ds(pl.multiple_of(j * blk, blk), blk)
        o = acc_f[rows, :] + acc_b[rows, :]
        o = o * lax.rsqrt(jnp.mean(o * o, axis=-1, keepdims=True) + RMS_EPS) * ng
        g = g_ref[0, rows, :].astype(F32)
        o_ref[0, rows, :] = (o * (g * _sigmoid(g))).astype(o_ref.dtype)
        return 0

    lax.fori_loop(0, seq // blk, fin, 0)


def _gla(proj, lb_logits, norm_g, batch, seq):
    view = proj.reshape(batch, seq, 5 * D_MODEL)

    def col(seg):
        return pl.BlockSpec((1, seq, HG_DIM), lambda b, h: (b, 0, seg * HG_HEADS + h))

    return pl.pallas_call(
        functools.partial(_gla_kernel, seq=seq),
        grid=(batch, HG_HEADS),
        in_specs=[col(0), col(1), col(2), col(3), col(4),
                  pl.BlockSpec((DEPTH, HG_DIM), lambda b, h: (0, h)),
                  pl.BlockSpec((1, HG_DIM), lambda b, h: (0, h))],
        out_specs=pl.BlockSpec((1, seq, HG_DIM), lambda b, h: (b, 0, h)),
        out_shape=jax.ShapeDtypeStruct((batch, seq, D_MODEL), BF16),
        scratch_shapes=[pltpu.VMEM((seq, HG_DIM), F32), pltpu.VMEM((seq, HG_DIM), F32),
                        pltpu.VMEM((seq, HG_DIM), BF16), pltpu.VMEM((seq, HG_DIM), BF16),
                        pltpu.VMEM((seq, HG_DIM), BF16), pltpu.VMEM((seq, HG_DIM), BF16),
                        pltpu.VMEM((seq // GLA_CHUNK, HG_DIM), F32),
                        pltpu.VMEM((seq // GLA_CHUNK, HG_DIM), F32),
                        pltpu.VMEM((seq, GLA_CHUNK), BF16), pltpu.VMEM((seq, GLA_CHUNK), BF16),
                        pltpu.VMEM((2, 2 * GLA_INTRA_UNROLL * GLA_CHUNK, HG_DIM), F32),
                        pltpu.VMEM((2, 2 * GLA_INTRA_UNROLL * GLA_CHUNK, HG_DIM), F32)],
        compiler_params=_cparams("parallel", "parallel"),
        name="gla",
    )(view, view, view, view, view, lb_logits, norm_g).reshape(batch * seq, D_MODEL)


def _proj_ln_kernel(a_ref, w_ref, x_ref, g_ref, b_ref, y_ref):
    mix = jnp.dot(a_ref[...], w_ref[...], preferred_element_type=F32)
    y_ref[...] = _layer_norm(DEEPNORM_ALPHA * x_ref[...] + mix, g_ref[...], b_ref[...])


def _proj_ln(a, w, x, g, b, tm):
    n = x.shape[0]
    full = lambda t: pl.BlockSpec(t.shape, lambda i: (0, 0))
    row = pl.BlockSpec((tm, D_MODEL), lambda i: (i, 0))
    return pl.pallas_call(
        _proj_ln_kernel,
        grid=(n // tm,),
        in_specs=[row, full(w), row, full(g), full(b)],
        out_specs=row,
        out_shape=jax.ShapeDtypeStruct((n, D_MODEL), F32),
        compiler_params=_cparams("parallel"),
        name="proj_ln",
    )(a, w, x, g, b)


ROUTER_ROWS = 64
ROUTE_OUT_ROWS = 8


def _first_index_of(vals, target, n_rows):
    idx = lax.broadcasted_iota(jnp.int32, vals.shape, 0)
    return jnp.min(jnp.where(vals == target, idx, n_rows), axis=0, keepdims=True)


def _xattn_kernel(x_ref, wq_ref, kv_ref, wo_ref, g_ref, b_ref, wr_ref, br_ref, y_ref, yb_ref, route_ref):
    x = x_ref[...]
    q = jnp.dot(x.astype(BF16), wq_ref[...], preferred_element_type=F32) * (XA_HEAD_DIM ** -0.5)
    qb = q.astype(BF16)
    nt = (((1,), (1,)), ((), ()))
    heads = []
    for h in range(XA_HEADS):
        cols = slice(h * XA_HEAD_DIM, (h + 1) * XA_HEAD_DIM)
        k = kv_ref[0, :, cols]
        v = kv_ref[0, :, D_MODEL + h * XA_HEAD_DIM:D_MODEL + (h + 1) * XA_HEAD_DIM]
        sc = lax.dot_general(qb[:, cols], k, nt, preferred_element_type=F32)
        m = jnp.max(sc, axis=-1, keepdims=True)
        pe = jnp.exp(sc - m)
        p = pe / jnp.sum(pe, axis=-1, keepdims=True)
        heads.append(jnp.dot(p.astype(BF16), v, preferred_element_type=F32).astype(BF16))
    o = jnp.concatenate(heads, axis=1)
    xa = jnp.dot(o, wo_ref[...], preferred_element_type=F32)
    y = _layer_norm(DEEPNORM_ALPHA * x + xa, g_ref[...], b_ref[...])
    y_ref[...] = y
    yb_ref[...] = _pack_bf16_pair(y)

    lg = lax.dot_general(wr_ref[...], y.astype(BF16), nt, preferred_element_type=F32) + br_ref[:, 0:1]
    gl = lg[0:N_GROUPS, :]
    gmax = jnp.max(gl, axis=0, keepdims=True)
    g_w = 1.0 / jnp.sum(jnp.exp(gl - gmax), axis=0, keepdims=True)
    g_sel = _first_index_of(gl, gmax, N_GROUPS)
    el = jnp.zeros((EXPERTS_PER_GROUP, gl.shape[1]), F32)
    for grp in range(N_GROUPS):
        rows = slice(8 + grp * EXPERTS_PER_GROUP, 8 + (grp + 1) * EXPERTS_PER_GROUP)
        el = el + jnp.where(g_sel == grp, lg[rows, :], 0.0)
    m1 = jnp.max(el, axis=0, keepdims=True)
    i1 = _first_index_of(el, m1, EXPERTS_PER_GROUP)
    eidx = lax.broadcasted_iota(jnp.int32, el.shape, 0)
    el2 = jnp.where(eidx == i1, -jnp.inf, el)
    m2 = jnp.max(el2, axis=0, keepdims=True)
    i2 = _first_index_of(el2, m2, EXPERTS_PER_GROUP)
    e2 = jnp.exp(m2 - m1)
    den = 1.0 + e2
    w1 = g_w / den
    w2 = g_w * e2 / den
    base = g_sel * EXPERTS_PER_GROUP
    zero = jnp.zeros_like(w1)
    route_ref[...] = jnp.concatenate(
        [(base + i1).astype(F32), (base + i2).astype(F32), w1, w2, zero, zero, zero, zero], axis=0)


def _xattn(x, kv, wq, wo, g, b, wr, br, seq, tm):
    n = x.shape[0]
    spt = seq // tm
    full = lambda t: pl.BlockSpec(t.shape, lambda i: (0, 0))
    row = pl.BlockSpec((tm, D_MODEL), lambda i: (i, 0))
    return pl.pallas_call(
        _xattn_kernel,
        grid=(n // tm,),
        in_specs=[row, full(wq),
                  pl.BlockSpec((1,) + kv.shape[1:], lambda i: (i // spt, 0, 0)),
                  full(wo), full(g), full(b), full(wr), full(br)],
        out_specs=[row, pl.BlockSpec((tm, D_MODEL // 2), lambda i: (i, 0)),
                   pl.BlockSpec((ROUTE_OUT_ROWS, tm), lambda i: (0, i))],
        out_shape=[jax.ShapeDtypeStruct((n, D_MODEL), F32),
                   jax.ShapeDtypeStruct((n, D_MODEL // 2), jnp.int32),
                   jax.ShapeDtypeStruct((ROUTE_OUT_ROWS, n), F32)],
        compiler_params=_cparams("parallel"),
        name="xattn_router",
    )(x, wq, kv, wo, g, b, wr, br)


def _expert_kernel(te_ref, nu_ref, xs_ref, w1_ref, w3_ref, w2_ref, ys_ref):
    j = pl.program_id(0)

    @pl.when(j < nu_ref[0])
    def _():
        half = D_MODEL // 2
        x_hi, x_lo = _unpack_bf16_pair(xs_ref[...])
        x_hi, x_lo = x_hi.astype(BF16), x_lo.astype(BF16)

        def up(w_ref):
            return (jnp.dot(x_hi, w_ref[0, 0:half, :].astype(BF16), preferred_element_type=F32)
                    + jnp.dot(x_lo, w_ref[0, half:D_MODEL, :].astype(BF16), preferred_element_type=F32))

        h1 = up(w1_ref)
        hid = h1 * _sigmoid(h1) * up(w3_ref)
        ys_ref[...] = _pack_bf16_pair(
            jnp.dot(hid.astype(BF16), w2_ref[0].astype(BF16), preferred_element_type=F32))

    @pl.when(j >= nu_ref[0])
    def _():
        ys_ref[...] = jnp.zeros_like(ys_ref)


def _experts(xs, tile_expert, n_used, w1, w3, w2, tm):
    mp = xs.shape[0]
    w_in = pl.BlockSpec((1, D_MODEL, EXPERT_FF), lambda j, te, nu: (te[j], 0, 0))
    rows = pl.BlockSpec((tm, D_MODEL // 2), lambda j, te, nu: (j, 0))
    grid_spec = pltpu.PrefetchScalarGridSpec(
        num_scalar_prefetch=2,
        grid=(mp // tm,),
        in_specs=[rows, w_in, w_in,
                  pl.BlockSpec((1, EXPERT_FF, D_MODEL), lambda j, te, nu: (te[j], 0, 0))],
        out_specs=rows,
    )
    return pl.pallas_call(
        _expert_kernel,
        grid_spec=grid_spec,
        out_shape=jax.ShapeDtypeStruct((mp, D_MODEL // 2), jnp.int32),
        compiler_params=_cparams("arbitrary"),
        name="experts",
    )(tile_expert, n_used, xs, w1, w3, w2)


def _moe_ln_kernel(x_ref, y0_ref, y1_ref, gate_ref, g_ref, b_ref, o_ref):
    g0, g1 = gate_ref[:, 0:1], gate_ref[:, 1:2]
    hi0, lo0 = _unpack_bf16_pair(y0_ref[...])
    hi1, lo1 = _unpack_bf16_pair(y1_ref[...])
    ff = jnp.concatenate([g0 * hi0 + g1 * hi1, g0 * lo0 + g1 * lo1], axis=1)
    o_ref[...] = _layer_norm(DEEPNORM_ALPHA * x_ref[...] + ff, g_ref[...], b_ref[...])


def _moe_ln(x, y01, gates, g, b, tm):
    n = x.shape[0]
    full = lambda t: pl.BlockSpec(t.shape, lambda i: (0, 0))
    row = pl.BlockSpec((tm, D_MODEL), lambda i: (i, 0))
    second = n // tm
    return pl.pallas_call(
        _moe_ln_kernel,
        grid=(n // tm,),
        in_specs=[row,
                  pl.BlockSpec((tm, D_MODEL // 2), lambda i: (i, 0)),
                  pl.BlockSpec((tm, D_MODEL // 2), lambda i: (i + second, 0)),
                  pl.BlockSpec((tm, 2), lambda i: (i, 0)), full(g), full(b)],
        out_specs=row,
        out_shape=jax.ShapeDtypeStruct((n, D_MODEL), F32),
        compiler_params=_cparams("parallel"),
        name="moe_ln",
    )(x, y01, y01, gates, g, b)


SC_CORES = 2
SC_SUBCORES = 16
SC_GATHER_ROWS = 32


def _sc_gather_rows(table, idx):
    n_out = idx.shape[0]
    width = table.shape[1]
    workers = SC_CORES * SC_SUBCORES
    ch = SC_GATHER_ROWS
    per_w = n_out // workers
    steps = per_w // ch
    assert per_w * workers == n_out and steps * ch == per_w and steps % 2 == 0
    mesh = plsc.VectorSubcoreMesh(core_axis_name="c", subcore_axis_name="s")

    @functools.partial(
        pl.kernel, mesh=mesh,
        out_type=jax.ShapeDtypeStruct((n_out, width), table.dtype),
        scratch_types=[pltpu.VMEM((per_w,), jnp.int32),
                       pltpu.VMEM((ch, width), table.dtype), pltpu.VMEM((ch, width), table.dtype),
                       pltpu.SemaphoreType.DMA, pltpu.SemaphoreType.DMA,
                       pltpu.SemaphoreType.DMA, pltpu.SemaphoreType.DMA],
    )
    def gather_kernel(table_hbm, idx_hbm, out_hbm, idx_v, rows0, rows1, g0, g1, w0, w1):
        wid = lax.axis_index("s") * SC_CORES + lax.axis_index("c")
        base = wid * per_w
        pltpu.sync_copy(idx_hbm.at[pl.ds(base, per_w)], idx_v)
        bufs = ((rows0, g0, w0), (rows1, g1, w1))

        def gather(i, b):
            return pltpu.make_async_copy(table_hbm.at[idx_v.at[pl.ds(i * ch, ch)]], bufs[b][0], bufs[b][1])

        def write(i, b):
            return pltpu.make_async_copy(bufs[b][0], out_hbm.at[pl.ds(base + i * ch, ch)], bufs[b][2])

        gather(0, 0).start()

        @pl.loop(0, steps, step=2)
        def _(i):
            for b in range(2):
                ii = i + b
                gather(ii, b).wait()

                @pl.when(ii >= 1)
                def _():
                    write(ii - 1, 1 - b).wait()

                @pl.when(ii + 1 < steps)
                def _():
                    gather(ii + 1, 1 - b).start()

                write(ii, b).start()

        write(steps - 1, 1).wait()

    return gather_kernel(table, idx)


def _dispatch_plan(ids, n, tm_e):
    n_asg = 2 * n
    mp = n_asg + N_EXPERTS * tm_e
    order = jnp.argsort(ids, stable=True).astype(jnp.int32)
    pos = jnp.argsort(order).astype(jnp.int32)
    onehot = (ids[:, None] == jnp.arange(N_EXPERTS, dtype=jnp.int32)[None, :]).astype(jnp.int32)
    counts = jnp.sum(onehot, axis=0)
    dense_start = jnp.cumsum(counts) - counts
    padded = ((counts + tm_e - 1) // tm_e) * tm_e
    row_end = jnp.cumsum(padded)
    row_start = row_end - padded
    row_of_asg = pos + jnp.sum(onehot * (row_start - dense_start)[None, :], axis=1)
    tile_start = jnp.arange(mp // tm_e, dtype=jnp.int32) * tm_e
    tile_expert = jnp.minimum(jnp.sum((tile_start[:, None] >= row_end[None, :]).astype(jnp.int32), axis=1),
                              N_EXPERTS - 1)
    shift = (dense_start - row_start)[tile_expert]
    src = (tile_start + shift)[:, None] + jnp.arange(tm_e, dtype=jnp.int32)[None, :]
    asg_of_row = order[jnp.clip(src.reshape(mp), 0, n_asg - 1)]
    tok_of_row = jnp.where(asg_of_row >= n, asg_of_row - n, asg_of_row)
    n_used = (row_end[-1] // tm_e).astype(jnp.int32).reshape(1)
    return tok_of_row, row_of_asg, tile_expert.astype(jnp.int32), n_used


def _moe_steps(x, xp, route, layer, w1, w3, w2, g, b, tm_e, tm):
    n = x.shape[0]
    ids = route[0:2].astype(jnp.int32).reshape(2 * n)
    tok_of_row, row_of_asg, tile_expert, n_used = _dispatch_plan(ids, n, tm_e)
    xs = _sc_gather_rows(xp, tok_of_row)
    yield
    ys = _experts(xs, tile_expert + layer * N_EXPERTS, n_used, w1, w3, w2, tm_e)
    y01 = _sc_gather_rows(ys, row_of_asg)
    yield
    return _moe_ln(x, y01, route[2:4].T, g, b, tm)


def _router_weights(w_group, b_group, w_expert, b_expert):
    wr = jnp.zeros((ROUTER_ROWS, D_MODEL), F32)
    wr = wr.at[0:N_GROUPS].set(w_group.T).at[8:8 + N_EXPERTS].set(w_expert.T)
    br = jnp.zeros((ROUTER_ROWS, LANES), F32)
    br = br.at[0:N_GROUPS, :].set(b_group[:, None]).at[8:8 + N_EXPERTS, :].set(b_expert[:, None])
    return wr.astype(BF16), br


def _trunk_steps(x3, mem3, p):
    batch, seq, _ = x3.shape
    n = batch * seq
    x = x3.reshape(n, D_MODEL)
    mem = mem3.reshape(batch * mem3.shape[1], D_MODEL)
    tm = 512
    tables = _rope_tables(seq)
    for layer in range(DEPTH):
        j = layer // 2
        row = lambda a: a.reshape(1, D_MODEL)
        if layer % 2 == 0:
            *views, gu = _even_proj(x, p["ev_w_in"][j], tables, seq, tm)
            res = [_band_attention(v, batch, seq, dil) for v, (_, dil) in zip(views, DILATED_PATTERNS)]
            x = _even_out([o for o, _ in res], [l for _, l in res], gu, x, p["ev_w_out"][j],
                          p["ev_conv_w"][j], row(p["ln_g"][layer, 0]), row(p["ln_b"][layer, 0]), seq, tm)
        else:
            proj = _matmul(x, p["od_w_in"][j], BF16, tm, D_MODEL)
            o = _gla(proj, p["lb_logits"], p["od_norm_g"][j].reshape(1, D_MODEL), batch, seq)
            x = _proj_ln(o, p["od_w_out"][j], x, row(p["ln_g"][layer, 0]), row(p["ln_b"][layer, 0]), tm)
        kv = _matmul(mem, p["xa_w_kv"][layer], BF16, 256, D_MODEL).reshape(batch, mem3.shape[1], 2 * D_MODEL)
        wr, br = _router_weights(p["moe_w_group"][layer], p["moe_b_group"][layer],
                                 p["moe_w_expert"][layer], p["moe_b_expert"][layer])
        x, xp, route = _xattn(x, kv, p["xa_w_q"][layer], p["xa_w_out"][layer],
                              row(p["ln_g"][layer, 1]), row(p["ln_b"][layer, 1]), wr, br, seq, tm)
        x = yield from _moe_steps(x, xp, route, layer, p["moe_w1"], p["moe_w3"], p["moe_w2"],
                                  row(p["ln_g"][layer, 2]), row(p["ln_b"][layer, 2]), 512, tm)
    return x.reshape(batch, seq, D_MODEL)


def _run_interleaved(generators):
    results = [None] * len(generators)
    live = list(range(len(generators)))
    while live:
        for k in list(live):
            try:
                next(generators[k])
            except StopIteration as stop:
                results[k] = stop.value
                live.remove(k)
    return results


def _trunk(x3, mem3, p):
    return _run_interleaved([_trunk_steps(x3, mem3, p)])[0]


def kernel(x_prompt, x_sample, mem_prompt, mem_sample, ev_w_in, ev_conv_w, ev_w_out, od_w_in, lb_logits,
           od_norm_g, od_w_out, xa_w_q, xa_w_kv, xa_w_out, moe_w_group, moe_b_group, moe_w_expert,
           moe_b_expert, moe_w1, moe_w3, moe_w2, ln_g, ln_b):
    ff = moe_w1.shape[-1]
    p = dict(
        ev_w_in=ev_w_in.astype(BF16), ev_conv_w=ev_conv_w, ev_w_out=ev_w_out.astype(BF16),
        od_w_in=od_w_in.astype(BF16), lb_logits=lb_logits, od_norm_g=od_norm_g,
        od_w_out=od_w_out.astype(BF16), xa_w_q=xa_w_q.astype(BF16), xa_w_kv=xa_w_kv.astype(BF16),
        xa_w_out=xa_w_out.astype(BF16), moe_w_group=moe_w_group, moe_b_group=moe_b_group,
        moe_w_expert=moe_w_expert, moe_b_expert=moe_b_expert,
        moe_w1=moe_w1.reshape(DEPTH * N_EXPERTS, D_MODEL, ff),
        moe_w3=moe_w3.reshape(DEPTH * N_EXPERTS, D_MODEL, ff),
        moe_w2=moe_w2.reshape(DEPTH * N_EXPERTS, ff, D_MODEL),
        ln_g=ln_g, ln_b=ln_b)
    y_prompt, y_sample = _run_interleaved([_trunk_steps(x_prompt, mem_prompt, p),
                                           _trunk_steps(x_sample, mem_sample, p)])
    return y_prompt, y_sample
```

```python
import functools
import math

import jax
import jax.numpy as jnp
from jax import lax
from jax.experimental import pallas as pl
from jax.experimental.pallas import tpu as pltpu
from jax.experimental.pallas import tpu_sc as plsc

F32 = jnp.float32
BF16 = jnp.bfloat16

D_MODEL = 1024
DEPTH = 2
A_HEADS = 8
A_HEAD_DIM = 64
A_WIDTH = A_HEADS * A_HEAD_DIM
DILATED_PATTERNS = ((128, 1), (512, 4), (2048, 16))
ROPE_THETA = 500000.0
ROPE_DIM = A_HEAD_DIM // 4
B_WIDTH = D_MODEL // 2
CONV_WIDTH = 3
HG_HEADS = 8
HG_DIM = D_MODEL // HG_HEADS
XA_HEADS = 4
XA_HEAD_DIM = D_MODEL // XA_HEADS
N_GROUPS = 4
EXPERTS_PER_GROUP = 8
N_EXPERTS = N_GROUPS * EXPERTS_PER_GROUP
EXPERT_FF = D_MODEL // 4
LN_EPS = 1e-5
RMS_EPS = 1e-6
DEEPNORM_ALPHA = (2 * DEPTH) ** 0.25

LANES = 128
BAND_RADIUS = 64
ATTN_QBLOCK = 128
NEG_BIG = -1e30
GLA_CHUNK = 64
GLA_SUB = 16
GLA_EXP2_CLAMP = 100.0
GLA_INTRA_UNROLL = 2
LOG2_E = 1.4426950408889634
VMEM_LIMIT = 56 * 1024 * 1024


def _cparams(*sem):
    return pltpu.CompilerParams(dimension_semantics=sem, vmem_limit_bytes=VMEM_LIMIT)


def _layer_norm(y, g, b):
    mu = jnp.mean(y, axis=-1, keepdims=True)
    d = y - mu
    var = jnp.mean(d * d, axis=-1, keepdims=True)
    return d * lax.rsqrt(var + LN_EPS) * g + b


def _sigmoid(z):
    return 1.0 / (1.0 + jnp.exp(-z))


def _pack_bf16_pair(y):
    w = y.shape[1] // 2
    hi = lax.bitcast_convert_type(y[:, :w].astype(BF16).astype(F32), jnp.int32)
    lo = lax.bitcast_convert_type(y[:, w:].astype(BF16).astype(F32), jnp.int32)
    return hi | lax.shift_right_logical(lo, 16)


def _unpack_bf16_pair(p):
    hi = lax.bitcast_convert_type(p & jnp.int32(-65536), F32)
    lo = lax.bitcast_convert_type(lax.shift_left(p, 16), F32)
    return hi, lo


def _mm_kernel(x_ref, w_ref, o_ref, *, chunk):
    xb = x_ref[...].astype(BF16)
    for c in range(w_ref.shape[1] // chunk):
        cols = slice(c * chunk, (c + 1) * chunk)
        o_ref[:, cols] = jnp.dot(xb, w_ref[:, cols], preferred_element_type=F32).astype(o_ref.dtype)


def _matmul(x, w, out_dtype, tm, chunk):
    n, k = x.shape
    m = w.shape[1]
    return pl.pallas_call(
        functools.partial(_mm_kernel, chunk=chunk),
        grid=(n // tm,),
        in_specs=[pl.BlockSpec((tm, k), lambda i: (i, 0)), pl.BlockSpec((k, m), lambda i: (0, 0))],
        out_specs=pl.BlockSpec((tm, m), lambda i: (i, 0)),
        out_shape=jax.ShapeDtypeStruct((n, m), out_dtype),
        compiler_params=_cparams("parallel"),
        name="matmul",
    )(x, w)


def _rope_tables(seq):
    half = ROPE_DIM // 2
    inv_freq = jnp.exp(-math.log(ROPE_THETA) * jnp.arange(half, dtype=F32) * (2.0 / ROPE_DIM))
    ang = jnp.arange(seq, dtype=F32)[:, None] * inv_freq[None, :]
    cos, sin = jnp.cos(ang), jnp.sin(ang)
    ones = jnp.ones((seq, A_HEAD_DIM - ROPE_DIM), F32)
    zeros = jnp.zeros((seq, A_HEAD_DIM - ROPE_DIM), F32)
    zh = jnp.zeros((seq, half), F32)
    c = jnp.concatenate([cos, cos, ones], -1)
    s_up = jnp.concatenate([-sin, zh, zeros], -1)
    s_dn = jnp.concatenate([zh, sin, zeros], -1)
    rep = LANES // A_HEAD_DIM
    return tuple(jnp.tile(t, (1, rep)) for t in (c, s_up, s_dn))


def _even_proj_kernel(x_ref, w_ref, c_ref, su_ref, sd_ref, qkv_ref, qkv4_ref, qkv16_ref, gu_ref, slab_ref):
    tm = x_ref.shape[0]
    xb = x_ref[...].astype(BF16)
    rep = A_WIDTH // LANES
    half = ROPE_DIM // 2
    c = jnp.tile(c_ref[...], (1, rep))
    su = jnp.tile(su_ref[...], (1, rep))
    sd = jnp.tile(sd_ref[...], (1, rep))

    def proj(j):
        return jnp.dot(xb, w_ref[:, j * A_WIDTH:(j + 1) * A_WIDTH], preferred_element_type=F32)

    def rope(t):
        up = pltpu.roll(t, A_WIDTH - half, axis=1)
        dn = pltpu.roll(t, half, axis=1)
        return t * c + up * su + dn * sd

    qkv = (rope(proj(0)) * (A_HEAD_DIM ** -0.5), rope(proj(1)), proj(2))
    per = A_WIDTH // LANES
    for j, part in enumerate(qkv):
        qkv_ref[:, j * A_WIDTH:(j + 1) * A_WIDTH] = part.astype(BF16)
        for s in range(per):
            slab_ref[j * per + s] = part[:, s * LANES:(s + 1) * LANES]
    for dil, out_ref in ((DILATED_PATTERNS[1][1], qkv4_ref), (DILATED_PATTERNS[2][1], qkv16_ref)):
        for r in range(dil):
            for s in range(3 * per):
                val = slab_ref[s, pl.ds(r, tm // dil, stride=dil), :]
                col = r * 3 * A_WIDTH + s * LANES
                out_ref[:, col:col + LANES] = val.astype(BF16)
    gu_ref[:, 0:B_WIDTH] = proj(3).astype(BF16)
    gu_ref[:, B_WIDTH:2 * B_WIDTH] = (proj(4) * proj(5)).astype(BF16)


def _even_proj(x, w_in, tables, seq, tm):
    n = x.shape[0]
    spt = seq // tm
    tab_spec = pl.BlockSpec((tm, LANES), lambda i: (i % spt, 0))
    d4, d16 = DILATED_PATTERNS[1][1], DILATED_PATTERNS[2][1]
    width = 3 * A_WIDTH
    return pl.pallas_call(
        _even_proj_kernel,
        grid=(n // tm,),
        in_specs=[pl.BlockSpec((tm, D_MODEL), lambda i: (i, 0)),
                  pl.BlockSpec(w_in.shape, lambda i: (0, 0)),
                  tab_spec, tab_spec, tab_spec],
        out_specs=[pl.BlockSpec((tm, width), lambda i: (i, 0)),
                   pl.BlockSpec((tm // d4, d4 * width), lambda i: (i, 0)),
                   pl.BlockSpec((tm // d16, d16 * width), lambda i: (i, 0)),
                   pl.BlockSpec((tm, 2 * B_WIDTH), lambda i: (i, 0))],
        out_shape=[jax.ShapeDtypeStruct((n, width), BF16),
                   jax.ShapeDtypeStruct((n // d4, d4 * width), BF16),
                   jax.ShapeDtypeStruct((n // d16, d16 * width), BF16),
                   jax.ShapeDtypeStruct((n, 2 * B_WIDTH), BF16)],
        scratch_shapes=[pltpu.VMEM((width // LANES, tm, LANES), F32)],
        compiler_params=_cparams("parallel"),
        name="even_proj",
    )(x, w_in, *tables)


def _band_attn_kernel(q_ref, kp_ref, km_ref, kn_ref, vp_ref, vm_ref, vn_ref, o_ref, lse_ref,
                      kbuf, vbuf, *, tq, length):
    i = pl.program_id(2)
    r = BAND_RADIUS
    kbuf[0:r] = kp_ref[0]
    kbuf[r:r + tq] = km_ref[0]
    kbuf[r + tq:r + tq + r] = kn_ref[0]
    vbuf[0:r] = vp_ref[0]
    vbuf[r:r + tq] = vm_ref[0]
    vbuf[r + tq:r + tq + r] = vn_ref[0]

    qb = ATTN_QBLOCK
    kw = qb + 2 * r
    qi = lax.broadcasted_iota(jnp.int32, (qb, kw), 0)
    kj = lax.broadcasted_iota(jnp.int32, (qb, kw), 1)
    rel = kj - qi
    band = (rel >= 0) & (rel <= 2 * r)
    lane = lax.broadcasted_iota(jnp.int32, (qb, LANES), 1)
    low = lane < A_HEAD_DIM
    nt = (((1,), (1,)), ((), ()))

    for s in range(tq // qb):
        kpos = i * tq + (s * qb - r) + kj
        valid = band & (kpos >= 0) & (kpos < length)
        bias = jnp.where(valid, 0.0, NEG_BIG)
        rows = slice(s * qb, (s + 1) * qb)
        wrows = slice(s * qb, s * qb + kw)
        for p in range(A_WIDTH // LANES):
            cols = slice(p * LANES, (p + 1) * LANES)
            qp = q_ref[0, rows, cols]
            kwin = kbuf[wrows, cols]
            vwin = vbuf[wrows, cols]
            outs, lses = [], []
            for sel in (low, jnp.logical_not(low)):
                qm = jnp.where(sel, qp, jnp.zeros_like(qp))
                sc = lax.dot_general(qm, kwin, nt, preferred_element_type=F32) + bias
                m = jnp.max(sc, axis=-1, keepdims=True)
                pe = jnp.exp(sc - m)
                l = jnp.sum(pe, axis=-1, keepdims=True)
                pv = jnp.dot(pe.astype(BF16), vwin, preferred_element_type=F32)
                outs.append(pv / l)
                lses.append(jnp.broadcast_to(m + jnp.log(l), (qb, LANES)))
            o_ref[0, rows, cols] = jnp.where(low, outs[0], outs[1]).astype(o_ref.dtype)
            lse_ref[0, rows, cols] = jnp.where(low, lses[0], lses[1])


def _band_attention(qkv_view, batch, seq, dil):
    length = seq // dil
    tq = min(512, length)
    r = BAND_RADIUS
    view = qkv_view.reshape(batch, length, dil * 3 * A_WIDTH)
    nblk_h = length // r
    per = tq // r

    def main(j):
        return pl.BlockSpec((1, tq, A_WIDTH), lambda b, rr, i: (b, i, rr * 3 + j))

    def prev(j):
        return pl.BlockSpec((1, r, A_WIDTH), lambda b, rr, i: (b, jnp.maximum(i * per - 1, 0), rr * 3 + j))

    def nxt(j):
        return pl.BlockSpec((1, r, A_WIDTH),
                            lambda b, rr, i: (b, jnp.minimum((i + 1) * per, nblk_h - 1), rr * 3 + j))

    out_spec = pl.BlockSpec((1, tq, A_WIDTH), lambda b, rr, i: (b, i, rr))
    o, lse = pl.pallas_call(
        functools.partial(_band_attn_kernel, tq=tq, length=length),
        grid=(batch, dil, length // tq),
        in_specs=[main(0), prev(1), main(1), nxt(1), prev(2), main(2), nxt(2)],
        out_specs=[out_spec, out_spec],
        out_shape=[jax.ShapeDtypeStruct((batch, length, dil * A_WIDTH), BF16),
                   jax.ShapeDtypeStruct((batch, length, dil * A_WIDTH), F32)],
        scratch_shapes=[pltpu.VMEM((tq + 2 * r, A_WIDTH), BF16), pltpu.VMEM((tq + 2 * r, A_WIDTH), BF16)],
        compiler_params=_cparams("parallel", "parallel", "parallel"),
        name=f"band_attn_d{dil}",
    )(view, view, view, view, view, view, view)
    return o.reshape(batch * length, dil * A_WIDTH), lse.reshape(batch * length, dil * A_WIDTH)


def _even_out_kernel(o1, o4, o16, l1, l4, l16, gu_ref, up_ref, un_ref, x_ref, w_ref, cw_ref, g_ref, b_ref,
                     y_ref, so4, sl4, so16, sl16, *, tm, seq):
    i = pl.program_id(0)
    pos = (i * tm) % seq
    per = A_WIDTH // LANES
    for dil, o_ref, l_ref, so, sl in ((DILATED_PATTERNS[1][1], o4, l4, so4, sl4),
                                      (DILATED_PATTERNS[2][1], o16, l16, so16, sl16)):
        for r in range(dil):
            for s in range(per):
                cols = slice(r * A_WIDTH + s * LANES, r * A_WIDTH + (s + 1) * LANES)
                so[s, pl.ds(r, tm // dil, stride=dil), :] = o_ref[:, cols].astype(F32)
                sl[s, pl.ds(r, tm // dil, stride=dil), :] = l_ref[:, cols]
    slabs = []
    for s in range(per):
        cols = slice(s * LANES, (s + 1) * LANES)
        la, lb, lc = l1[:, cols], sl4[s], sl16[s]
        mx = jnp.maximum(jnp.maximum(la, lb), lc)
        ea, eb, ec = jnp.exp(la - mx), jnp.exp(lb - mx), jnp.exp(lc - mx)
        num = ea * o1[:, cols].astype(F32) + eb * so4[s] + ec * so16[s]
        slabs.append(num / (ea + eb + ec))
    a_out = jnp.concatenate(slabs, axis=1)

    gate_b = gu_ref[:, 0:B_WIDTH].astype(F32)
    u = gu_ref[:, B_WIDTH:2 * B_WIDTH].astype(F32)
    hrows = up_ref.shape[0]
    u_before = jnp.where(pos > 0, up_ref[hrows - 1:hrows, :].astype(F32), 0.0)
    u_after = jnp.where(pos + tm < seq, un_ref[0:1, :].astype(F32), 0.0)
    row = lax.broadcasted_iota(jnp.int32, (tm, B_WIDTH), 0)
    u_prev = jnp.where(row == 0, u_before, pltpu.roll(u, 1, axis=0))
    u_next = jnp.where(row == tm - 1, u_after, pltpu.roll(u, tm - 1, axis=0))
    conv = u_prev * cw_ref[0:1, :] + u * cw_ref[1:2, :] + u_next * cw_ref[2:3, :]
    b_out = gate_b * conv

    mix = jnp.dot(a_out.astype(BF16), w_ref[0:A_WIDTH, :], preferred_element_type=F32)
    mix = mix + jnp.dot(b_out.astype(BF16), w_ref[A_WIDTH:A_WIDTH + B_WIDTH, :], preferred_element_type=F32)
    y_ref[...] = _layer_norm(DEEPNORM_ALPHA * x_ref[...] + mix, g_ref[...], b_ref[...])


def _even_out(outs, lses, gu, x, w_out, conv_w, g, b, seq, tm):
    n = x.shape[0]
    hrows = 16
    nh = n // hrows
    per = tm // hrows
    d4, d16 = DILATED_PATTERNS[1][1], DILATED_PATTERNS[2][1]
    views = [pl.BlockSpec((tm // d, d * A_WIDTH), lambda i: (i, 0)) for d in (1, d4, d16)]
    full = lambda a: pl.BlockSpec(a.shape, lambda i: (0, 0))
    slab = pltpu.VMEM((A_WIDTH // LANES, tm, LANES), F32)
    return pl.pallas_call(
        functools.partial(_even_out_kernel, tm=tm, seq=seq),
        grid=(n // tm,),
        scratch_shapes=[slab, slab, slab, slab],
        in_specs=views + views + [
            pl.BlockSpec((tm, 2 * B_WIDTH), lambda i: (i, 0)),
            pl.BlockSpec((hrows, B_WIDTH), lambda i: (jnp.maximum(i * per - 1, 0), 1)),
            pl.BlockSpec((hrows, B_WIDTH), lambda i: (jnp.minimum((i + 1) * per, nh - 1), 1)),
            pl.BlockSpec((tm, D_MODEL), lambda i: (i, 0)),
            full(w_out), full(conv_w), full(g), full(b)],
        out_specs=pl.BlockSpec((tm, D_MODEL), lambda i: (i, 0)),
        out_shape=jax.ShapeDtypeStruct((n, D_MODEL), F32),
        compiler_params=_cparams("parallel"),
        name="even_out",
    )(*outs, *lses, gu, gu, gu, x, w_out, conv_w, g, b)


def _gla_gates(z, lb, tri2):
    f = lb + (1.0 - lb) * _sigmoid(z)
    lf2 = jnp.log(f) * LOG2_E
    hi = lf2.astype(BF16)
    lo = (lf2 - hi.astype(F32)).astype(BF16)
    cum = jnp.dot(tri2, jnp.concatenate([hi, lo], axis=0), preferred_element_type=F32)
    return 1.0 - f, cum


def _gla_scores(q, kk, cum, *, reverse):
    c = GLA_CHUNK
    sb = GLA_SUB
    dk = q.shape[1]
    nt = (((1,), (1,)), ((), ()))
    ti = lax.broadcasted_iota(jnp.int32, (c, c), 0)
    si = lax.broadcasted_iota(jnp.int32, (c, c), 1)
    causal = (si >= ti) if reverse else (ti >= si)

    parts = []
    for blk in range(c // sb):
        rows = slice(blk * sb, (blk + 1) * sb)
        if reverse:
            edge = (blk + 1) * sb
            ref = cum[edge:edge + 1, :] if edge < c else jnp.zeros((1, dk), F32)
            other = slice((blk + 1) * sb, c)
        else:
            edge = blk * sb - 1
            ref = cum[edge:edge + 1, :] if edge >= 0 else jnp.zeros((1, dk), F32)
            other = slice(0, blk * sb)
        qs = q[rows] * jnp.exp2(cum[rows] - ref)
        k_own = (kk[rows] * jnp.exp2(jnp.minimum(ref - cum[rows], GLA_EXP2_CLAMP))).astype(BF16)
        pieces = [k_own]
        n_other = other.stop - other.start
        if n_other:
            k_other = (kk[other] * jnp.exp2(ref - cum[other])).astype(BF16)
            pieces = [k_own, k_other] if reverse else [k_other, k_own]
        if n_other + sb < c:
            pad = jnp.zeros((c - n_other - sb, dk), BF16)
            pieces = [pad] + pieces if reverse else pieces + [pad]
        ks = jnp.concatenate(pieces, axis=0) if len(pieces) > 1 else pieces[0]
        parts.append(lax.dot_general(qs.astype(BF16), ks, nt, preferred_element_type=F32))
    return jnp.where(causal, jnp.concatenate(parts, axis=0), 0.0).astype(BF16)


def _gla_state_terms(q, kk, cum, *, reverse):
    c = GLA_CHUNK
    total = cum[0:1, :] if reverse else cum[c - 1:c, :]
    qe = (q * jnp.exp2(cum)).astype(BF16)
    kd = (kk * jnp.exp2(total - cum)).astype(BF16)
    return qe, kd, jnp.exp2(total)


def _gla_kernel(q_ref, zf_ref, zb_ref, v_ref, g_ref, lbl_ref, ng_ref, o_ref,
                acc_f, acc_b, qe_f, qe_b, kd_f, kd_b, et_f, et_b, att_f, att_b, ring_a, ring_b, *, seq):
    c = GLA_CHUNK
    nc = seq // c
    dk = HG_DIM
    nt = (((1,), (1,)), ((), ()))
    tn = (((0,), (0,)), ((), ()))
    l0 = lbl_ref[0:1, :]
    l1 = lbl_ref[1:2, :]
    mx = jnp.maximum(l0, l1)
    e0, e1 = jnp.exp(l0 - mx), jnp.exp(l1 - mx)
    lb = e0 / (e0 + e1)

    ti = lax.broadcasted_iota(jnp.int32, (c, 2 * c), 0)
    si = lax.broadcasted_iota(jnp.int32, (c, 2 * c), 1) & (c - 1)
    tri_f = jnp.where(ti >= si, 1.0, 0.0).astype(BF16)
    tri_b = jnp.where(si >= ti, 1.0, 0.0).astype(BF16)

    dirs = ((zf_ref, tri_f, att_f, qe_f, kd_f, et_f, False),
            (zb_ref, tri_b, att_b, qe_b, kd_b, et_b, True))
    grp = GLA_INTRA_UNROLL
    per_step = 2 * grp
    n_steps = nc // per_step
    chains = [(u, d) for u in range(grp) for d in range(2)]

    def chunk_of(step, half, u, d):
        ci = step * per_step + half * grp + u
        return (nc - 1 - ci) if d else ci

    def park_gates(step, half, ring):
        for k, (u, d) in enumerate(chains):
            rows = pl.ds(pl.multiple_of(chunk_of(step, half, u, d) * c, c), c)
            kk, cum = _gla_gates(dirs[d][0][0, rows, :].astype(F32), lb, dirs[d][1])
            ring[0, k * c:(k + 1) * c, :] = kk
            ring[1, k * c:(k + 1) * c, :] = cum

    def scores_from_ring(step, half, ring):
        for k, (u, d) in enumerate(chains):
            ci = chunk_of(step, half, u, d)
            rows = pl.ds(pl.multiple_of(ci * c, c), c)
            q = q_ref[0, rows, :].astype(F32)
            kk = ring[0, k * c:(k + 1) * c, :]
            cum = ring[1, k * c:(k + 1) * c, :]
            dirs[d][2][rows, :] = _gla_scores(q, kk, cum, reverse=dirs[d][6])
            qe, kd, et = _gla_state_terms(q, kk, cum, reverse=dirs[d][6])
            dirs[d][3][rows, :] = qe
            dirs[d][4][rows, :] = kd
            dirs[d][5][pl.ds(ci, 1), :] = et

    def intra(step):
        scores_from_ring(step, 0, ring_a)
        scores_from_ring(step, 1, ring_b)
        nxt = jnp.minimum(step + 1, n_steps - 1)
        park_gates(nxt, 0, ring_a)
        park_gates(nxt, 1, ring_b)

    def scan(j, carry):
        sf, sr = carry
        cfs = [j * per_step + u for u in range(per_step)]
        crs = [nc - 1 - cf for cf in cfs]
        rfs = [pl.ds(pl.multiple_of(cf * c, c), c) for cf in cfs]
        rrs = [pl.ds(pl.multiple_of(cr * c, c), c) for cr in crs]
        upd_f = [lax.dot_general(v_ref[0, r, :], kd_f[r, :], tn, preferred_element_type=F32) for r in rfs]
        upd_r = [lax.dot_general(v_ref[0, r, :], kd_b[r, :], tn, preferred_element_type=F32) for r in rrs]
        loc_f = [jnp.dot(att_f[r, :], v_ref[0, r, :], preferred_element_type=F32) for r in rfs]
        loc_r = [jnp.dot(att_b[r, :], v_ref[0, r, :], preferred_element_type=F32) for r in rrs]
        sfs, srs = [sf], [sr]
        for u in range(per_step):
            sfs.append(sfs[-1] * et_f[pl.ds(cfs[u], 1), :] + upd_f[u])
            srs.append(srs[-1] * et_b[pl.ds(crs[u], 1), :] + upd_r[u])
        for u in range(per_step):
            acc_f[rfs[u], :] = loc_f[u] + lax.dot_general(qe_f[rfs[u], :], sfs[u].astype(BF16), nt,
                                                          preferred_element_type=F32)
            acc_b[rrs[u], :] = loc_r[u] + lax.dot_general(qe_b[rrs[u], :], srs[u].astype(BF16), nt,
                                                          preferred_element_type=F32)
        return sfs[-1], srs[-1]

    park_gates(0, 0, ring_a)
    park_gates(0, 1, ring_b)
    intra(0)

    def step(j, carry):
        carry = scan(j - 1, carry)
        intra(j)
        return carry

    zero = jnp.zeros((dk, dk), F32)
    carry = lax.fori_loop(1, n_steps, step, (zero, zero))
    scan(n_steps - 1, carry)

    blk = 512
    ng = ng_ref[...]

    def fin(j, _):
        rows = pl.ds(pl.multiple_of(j * blk, blk), blk)
        o = acc_f[rows, :] + acc_b[rows, :]
        o = o * lax.rsqrt(jnp.mean(o * o, axis=-1, keepdims=True) + RMS_EPS) * ng
        g = g_ref[0, rows, :].astype(F32)
        o_ref[0, rows, :] = (o * (g * _sigmoid(g))).astype(o_ref.dtype)
        return 0

    lax.fori_loop(0, seq // blk, fin, 0)


def _gla(proj, lb_logits, norm_g, batch, seq):
    view = proj.reshape(batch, seq, 5 * D_MODEL)

    def col(seg):
        return pl.BlockSpec((1, seq, HG_DIM), lambda b, h: (b, 0, seg * HG_HEADS + h))

    return pl.pallas_call(
        functools.partial(_gla_kernel, seq=seq),
        grid=(batch, HG_HEADS),
        in_specs=[col(0), col(1), col(2), col(3), col(4),
                  pl.BlockSpec((DEPTH, HG_DIM), lambda b, h: (0, h)),
                  pl.BlockSpec((1, HG_DIM), lambda b, h: (0, h))],
        out_specs=pl.BlockSpec((1, seq, HG_DIM), lambda b, h: (b, 0, h)),
        out_shape=jax.ShapeDtypeStruct((batch, seq, D_MODEL), BF16),
        scratch_shapes=[pltpu.VMEM((seq, HG_DIM), F32), pltpu.VMEM((seq, HG_DIM), F32),
                        pltpu.VMEM((seq, HG_DIM), BF16), pltpu.VMEM((seq, HG_DIM), BF16),
                        pltpu.VMEM((seq, HG_DIM), BF16), pltpu.VMEM((seq, HG_DIM), BF16),
                        pltpu.VMEM((seq // GLA_CHUNK, HG_DIM), F32),
                        pltpu.VMEM((seq // GLA_CHUNK, HG_DIM), F32),
                        pltpu.VMEM((seq, GLA_CHUNK), BF16), pltpu.VMEM((seq, GLA_CHUNK), BF16),
                        pltpu.VMEM((2, 2 * GLA_INTRA_UNROLL * GLA_CHUNK, HG_DIM), F32),
                        pltpu.VMEM((2, 2 * GLA_INTRA_UNROLL * GLA_CHUNK, HG_DIM), F32)],
        compiler_params=_cparams("parallel", "parallel"),
        name="gla",
    )(view, view, view, view, view, lb_logits, norm_g).reshape(batch * seq, D_MODEL)


def _proj_ln_kernel(a_ref, w_ref, x_ref, g_ref, b_ref, y_ref):
    mix = jnp.dot(a_ref[...], w_ref[...], preferred_element_type=F32)
    y_ref[...] = _layer_norm(DEEPNORM_ALPHA * x_ref[...] + mix, g_ref[...], b_ref[...])


def _proj_ln(a, w, x, g, b, tm):
    n = x.shape[0]
    full = lambda t: pl.BlockSpec(t.shape, lambda i: (0, 0))
    row = pl.BlockSpec((tm, D_MODEL), lambda i: (i, 0))
    return pl.pallas_call(
        _proj_ln_kernel,
        grid=(n // tm,),
        in_specs=[row, full(w), row, full(g), full(b)],
        out_specs=row,
        out_shape=jax.ShapeDtypeStruct((n, D_MODEL), F32),
        compiler_params=_cparams("parallel"),
        name="proj_ln",
    )(a, w, x, g, b)


ROUTER_ROWS = 64
ROUTE_OUT_ROWS = 8


def _first_index_of(vals, target, n_rows):
    idx = lax.broadcasted_iota(jnp.int32, vals.shape, 0)
    return jnp.min(jnp.where(vals == target, idx, n_rows), axis=0, keepdims=True)


def _xattn_kernel(x_ref, wq_ref, kv_ref, wo_ref, g_ref, b_ref, wr_ref, br_ref, y_ref, yb_ref, route_ref):
    x = x_ref[...]
    q = jnp.dot(x.astype(BF16), wq_ref[...], preferred_element_type=F32) * (XA_HEAD_DIM ** -0.5)
    qb = q.astype(BF16)
    nt = (((1,), (1,)), ((), ()))
    heads = []
    for h in range(XA_HEADS):
        cols = slice(h * XA_HEAD_DIM, (h + 1) * XA_HEAD_DIM)
        k = kv_ref[0, :, cols]
        v = kv_ref[0, :, D_MODEL + h * XA_HEAD_DIM:D_MODEL + (h + 1) * XA_HEAD_DIM]
        sc = lax.dot_general(qb[:, cols], k, nt, preferred_element_type=F32)
        m = jnp.max(sc, axis=-1, keepdims=True)
        pe = jnp.exp(sc - m)
        p = pe / jnp.sum(pe, axis=-1, keepdims=True)
        heads.append(jnp.dot(p.astype(BF16), v, preferred_element_type=F32).astype(BF16))
    o = jnp.concatenate(heads, axis=1)
    xa = jnp.dot(o, wo_ref[...], preferred_element_type=F32)
    y = _layer_norm(DEEPNORM_ALPHA * x + xa, g_ref[...], b_ref[...])
    y_ref[...] = y
    yb_ref[...] = _pack_bf16_pair(y)

    lg = lax.dot_general(wr_ref[...], y.astype(BF16), nt, preferred_element_type=F32) + br_ref[:, 0:1]
    gl = lg[0:N_GROUPS, :]
    gmax = jnp.max(gl, axis=0, keepdims=True)
    g_w = 1.0 / jnp.sum(jnp.exp(gl - gmax), axis=0, keepdims=True)
    g_sel = _first_index_of(gl, gmax, N_GROUPS)
    el = jnp.zeros((EXPERTS_PER_GROUP, gl.shape[1]), F32)
    for grp in range(N_GROUPS):
        rows = slice(8 + grp * EXPERTS_PER_GROUP, 8 + (grp + 1) * EXPERTS_PER_GROUP)
        el = el + jnp.where(g_sel == grp, lg[rows, :], 0.0)
    m1 = jnp.max(el, axis=0, keepdims=True)
    i1 = _first_index_of(el, m1, EXPERTS_PER_GROUP)
    eidx = lax.broadcasted_iota(jnp.int32, el.shape, 0)
    el2 = jnp.where(eidx == i1, -jnp.inf, el)
    m2 = jnp.max(el2, axis=0, keepdims=True)
    i2 = _first_index_of(el2, m2, EXPERTS_PER_GROUP)
    e2 = jnp.exp(m2 - m1)
    den = 1.0 + e2
    w1 = g_w / den
    w2 = g_w * e2 / den
    base = g_sel * EXPERTS_PER_GROUP
    zero = jnp.zeros_like(w1)
    route_ref[...] = jnp.concatenate(
        [(base + i1).astype(F32), (base + i2).astype(F32), w1, w2, zero, zero, zero, zero], axis=0)


def _xattn(x, kv, wq, wo, g, b, wr, br, seq, tm):
    n = x.shape[0]
    spt = seq // tm
    full = lambda t: pl.BlockSpec(t.shape, lambda i: (0, 0))
    row = pl.BlockSpec((tm, D_MODEL), lambda i: (i, 0))
    return pl.pallas_call(
        _xattn_kernel,
        grid=(n // tm,),
        in_specs=[row, full(wq),
                  pl.BlockSpec((1,) + kv.shape[1:], lambda i: (i // spt, 0, 0)),
                  full(wo), full(g), full(b), full(wr), full(br)],
        out_specs=[row, pl.BlockSpec((tm, D_MODEL // 2), lambda i: (i, 0)),
                   pl.BlockSpec((ROUTE_OUT_ROWS, tm), lambda i: (0, i))],
        out_shape=[jax.ShapeDtypeStruct((n, D_MODEL), F32),
                   jax.ShapeDtypeStruct((n, D_MODEL // 2), jnp.int32),
                   jax.ShapeDtypeStruct((ROUTE_OUT_ROWS, n), F32)],
        compiler_params=_cparams("parallel"),
        name="xattn_router",
    )(x, wq, kv, wo, g, b, wr, br)


def _expert_kernel(te_ref, nu_ref, xs_ref, w1_ref, w3_ref, w2_ref, ys_ref):
    j = pl.program_id(0)

    @pl.when(j < nu_ref[0])
    def _():
        half = D_MODEL // 2
        x_hi, x_lo = _unpack_bf16_pair(xs_ref[...])
        x_hi, x_lo = x_hi.astype(BF16), x_lo.astype(BF16)

        def up(w_ref):
            return (jnp.dot(x_hi, w_ref[0, 0:half, :].astype(BF16), preferred_element_type=F32)
                    + jnp.dot(x_lo, w_ref[0, half:D_MODEL, :].astype(BF16), preferred_element_type=F32))

        h1 = up(w1_ref)
        hid = h1 * _sigmoid(h1) * up(w3_ref)
        ys_ref[...] = _pack_bf16_pair(
            jnp.dot(hid.astype(BF16), w2_ref[0].astype(BF16), preferred_element_type=F32))

    @pl.when(j >= nu_ref[0])
    def _():
        ys_ref[...] = jnp.zeros_like(ys_ref)


def _experts(xs, tile_expert, n_used, w1, w3, w2, tm):
    mp = xs.shape[0]
    w_in = pl.BlockSpec((1, D_MODEL, EXPERT_FF), lambda j, te, nu: (te[j], 0, 0))
    rows = pl.BlockSpec((tm, D_MODEL // 2), lambda j, te, nu: (j, 0))
    grid_spec = pltpu.PrefetchScalarGridSpec(
        num_scalar_prefetch=2,
        grid=(mp // tm,),
        in_specs=[rows, w_in, w_in,
                  pl.BlockSpec((1, EXPERT_FF, D_MODEL), lambda j, te, nu: (te[j], 0, 0))],
        out_specs=rows,
    )
    return pl.pallas_call(
        _expert_kernel,
        grid_spec=grid_spec,
        out_shape=jax.ShapeDtypeStruct((mp, D_MODEL // 2), jnp.int32),
        compiler_params=_cparams("arbitrary"),
        name="experts",
    )(tile_expert, n_used, xs, w1, w3, w2)


def _moe_ln_kernel(x_ref, y0_ref, y1_ref, gate_ref, g_ref, b_ref, o_ref):
    g0, g1 = gate_ref[:, 0:1], gate_ref[:, 1:2]
    hi0, lo0 = _unpack_bf16_pair(y0_ref[...])
    hi1, lo1 = _unpack_bf16_pair(y1_ref[...])
    ff = jnp.concatenate([g0 * hi0 + g1 * hi1, g0 * lo0 + g1 * lo1], axis=1)
    o_ref[...] = _layer_norm(DEEPNORM_ALPHA * x_ref[...] + ff, g_ref[...], b_ref[...])


def _moe_ln(x, y01, gates, g, b, tm):
    n = x.shape[0]
    full = lambda t: pl.BlockSpec(t.shape, lambda i: (0, 0))
    row = pl.BlockSpec((tm, D_MODEL), lambda i: (i, 0))
    second = n // tm
    return pl.pallas_call(
        _moe_ln_kernel,
        grid=(n // tm,),
        in_specs=[row,
                  pl.BlockSpec((tm, D_MODEL // 2), lambda i: (i, 0)),
                  pl.BlockSpec((tm, D_MODEL // 2), lambda i: (i + second, 0)),
                  pl.BlockSpec((tm, 2), lambda i: (i, 0)), full(g), full(b)],
        out_specs=row,
        out_shape=jax.ShapeDtypeStruct((n, D_MODEL), F32),
        compiler_params=_cparams("parallel"),
        name="moe_ln",
    )(x, y01, y01, gates, g, b)


SC_CORES = 2
SC_SUBCORES = 16
SC_GATHER_ROWS = 32


def _sc_gather_rows(table, idx):
    n_out = idx.shape[0]
    width = table.shape[1]
    workers = SC_CORES * SC_SUBCORES
    ch = SC_GATHER_ROWS
    per_w = n_out // workers
    steps = per_w // ch
    assert per_w * workers == n_out and steps * ch == per_w and steps % 2 == 0
    mesh = plsc.VectorSubcoreMesh(core_axis_name="c", subcore_axis_name="s")

    @functools.partial(
        pl.kernel, mesh=mesh,
        out_type=jax.ShapeDtypeStruct((n_out, width), table.dtype),
        scratch_types=[pltpu.VMEM((per_w,), jnp.int32),
                       pltpu.VMEM((ch, width), table.dtype), pltpu.VMEM((ch, width), table.dtype),
                       pltpu.SemaphoreType.DMA, pltpu.SemaphoreType.DMA,
                       pltpu.SemaphoreType.DMA, pltpu.SemaphoreType.DMA],
    )
    def gather_kernel(table_hbm, idx_hbm, out_hbm, idx_v, rows0, rows1, g0, g1, w0, w1):
        wid = lax.axis_index("s") * SC_CORES + lax.axis_index("c")
        base = wid * per_w
        pltpu.sync_copy(idx_hbm.at[pl.ds(base, per_w)], idx_v)
        bufs = ((rows0, g0, w0), (rows1, g1, w1))

        def gather(i, b):
            return pltpu.make_async_copy(table_hbm.at[idx_v.at[pl.ds(i * ch, ch)]], bufs[b][0], bufs[b][1])

        def write(i, b):
            return pltpu.make_async_copy(bufs[b][0], out_hbm.at[pl.ds(base + i * ch, ch)], bufs[b][2])

        gather(0, 0).start()

        @pl.loop(0, steps, step=2)
        def _(i):
            for b in range(2):
                ii = i + b
                gather(ii, b).wait()

                @pl.when(ii >= 1)
                def _():
                    write(ii - 1, 1 - b).wait()

                @pl.when(ii + 1 < steps)
                def _():
                    gather(ii + 1, 1 - b).start()

                write(ii, b).start()

        write(steps - 1, 1).wait()

    return gather_kernel(table, idx)


def _dispatch_plan(ids, n, tm_e):
    n_asg = 2 * n
    mp = n_asg + N_EXPERTS * tm_e
    order = jnp.argsort(ids, stable=True).astype(jnp.int32)
    pos = jnp.argsort(order).astype(jnp.int32)
    onehot = (ids[:, None] == jnp.arange(N_EXPERTS, dtype=jnp.int32)[None, :]).astype(jnp.int32)
    counts = jnp.sum(onehot, axis=0)
    dense_start = jnp.cumsum(counts) - counts
    padded = ((counts + tm_e - 1) // tm_e) * tm_e
    row_end = jnp.cumsum(padded)
    row_start = row_end - padded
    row_of_asg = pos + jnp.sum(onehot * (row_start - dense_start)[None, :], axis=1)
    tile_start = jnp.arange(mp // tm_e, dtype=jnp.int32) * tm_e
    tile_expert = jnp.minimum(jnp.sum((tile_start[:, None] >= row_end[None, :]).astype(jnp.int32), axis=1),
                              N_EXPERTS - 1)
    shift = (dense_start - row_start)[tile_expert]
    src = (tile_start + shift)[:, None] + jnp.arange(tm_e, dtype=jnp.int32)[None, :]
    asg_of_row = order[jnp.clip(src.reshape(mp), 0, n_asg - 1)]
    tok_of_row = jnp.where(asg_of_row >= n, asg_of_row - n, asg_of_row)
    n_used = (row_end[-1] // tm_e).astype(jnp.int32).reshape(1)
    return tok_of_row, row_of_asg, tile_expert.astype(jnp.int32), n_used


def _moe_steps(x, xp, route, layer, w1, w3, w2, g, b, tm_e, tm, baton):
    n = x.shape[0]
    ids = route[0:2].astype(jnp.int32).reshape(2 * n)
    tok_of_row, row_of_asg, tile_expert, n_used = _dispatch_plan(ids, n, tm_e)
    xs = _sc_gather_rows(xp, tok_of_row)
    if baton is not None and baton["lead"]:
        baton["box"].append(tok_of_row)
    yield
    ys = _experts(xs, tile_expert + layer * N_EXPERTS, n_used, w1, w3, w2, tm_e)
    y01 = _sc_gather_rows(ys, row_of_asg)
    yield
    return _moe_ln(x, y01, route[2:4].T, g, b, tm)


def _router_weights(w_group, b_group, w_expert, b_expert):
    wr = jnp.zeros((ROUTER_ROWS, D_MODEL), F32)
    wr = wr.at[0:N_GROUPS].set(w_group.T).at[8:8 + N_EXPERTS].set(w_expert.T)
    br = jnp.zeros((ROUTER_ROWS, LANES), F32)
    br = br.at[0:N_GROUPS, :].set(b_group[:, None]).at[8:8 + N_EXPERTS, :].set(b_expert[:, None])
    return wr.astype(BF16), br


def _trunk_steps(x3, mem3, p, baton=None):
    batch, seq, _ = x3.shape
    n = batch * seq
    x = x3.reshape(n, D_MODEL)
    mem = mem3.reshape(batch * mem3.shape[1], D_MODEL)
    tm = 512
    tables = _rope_tables(seq)
    for layer in range(DEPTH):
        j = layer // 2
        row = lambda a: a.reshape(1, D_MODEL)
        if baton is not None and not baton["lead"] and baton["box"]:
            x, _ = lax.optimization_barrier((x, baton["box"].pop()))
        if layer % 2 == 0:
            *views, gu = _even_proj(x, p["ev_w_in"][j], tables, seq, tm)
            res = [_band_attention(v, batch, seq, dil) for v, (_, dil) in zip(views, DILATED_PATTERNS)]
            x = _even_out([o for o, _ in res], [l for _, l in res], gu, x, p["ev_w_out"][j],
                          p["ev_conv_w"][j], row(p["ln_g"][layer, 0]), row(p["ln_b"][layer, 0]), seq, tm)
        else:
            proj = _matmul(x, p["od_w_in"][j], BF16, tm, D_MODEL)
            o = _gla(proj, p["lb_logits"], p["od_norm_g"][j].reshape(1, D_MODEL), batch, seq)
            x = _proj_ln(o, p["od_w_out"][j], x, row(p["ln_g"][layer, 0]), row(p["ln_b"][layer, 0]), tm)
        kv = _matmul(mem, p["xa_w_kv"][layer], BF16, 256, D_MODEL).reshape(batch, mem3.shape[1], 2 * D_MODEL)
        wr, br = _router_weights(p["moe_w_group"][layer], p["moe_b_group"][layer],
                                 p["moe_w_expert"][layer], p["moe_b_expert"][layer])
        x, xp, route = _xattn(x, kv, p["xa_w_q"][layer], p["xa_w_out"][layer],
                              row(p["ln_g"][layer, 1]), row(p["ln_b"][layer, 1]), wr, br, seq, tm)
        x = yield from _moe_steps(x, xp, route, layer, p["moe_w1"], p["moe_w3"], p["moe_w2"],
                                  row(p["ln_g"][layer, 2]), row(p["ln_b"][layer, 2]), 512, tm, baton)
    return x.reshape(batch, seq, D_MODEL)


def _run_interleaved(generators):
    results = [None] * len(generators)
    live = list(range(len(generators)))
    while live:
        for k in list(live):
            try:
                next(generators[k])
            except StopIteration as stop:
                results[k] = stop.value
                live.remove(k)
    return results


def _trunk(x3, mem3, p):
    return _run_interleaved([_trunk_steps(x3, mem3, p)])[0]


def kernel(x_prompt, x_sample, mem_prompt, mem_sample, ev_w_in, ev_conv_w, ev_w_out, od_w_in, lb_logits,
           od_norm_g, od_w_out, xa_w_q, xa_w_kv, xa_w_out, moe_w_group, moe_b_group, moe_w_expert,
           moe_b_expert, moe_w1, moe_w3, moe_w2, ln_g, ln_b):
    ff = moe_w1.shape[-1]
    p = dict(
        ev_w_in=ev_w_in.astype(BF16), ev_conv_w=ev_conv_w, ev_w_out=ev_w_out.astype(BF16),
        od_w_in=od_w_in.astype(BF16), lb_logits=lb_logits, od_norm_g=od_norm_g,
        od_w_out=od_w_out.astype(BF16), xa_w_q=xa_w_q.astype(BF16), xa_w_kv=xa_w_kv.astype(BF16),
        xa_w_out=xa_w_out.astype(BF16), moe_w_group=moe_w_group, moe_b_group=moe_b_group,
        moe_w_expert=moe_w_expert, moe_b_expert=moe_b_expert,
        moe_w1=moe_w1.reshape(DEPTH * N_EXPERTS, D_MODEL, ff),
        moe_w3=moe_w3.reshape(DEPTH * N_EXPERTS, D_MODEL, ff),
        moe_w2=moe_w2.reshape(DEPTH * N_EXPERTS, ff, D_MODEL),
        ln_g=ln_g, ln_b=ln_b)
    box = []
    y_prompt, y_sample = _run_interleaved([
        _trunk_steps(x_prompt, mem_prompt, p, dict(lead=True, box=box)),
        _trunk_steps(x_sample, mem_sample, p, dict(lead=False, box=box))])
    return y_prompt, y_sample
```

```python
import functools
import math

import jax
import jax.numpy as jnp
from jax import lax
from jax.experimental import pallas as pl
from jax.experimental.pallas import tpu as pltpu
from jax.experimental.pallas import tpu_sc as plsc

F32 = jnp.float32
BF16 = jnp.bfloat16

D_MODEL = 1024
DEPTH = 2
A_HEADS = 8
A_HEAD_DIM = 64
A_WIDTH = A_HEADS * A_HEAD_DIM
DILATED_PATTERNS = ((128, 1), (512, 4), (2048, 16))
ROPE_THETA = 500000.0
ROPE_DIM = A_HEAD_DIM // 4
B_WIDTH = D_MODEL // 2
CONV_WIDTH = 3
HG_HEADS = 8
HG_DIM = D_MODEL // HG_HEADS
XA_HEADS = 4
XA_HEAD_DIM = D_MODEL // XA_HEADS
N_GROUPS = 4
EXPERTS_PER_GROUP = 8
N_EXPERTS = N_GROUPS * EXPERTS_PER_GROUP
EXPERT_FF = D_MODEL // 4
LN_EPS = 1e-5
RMS_EPS = 1e-6
DEEPNORM_ALPHA = (2 * DEPTH) ** 0.25

LANES = 128
BAND_RADIUS = 64
ATTN_QBLOCK = 128
NEG_BIG = -1e30
GLA_CHUNK = 64
GLA_SUB = 16
GLA_EXP2_CLAMP = 100.0
GLA_INTRA_UNROLL = 2
LOG2_E = 1.4426950408889634
VMEM_LIMIT = 56 * 1024 * 1024


def _cost(flops, nbytes, transcendentals=0):
    return pl.CostEstimate(flops=int(flops), transcendentals=int(transcendentals), bytes_accessed=int(nbytes))


def _cparams(*sem):
    return pltpu.CompilerParams(dimension_semantics=sem, vmem_limit_bytes=VMEM_LIMIT)


def _layer_norm(y, g, b):
    mu = jnp.mean(y, axis=-1, keepdims=True)
    d = y - mu
    var = jnp.mean(d * d, axis=-1, keepdims=True)
    return d * lax.rsqrt(var + LN_EPS) * g + b


def _sigmoid(z):
    return 1.0 / (1.0 + jnp.exp(-z))


def _pack_bf16_pair(y):
    w = y.shape[1] // 2
    hi = lax.bitcast_convert_type(y[:, :w].astype(BF16).astype(F32), jnp.int32)
    lo = lax.bitcast_convert_type(y[:, w:].astype(BF16).astype(F32), jnp.int32)
    return hi | lax.shift_right_logical(lo, 16)


def _unpack_bf16_pair(p):
    hi = lax.bitcast_convert_type(p & jnp.int32(-65536), F32)
    lo = lax.bitcast_convert_type(lax.shift_left(p, 16), F32)
    return hi, lo


def _mm_kernel(x_ref, w_ref, o_ref, *, chunk):
    xb = x_ref[...].astype(BF16)
    for c in range(w_ref.shape[1] // chunk):
        cols = slice(c * chunk, (c + 1) * chunk)
        o_ref[:, cols] = jnp.dot(xb, w_ref[:, cols], preferred_element_type=F32).astype(o_ref.dtype)


def _matmul(x, w, out_dtype, tm, chunk):
    n, k = x.shape
    m = w.shape[1]
    return pl.pallas_call(
        functools.partial(_mm_kernel, chunk=chunk),
        grid=(n // tm,),
        in_specs=[pl.BlockSpec((tm, k), lambda i: (i, 0)), pl.BlockSpec((k, m), lambda i: (0, 0))],
        out_specs=pl.BlockSpec((tm, m), lambda i: (i, 0)),
        out_shape=jax.ShapeDtypeStruct((n, m), out_dtype),
        compiler_params=_cparams("parallel"),
        cost_estimate=_cost(2 * n * k * m, n * k * x.dtype.itemsize + 2 * k * m + n * m * 2),
        name="matmul",
    )(x, w)


def _rope_tables(seq):
    half = ROPE_DIM // 2
    inv_freq = jnp.exp(-math.log(ROPE_THETA) * jnp.arange(half, dtype=F32) * (2.0 / ROPE_DIM))
    ang = jnp.arange(seq, dtype=F32)[:, None] * inv_freq[None, :]
    cos, sin = jnp.cos(ang), jnp.sin(ang)
    ones = jnp.ones((seq, A_HEAD_DIM - ROPE_DIM), F32)
    zeros = jnp.zeros((seq, A_HEAD_DIM - ROPE_DIM), F32)
    zh = jnp.zeros((seq, half), F32)
    c = jnp.concatenate([cos, cos, ones], -1)
    s_up = jnp.concatenate([-sin, zh, zeros], -1)
    s_dn = jnp.concatenate([zh, sin, zeros], -1)
    rep = LANES // A_HEAD_DIM
    return tuple(jnp.tile(t, (1, rep)) for t in (c, s_up, s_dn))


def _even_proj_kernel(x_ref, w_ref, c_ref, su_ref, sd_ref, qkv_ref, qkv4_ref, qkv16_ref, gu_ref, slab_ref):
    tm = x_ref.shape[0]
    xb = x_ref[...].astype(BF16)
    rep = A_WIDTH // LANES
    half = ROPE_DIM // 2
    c = jnp.tile(c_ref[...], (1, rep))
    su = jnp.tile(su_ref[...], (1, rep))
    sd = jnp.tile(sd_ref[...], (1, rep))

    def proj(j):
        return jnp.dot(xb, w_ref[:, j * A_WIDTH:(j + 1) * A_WIDTH], preferred_element_type=F32)

    def rope(t):
        up = pltpu.roll(t, A_WIDTH - half, axis=1)
        dn = pltpu.roll(t, half, axis=1)
        return t * c + up * su + dn * sd

    qkv = (rope(proj(0)) * (A_HEAD_DIM ** -0.5), rope(proj(1)), proj(2))
    per = A_WIDTH // LANES
    for j, part in enumerate(qkv):
        qkv_ref[:, j * A_WIDTH:(j + 1) * A_WIDTH] = part.astype(BF16)
        for s in range(per):
            slab_ref[j * per + s] = part[:, s * LANES:(s + 1) * LANES]
    for dil, out_ref in ((DILATED_PATTERNS[1][1], qkv4_ref), (DILATED_PATTERNS[2][1], qkv16_ref)):
        for r in range(dil):
            for s in range(3 * per):
                val = slab_ref[s, pl.ds(r, tm // dil, stride=dil), :]
                col = r * 3 * A_WIDTH + s * LANES
                out_ref[:, col:col + LANES] = val.astype(BF16)
    gu_ref[:, 0:B_WIDTH] = proj(3).astype(BF16)
    gu_ref[:, B_WIDTH:2 * B_WIDTH] = (proj(4) * proj(5)).astype(BF16)


def _even_proj(x, w_in, tables, seq, tm):
    n = x.shape[0]
    spt = seq // tm
    tab_spec = pl.BlockSpec((tm, LANES), lambda i: (i % spt, 0))
    d4, d16 = DILATED_PATTERNS[1][1], DILATED_PATTERNS[2][1]
    width = 3 * A_WIDTH
    return pl.pallas_call(
        _even_proj_kernel,
        grid=(n // tm,),
        in_specs=[pl.BlockSpec((tm, D_MODEL), lambda i: (i, 0)),
                  pl.BlockSpec(w_in.shape, lambda i: (0, 0)),
                  tab_spec, tab_spec, tab_spec],
        out_specs=[pl.BlockSpec((tm, width), lambda i: (i, 0)),
                   pl.BlockSpec((tm // d4, d4 * width), lambda i: (i, 0)),
                   pl.BlockSpec((tm // d16, d16 * width), lambda i: (i, 0)),
                   pl.BlockSpec((tm, 2 * B_WIDTH), lambda i: (i, 0))],
        out_shape=[jax.ShapeDtypeStruct((n, width), BF16),
                   jax.ShapeDtypeStruct((n // d4, d4 * width), BF16),
                   jax.ShapeDtypeStruct((n // d16, d16 * width), BF16),
                   jax.ShapeDtypeStruct((n, 2 * B_WIDTH), BF16)],
        scratch_shapes=[pltpu.VMEM((width // LANES, tm, LANES), F32)],
        compiler_params=_cparams("parallel"),
        cost_estimate=_cost(2 * n * D_MODEL * 6 * A_WIDTH, n * (4 * D_MODEL + 2 * (9 * A_WIDTH + 2 * B_WIDTH))),
        name="even_proj",
    )(x, w_in, *tables)


def _band_attn_kernel(q_ref, kp_ref, km_ref, kn_ref, vp_ref, vm_ref, vn_ref, o_ref, lse_ref,
                      kbuf, vbuf, *, tq, length):
    i = pl.program_id(2)
    r = BAND_RADIUS
    kbuf[0:r] = kp_ref[0]
    kbuf[r:r + tq] = km_ref[0]
    kbuf[r + tq:r + tq + r] = kn_ref[0]
    vbuf[0:r] = vp_ref[0]
    vbuf[r:r + tq] = vm_ref[0]
    vbuf[r + tq:r + tq + r] = vn_ref[0]

    qb = ATTN_QBLOCK
    kw = qb + 2 * r
    qi = lax.broadcasted_iota(jnp.int32, (qb, kw), 0)
    kj = lax.broadcasted_iota(jnp.int32, (qb, kw), 1)
    rel = kj - qi
    band = (rel >= 0) & (rel <= 2 * r)
    lane = lax.broadcasted_iota(jnp.int32, (qb, LANES), 1)
    low = lane < A_HEAD_DIM
    nt = (((1,), (1,)), ((), ()))

    for s in range(tq // qb):
        kpos = i * tq + (s * qb - r) + kj
        valid = band & (kpos >= 0) & (kpos < length)
        bias = jnp.where(valid, 0.0, NEG_BIG)
        rows = slice(s * qb, (s + 1) * qb)
        wrows = slice(s * qb, s * qb + kw)
        for p in range(A_WIDTH // LANES):
            cols = slice(p * LANES, (p + 1) * LANES)
            qp = q_ref[0, rows, cols]
            kwin = kbuf[wrows, cols]
            vwin = vbuf[wrows, cols]
            outs, lses = [], []
            for sel in (low, jnp.logical_not(low)):
                qm = jnp.where(sel, qp, jnp.zeros_like(qp))
                sc = lax.dot_general(qm, kwin, nt, preferred_element_type=F32) + bias
                m = jnp.max(sc, axis=-1, keepdims=True)
                pe = jnp.exp(sc - m)
                l = jnp.sum(pe, axis=-1, keepdims=True)
                pv = jnp.dot(pe.astype(BF16), vwin, preferred_element_type=F32)
                outs.append(pv / l)
                lses.append(jnp.broadcast_to(m + jnp.log(l), (qb, LANES)))
            o_ref[0, rows, cols] = jnp.where(low, outs[0], outs[1]).astype(o_ref.dtype)
            lse_ref[0, rows, cols] = jnp.where(low, lses[0], lses[1])


def _band_attention(qkv_view, batch, seq, dil):
    length = seq // dil
    tq = min(512, length)
    r = BAND_RADIUS
    view = qkv_view.reshape(batch, length, dil * 3 * A_WIDTH)
    nblk_h = length // r
    per = tq // r

    def main(j):
        return pl.BlockSpec((1, tq, A_WIDTH), lambda b, rr, i: (b, i, rr * 3 + j))

    def prev(j):
        return pl.BlockSpec((1, r, A_WIDTH), lambda b, rr, i: (b, jnp.maximum(i * per - 1, 0), rr * 3 + j))

    def nxt(j):
        return pl.BlockSpec((1, r, A_WIDTH),
                            lambda b, rr, i: (b, jnp.minimum((i + 1) * per, nblk_h - 1), rr * 3 + j))

    out_spec = pl.BlockSpec((1, tq, A_WIDTH), lambda b, rr, i: (b, i, rr))
    o, lse = pl.pallas_call(
        functools.partial(_band_attn_kernel, tq=tq, length=length),
        grid=(batch, dil, length // tq),
        in_specs=[main(0), prev(1), main(1), nxt(1), prev(2), main(2), nxt(2)],
        out_specs=[out_spec, out_spec],
        out_shape=[jax.ShapeDtypeStruct((batch, length, dil * A_WIDTH), BF16),
                   jax.ShapeDtypeStruct((batch, length, dil * A_WIDTH), F32)],
        scratch_shapes=[pltpu.VMEM((tq + 2 * r, A_WIDTH), BF16), pltpu.VMEM((tq + 2 * r, A_WIDTH), BF16)],
        compiler_params=_cparams("parallel", "parallel", "parallel"),
        cost_estimate=_cost(batch * seq * A_HEADS * 8 * (ATTN_QBLOCK + 2 * BAND_RADIUS) * LANES,
                            batch * seq * A_WIDTH * (2 * 4 + 2 + 4),
                            batch * seq * A_HEADS * (ATTN_QBLOCK + 2 * BAND_RADIUS)),
        name=f"band_attn_d{dil}",
    )(view, view, view, view, view, view, view)
    return o.reshape(batch * length, dil * A_WIDTH), lse.reshape(batch * length, dil * A_WIDTH)


def _even_out_kernel(o1, o4, o16, l1, l4, l16, gu_ref, up_ref, un_ref, x_ref, w_ref, cw_ref, g_ref, b_ref,
                     y_ref, so4, sl4, so16, sl16, *, tm, seq):
    i = pl.program_id(0)
    pos = (i * tm) % seq
    per = A_WIDTH // LANES
    for dil, o_ref, l_ref, so, sl in ((DILATED_PATTERNS[1][1], o4, l4, so4, sl4),
                                      (DILATED_PATTERNS[2][1], o16, l16, so16, sl16)):
        for r in range(dil):
            for s in range(per):
                cols = slice(r * A_WIDTH + s * LANES, r * A_WIDTH + (s + 1) * LANES)
                so[s, pl.ds(r, tm // dil, stride=dil), :] = o_ref[:, cols].astype(F32)
                sl[s, pl.ds(r, tm // dil, stride=dil), :] = l_ref[:, cols]
    slabs = []
    for s in range(per):
        cols = slice(s * LANES, (s + 1) * LANES)
        la, lb, lc = l1[:, cols], sl4[s], sl16[s]
        mx = jnp.maximum(jnp.maximum(la, lb), lc)
        ea, eb, ec = jnp.exp(la - mx), jnp.exp(lb - mx), jnp.exp(lc - mx)
        num = ea * o1[:, cols].astype(F32) + eb * so4[s] + ec * so16[s]
        slabs.append(num / (ea + eb + ec))
    a_out = jnp.concatenate(slabs, axis=1)

    gate_b = gu_ref[:, 0:B_WIDTH].astype(F32)
    u = gu_ref[:, B_WIDTH:2 * B_WIDTH].astype(F32)
    hrows = up_ref.shape[0]
    u_before = jnp.where(pos > 0, up_ref[hrows - 1:hrows, :].astype(F32), 0.0)
    u_after = jnp.where(pos + tm < seq, un_ref[0:1, :].astype(F32), 0.0)
    row = lax.broadcasted_iota(jnp.int32, (tm, B_WIDTH), 0)
    u_prev = jnp.where(row == 0, u_before, pltpu.roll(u, 1, axis=0))
    u_next = jnp.where(row == tm - 1, u_after, pltpu.roll(u, tm - 1, axis=0))
    conv = u_prev * cw_ref[0:1, :] + u * cw_ref[1:2, :] + u_next * cw_ref[2:3, :]
    b_out = gate_b * conv

    mix = jnp.dot(a_out.astype(BF16), w_ref[0:A_WIDTH, :], preferred_element_type=F32)
    mix = mix + jnp.dot(b_out.astype(BF16), w_ref[A_WIDTH:A_WIDTH + B_WIDTH, :], preferred_element_type=F32)
    y_ref[...] = _layer_norm(DEEPNORM_ALPHA * x_ref[...] + mix, g_ref[...], b_ref[...])


def _even_out(outs, lses, gu, x, w_out, conv_w, g, b, seq, tm):
    n = x.shape[0]
    hrows = 16
    nh = n // hrows
    per = tm // hrows
    d4, d16 = DILATED_PATTERNS[1][1], DILATED_PATTERNS[2][1]
    views = [pl.BlockSpec((tm // d, d * A_WIDTH), lambda i: (i, 0)) for d in (1, d4, d16)]
    full = lambda a: pl.BlockSpec(a.shape, lambda i: (0, 0))
    slab = pltpu.VMEM((A_WIDTH // LANES, tm, LANES), F32)
    return pl.pallas_call(
        functools.partial(_even_out_kernel, tm=tm, seq=seq),
        grid=(n // tm,),
        scratch_shapes=[slab, slab, slab, slab],
        in_specs=views + views + [
            pl.BlockSpec((tm, 2 * B_WIDTH), lambda i: (i, 0)),
            pl.BlockSpec((hrows, B_WIDTH), lambda i: (jnp.maximum(i * per - 1, 0), 1)),
            pl.BlockSpec((hrows, B_WIDTH), lambda i: (jnp.minimum((i + 1) * per, nh - 1), 1)),
            pl.BlockSpec((tm, D_MODEL), lambda i: (i, 0)),
            full(w_out), full(conv_w), full(g), full(b)],
        out_specs=pl.BlockSpec((tm, D_MODEL), lambda i: (i, 0)),
        out_shape=jax.ShapeDtypeStruct((n, D_MODEL), F32),
        compiler_params=_cparams("parallel"),
        cost_estimate=_cost(2 * n * D_MODEL * D_MODEL, n * (A_WIDTH * 18 + 4 * B_WIDTH + 8 * D_MODEL)),
        name="even_out",
    )(*outs, *lses, gu, gu, gu, x, w_out, conv_w, g, b)


def _gla_gates(z, lb, tri2):
    f = lb + (1.0 - lb) * _sigmoid(z)
    lf2 = jnp.log(f) * LOG2_E
    hi = lf2.astype(BF16)
    lo = (lf2 - hi.astype(F32)).astype(BF16)
    cum = jnp.dot(tri2, jnp.concatenate([hi, lo], axis=0), preferred_element_type=F32)
    return 1.0 - f, cum


def _gla_scores(q, kk, cum, *, reverse):
    c = GLA_CHUNK
    sb = GLA_SUB
    dk = q.shape[1]
    nt = (((1,), (1,)), ((), ()))
    ti = lax.broadcasted_iota(jnp.int32, (c, c), 0)
    si = lax.broadcasted_iota(jnp.int32, (c, c), 1)
    causal = (si >= ti) if reverse else (ti >= si)

    parts = []
    for blk in range(c // sb):
        rows = slice(blk * sb, (blk + 1) * sb)
        if reverse:
            edge = (blk + 1) * sb
            ref = cum[edge:edge + 1, :] if edge < c else jnp.zeros((1, dk), F32)
            other = slice((blk + 1) * sb, c)
        else:
            edge = blk * sb - 1
            ref = cum[edge:edge + 1, :] if edge >= 0 else jnp.zeros((1, dk), F32)
            other = slice(0, blk * sb)
        qs = q[rows] * jnp.exp2(cum[rows] - ref)
        k_own = (kk[rows] * jnp.exp2(jnp.minimum(ref - cum[rows], GLA_EXP2_CLAMP))).astype(BF16)
        pieces = [k_own]
        n_other = other.stop - other.start
        if n_other:
            k_other = (kk[other] * jnp.exp2(ref - cum[other])).astype(BF16)
            pieces = [k_own, k_other] if reverse else [k_other, k_own]
        if n_other + sb < c:
            pad = jnp.zeros((c - n_other - sb, dk), BF16)
            pieces = [pad] + pieces if reverse else pieces + [pad]
        ks = jnp.concatenate(pieces, axis=0) if len(pieces) > 1 else pieces[0]
        parts.append(lax.dot_general(qs.astype(BF16), ks, nt, preferred_element_type=F32))
    return jnp.where(causal, jnp.concatenate(parts, axis=0), 0.0).astype(BF16)


def _gla_state_terms(q, kk, cum, *, reverse):
    c = GLA_CHUNK
    total = cum[0:1, :] if reverse else cum[c - 1:c, :]
    qe = (q * jnp.exp2(cum)).astype(BF16)
    kd = (kk * jnp.exp2(total - cum)).astype(BF16)
    return qe, kd, jnp.exp2(total)


def _gla_kernel(q_ref, zf_ref, zb_ref, v_ref, g_ref, lbl_ref, ng_ref, o_ref,
                acc_f, acc_b, qe_f, qe_b, kd_f, kd_b, et_f, et_b, att_f, att_b, ring_a, ring_b, vt_s,
                *, seq):
    c = GLA_CHUNK
    nc = seq // c
    dk = HG_DIM
    nt = (((1,), (1,)), ((), ()))
    l0 = lbl_ref[0:1, :]
    l1 = lbl_ref[1:2, :]
    mx = jnp.maximum(l0, l1)
    e0, e1 = jnp.exp(l0 - mx), jnp.exp(l1 - mx)
    lb = e0 / (e0 + e1)

    ti = lax.broadcasted_iota(jnp.int32, (c, 2 * c), 0)
    si = lax.broadcasted_iota(jnp.int32, (c, 2 * c), 1) & (c - 1)
    tri_f = jnp.where(ti >= si, 1.0, 0.0).astype(BF16)
    tri_b = jnp.where(si >= ti, 1.0, 0.0).astype(BF16)

    dirs = ((zf_ref, tri_f, att_f, qe_f, kd_f, et_f, False),
            (zb_ref, tri_b, att_b, qe_b, kd_b, et_b, True))
    grp = GLA_INTRA_UNROLL
    per_step = 2 * grp
    n_steps = nc // per_step
    chains = [(u, d) for u in range(grp) for d in range(2)]

    def chunk_of(step, half, u, d):
        ci = step * per_step + half * grp + u
        return (nc - 1 - ci) if d else ci

    def park_gates(step, half, ring):
        for k, (u, d) in enumerate(chains):
            rows = pl.ds(pl.multiple_of(chunk_of(step, half, u, d) * c, c), c)
            kk, cum = _gla_gates(dirs[d][0][0, rows, :].astype(F32), lb, dirs[d][1])
            ring[0, k * c:(k + 1) * c, :] = kk
            ring[1, k * c:(k + 1) * c, :] = cum

    def scores_from_ring(step, half, ring):
        for k, (u, d) in enumerate(chains):
            ci = chunk_of(step, half, u, d)
            rows = pl.ds(pl.multiple_of(ci * c, c), c)
            q = q_ref[0, rows, :].astype(F32)
            kk = ring[0, k * c:(k + 1) * c, :]
            cum = ring[1, k * c:(k + 1) * c, :]
            dirs[d][2][rows, :] = _gla_scores(q, kk, cum, reverse=dirs[d][6])
            qe, kd, et = _gla_state_terms(q, kk, cum, reverse=dirs[d][6])
            dirs[d][3][rows, :] = qe
            dirs[d][4][rows, :] = kd
            dirs[d][5][pl.ds(ci, 1), :] = et
            vt_s[ci] = v_ref[0, rows, :].astype(F32).T.astype(BF16)

    def intra(step):
        scores_from_ring(step, 0, ring_a)
        scores_from_ring(step, 1, ring_b)
        nxt = jnp.minimum(step + 1, n_steps - 1)
        park_gates(nxt, 0, ring_a)
        park_gates(nxt, 1, ring_b)

    def scan(j, carry):
        sf, sr = carry
        cfs = [j * per_step + u for u in range(per_step)]
        crs = [nc - 1 - cf for cf in cfs]
        rfs = [pl.ds(pl.multiple_of(cf * c, c), c) for cf in cfs]
        rrs = [pl.ds(pl.multiple_of(cr * c, c), c) for cr in crs]
        upd_f = [jnp.dot(vt_s[cf], kd_f[r, :], preferred_element_type=F32) for cf, r in zip(cfs, rfs)]
        upd_r = [jnp.dot(vt_s[cr], kd_b[r, :], preferred_element_type=F32) for cr, r in zip(crs, rrs)]
        loc_f = [jnp.dot(att_f[r, :], v_ref[0, r, :], preferred_element_type=F32) for r in rfs]
        loc_r = [jnp.dot(att_b[r, :], v_ref[0, r, :], preferred_element_type=F32) for r in rrs]
        sfs, srs = [sf], [sr]
        for u in range(per_step):
            sfs.append(sfs[-1] * et_f[pl.ds(cfs[u], 1), :] + upd_f[u])
            srs.append(srs[-1] * et_b[pl.ds(crs[u], 1), :] + upd_r[u])
        for u in range(per_step):
            acc_f[rfs[u], :] = loc_f[u] + lax.dot_general(qe_f[rfs[u], :], sfs[u].astype(BF16), nt,
                                                          preferred_element_type=F32)
            acc_b[rrs[u], :] = loc_r[u] + lax.dot_general(qe_b[rrs[u], :], srs[u].astype(BF16), nt,
                                                          preferred_element_type=F32)
        return sfs[-1], srs[-1]

    park_gates(0, 0, ring_a)
    park_gates(0, 1, ring_b)
    intra(0)

    def step(j, carry):
        carry = scan(j - 1, carry)
        intra(j)
        return carry

    zero = jnp.zeros((dk, dk), F32)
    carry = lax.fori_loop(1, n_steps, step, (zero, zero))
    scan(n_steps - 1, carry)

    blk = 512
    ng = ng_ref[...]

    def fin(j, _):
        rows = pl.ds(pl.multiple_of(j * blk, blk), blk)
        o = acc_f[rows, :] + acc_b[rows, :]
        o = o * lax.rsqrt(jnp.mean(o * o, axis=-1, keepdims=True) + RMS_EPS) * ng
        g = g_ref[0, rows, :].astype(F32)
        o_ref[0, rows, :] = (o * (g * _sigmoid(g))).astype(o_ref.dtype)
        return 0

    lax.fori_loop(0, seq // blk, fin, 0)


def _gla(proj, lb_logits, norm_g, batch, seq):
    view = proj.reshape(batch, seq, 5 * D_MODEL)

    def col(seg):
        return pl.BlockSpec((1, seq, HG_DIM), lambda b, h: (b, 0, seg * HG_HEADS + h))

    return pl.pallas_call(
        functools.partial(_gla_kernel, seq=seq),
        grid=(batch, HG_HEADS),
        in_specs=[col(0), col(1), col(2), col(3), col(4),
                  pl.BlockSpec((DEPTH, HG_DIM), lambda b, h: (0, h)),
                  pl.BlockSpec((1, HG_DIM), lambda b, h: (0, h))],
        out_specs=pl.BlockSpec((1, seq, HG_DIM), lambda b, h: (b, 0, h)),
        out_shape=jax.ShapeDtypeStruct((batch, seq, D_MODEL), BF16),
        scratch_shapes=[pltpu.VMEM((seq, HG_DIM), F32), pltpu.VMEM((seq, HG_DIM), F32),
                        pltpu.VMEM((seq, HG_DIM), BF16), pltpu.VMEM((seq, HG_DIM), BF16),
                        pltpu.VMEM((seq, HG_DIM), BF16), pltpu.VMEM((seq, HG_DIM), BF16),
                        pltpu.VMEM((seq // GLA_CHUNK, HG_DIM), F32),
                        pltpu.VMEM((seq // GLA_CHUNK, HG_DIM), F32),
                        pltpu.VMEM((seq, GLA_CHUNK), BF16), pltpu.VMEM((seq, GLA_CHUNK), BF16),
                        pltpu.VMEM((2, 2 * GLA_INTRA_UNROLL * GLA_CHUNK, HG_DIM), F32),
                        pltpu.VMEM((2, 2 * GLA_INTRA_UNROLL * GLA_CHUNK, HG_DIM), F32),
                        pltpu.VMEM((seq // GLA_CHUNK, HG_DIM, GLA_CHUNK), BF16)],
        compiler_params=_cparams("parallel", "parallel"),
        cost_estimate=_cost(batch * seq * HG_HEADS * 2 * 2 * HG_DIM * (5 * GLA_CHUNK + 2 * HG_DIM) // 2,
                            batch * seq * D_MODEL * 12, batch * seq * D_MODEL * 16),
        name="gla",
    )(view, view, view, view, view, lb_logits, norm_g).reshape(batch * seq, D_MODEL)


def _proj_ln_kernel(a_ref, w_ref, x_ref, g_ref, b_ref, y_ref):
    mix = jnp.dot(a_ref[...], w_ref[...], preferred_element_type=F32)
    y_ref[...] = _layer_norm(DEEPNORM_ALPHA * x_ref[...] + mix, g_ref[...], b_ref[...])


def _proj_ln(a, w, x, g, b, tm):
    n = x.shape[0]
    full = lambda t: pl.BlockSpec(t.shape, lambda i: (0, 0))
    row = pl.BlockSpec((tm, D_MODEL), lambda i: (i, 0))
    return pl.pallas_call(
        _proj_ln_kernel,
        grid=(n // tm,),
        in_specs=[row, full(w), row, full(g), full(b)],
        out_specs=row,
        out_shape=jax.ShapeDtypeStruct((n, D_MODEL), F32),
        compiler_params=_cparams("parallel"),
        cost_estimate=_cost(2 * n * D_MODEL * D_MODEL, n * D_MODEL * 10),
        name="proj_ln",
    )(a, w, x, g, b)


ROUTER_ROWS = 64
ROUTE_OUT_ROWS = 8


def _first_index_of(vals, target, n_rows):
    idx = lax.broadcasted_iota(jnp.int32, vals.shape, 0)
    return jnp.min(jnp.where(vals == target, idx, n_rows), axis=0, keepdims=True)


def _xattn_kernel(x_ref, wq_ref, kv_ref, wo_ref, g_ref, b_ref, wr_ref, br_ref, y_ref, yb_ref, route_ref):
    x = x_ref[...]
    q = jnp.dot(x.astype(BF16), wq_ref[...], preferred_element_type=F32) * (XA_HEAD_DIM ** -0.5)
    qb = q.astype(BF16)
    nt = (((1,), (1,)), ((), ()))
    heads = []
    for h in range(XA_HEADS):
        cols = slice(h * XA_HEAD_DIM, (h + 1) * XA_HEAD_DIM)
        k = kv_ref[0, :, cols]
        v = kv_ref[0, :, D_MODEL + h * XA_HEAD_DIM:D_MODEL + (h + 1) * XA_HEAD_DIM]
        sc = lax.dot_general(qb[:, cols], k, nt, preferred_element_type=F32)
        m = jnp.max(sc, axis=-1, keepdims=True)
        pe = jnp.exp(sc - m)
        p = pe / jnp.sum(pe, axis=-1, keepdims=True)
        heads.append(jnp.dot(p.astype(BF16), v, preferred_element_type=F32).astype(BF16))
    o = jnp.concatenate(heads, axis=1)
    xa = jnp.dot(o, wo_ref[...], preferred_element_type=F32)
    y = _layer_norm(DEEPNORM_ALPHA * x + xa, g_ref[...], b_ref[...])
    y_ref[...] = y
    yb_ref[...] = _pack_bf16_pair(y)

    lg = lax.dot_general(wr_ref[...], y.astype(BF16), nt, preferred_element_type=F32) + br_ref[:, 0:1]
    gl = lg[0:N_GROUPS, :]
    gmax = jnp.max(gl, axis=0, keepdims=True)
    g_w = 1.0 / jnp.sum(jnp.exp(gl - gmax), axis=0, keepdims=True)
    g_sel = _first_index_of(gl, gmax, N_GROUPS)
    el = jnp.zeros((EXPERTS_PER_GROUP, gl.shape[1]), F32)
    for grp in range(N_GROUPS):
        rows = slice(8 + grp * EXPERTS_PER_GROUP, 8 + (grp + 1) * EXPERTS_PER_GROUP)
        el = el + jnp.where(g_sel == grp, lg[rows, :], 0.0)
    m1 = jnp.max(el, axis=0, keepdims=True)
    i1 = _first_index_of(el, m1, EXPERTS_PER_GROUP)
    eidx = lax.broadcasted_iota(jnp.int32, el.shape, 0)
    el2 = jnp.where(eidx == i1, -jnp.inf, el)
    m2 = jnp.max(el2, axis=0, keepdims=True)
    i2 = _first_index_of(el2, m2, EXPERTS_PER_GROUP)
    e2 = jnp.exp(m2 - m1)
    den = 1.0 + e2
    w1 = g_w / den
    w2 = g_w * e2 / den
    base = g_sel * EXPERTS_PER_GROUP
    zero = jnp.zeros_like(w1)
    route_ref[...] = jnp.concatenate(
        [(base + i1).astype(F32), (base + i2).astype(F32), w1, w2, zero, zero, zero, zero], axis=0)


def _xattn(x, kv, wq, wo, g, b, wr, br, seq, tm):
    n = x.shape[0]
    spt = seq // tm
    full = lambda t: pl.BlockSpec(t.shape, lambda i: (0, 0))
    row = pl.BlockSpec((tm, D_MODEL), lambda i: (i, 0))
    return pl.pallas_call(
        _xattn_kernel,
        grid=(n // tm,),
        in_specs=[row, full(wq),
                  pl.BlockSpec((1,) + kv.shape[1:], lambda i: (i // spt, 0, 0)),
                  full(wo), full(g), full(b), full(wr), full(br)],
        out_specs=[row, pl.BlockSpec((tm, D_MODEL // 2), lambda i: (i, 0)),
                   pl.BlockSpec((ROUTE_OUT_ROWS, tm), lambda i: (0, i))],
        out_shape=[jax.ShapeDtypeStruct((n, D_MODEL), F32),
                   jax.ShapeDtypeStruct((n, D_MODEL // 2), jnp.int32),
                   jax.ShapeDtypeStruct((ROUTE_OUT_ROWS, n), F32)],
        compiler_params=_cparams("parallel"),
        cost_estimate=_cost(n * (4 * D_MODEL * D_MODEL + 4 * D_MODEL * kv.shape[1] + 2 * ROUTER_ROWS * D_MODEL),
                            n * D_MODEL * 10, n * XA_HEADS * kv.shape[1]),
        name="xattn_router",
    )(x, wq, kv, wo, g, b, wr, br)


def _expert_kernel(te_ref, nu_ref, xs_ref, w1_ref, w3_ref, w2_ref, ys_ref):
    j = pl.program_id(0)

    @pl.when(j < nu_ref[0])
    def _():
        half = D_MODEL // 2
        x_hi, x_lo = _unpack_bf16_pair(xs_ref[...])
        x_hi, x_lo = x_hi.astype(BF16), x_lo.astype(BF16)

        def up(w_ref):
            return (jnp.dot(x_hi, w_ref[0, 0:half, :].astype(BF16), preferred_element_type=F32)
                    + jnp.dot(x_lo, w_ref[0, half:D_MODEL, :].astype(BF16), preferred_element_type=F32))

        h1 = up(w1_ref)
        hid = h1 * _sigmoid(h1) * up(w3_ref)
        ys_ref[...] = _pack_bf16_pair(
            jnp.dot(hid.astype(BF16), w2_ref[0].astype(BF16), preferred_element_type=F32))

    @pl.when(j >= nu_ref[0])
    def _():
        ys_ref[...] = jnp.zeros_like(ys_ref)


def _experts(xs, tile_expert, n_used, w1, w3, w2, tm):
    mp = xs.shape[0]
    w_in = pl.BlockSpec((1, D_MODEL, EXPERT_FF), lambda j, te, nu: (te[j], 0, 0))
    rows = pl.BlockSpec((tm, D_MODEL // 2), lambda j, te, nu: (j, 0))
    grid_spec = pltpu.PrefetchScalarGridSpec(
        num_scalar_prefetch=2,
        grid=(mp // tm,),
        in_specs=[rows, w_in, w_in,
                  pl.BlockSpec((1, EXPERT_FF, D_MODEL), lambda j, te, nu: (te[j], 0, 0))],
        out_specs=rows,
    )
    return pl.pallas_call(
        _expert_kernel,
        grid_spec=grid_spec,
        out_shape=jax.ShapeDtypeStruct((mp, D_MODEL // 2), jnp.int32),
        compiler_params=_cparams("arbitrary"),
        cost_estimate=_cost(6 * mp * D_MODEL * EXPERT_FF, mp * D_MODEL * 4 + 12 * N_EXPERTS * D_MODEL * EXPERT_FF,
                            mp * EXPERT_FF),
        name="experts",
    )(tile_expert, n_used, xs, w1, w3, w2)


def _moe_ln_kernel(x_ref, y0_ref, y1_ref, gate_ref, g_ref, b_ref, o_ref):
    g0, g1 = gate_ref[:, 0:1], gate_ref[:, 1:2]
    hi0, lo0 = _unpack_bf16_pair(y0_ref[...])
    hi1, lo1 = _unpack_bf16_pair(y1_ref[...])
    ff = jnp.concatenate([g0 * hi0 + g1 * hi1, g0 * lo0 + g1 * lo1], axis=1)
    o_ref[...] = _layer_norm(DEEPNORM_ALPHA * x_ref[...] + ff, g_ref[...], b_ref[...])


def _moe_ln(x, y01, gates, g, b, tm):
    n = x.shape[0]
    full = lambda t: pl.BlockSpec(t.shape, lambda i: (0, 0))
    row = pl.BlockSpec((tm, D_MODEL), lambda i: (i, 0))
    second = n // tm
    return pl.pallas_call(
        _moe_ln_kernel,
        grid=(n // tm,),
        in_specs=[row,
                  pl.BlockSpec((tm, D_MODEL // 2), lambda i: (i, 0)),
                  pl.BlockSpec((tm, D_MODEL // 2), lambda i: (i + second, 0)),
                  pl.BlockSpec((tm, 2), lambda i: (i, 0)), full(g), full(b)],
        out_specs=row,
        out_shape=jax.ShapeDtypeStruct((n, D_MODEL), F32),
        compiler_params=_cparams("parallel"),
        cost_estimate=_cost(0, n * D_MODEL * 12),
        name="moe_ln",
    )(x, y01, y01, gates, g, b)


SC_CORES = 2
SC_SUBCORES = 16
SC_GATHER_ROWS = 32


def _sc_gather_rows(table, idx):
    n_out = idx.shape[0]
    width = table.shape[1]
    workers = SC_CORES * SC_SUBCORES
    ch = SC_GATHER_ROWS
    per_w = n_out // workers
    steps = per_w // ch
    assert per_w * workers == n_out and steps * ch == per_w and steps % 2 == 0
    mesh = plsc.VectorSubcoreMesh(core_axis_name="c", subcore_axis_name="s")

    @functools.partial(
        pl.kernel, mesh=mesh,
        out_type=jax.ShapeDtypeStruct((n_out, width), table.dtype),
        cost_estimate=_cost(0, n_out * (2 * width * table.dtype.itemsize + 4)),
        scratch_types=[pltpu.VMEM((per_w,), jnp.int32),
                       pltpu.VMEM((ch, width), table.dtype), pltpu.VMEM((ch, width), table.dtype),
                       pltpu.SemaphoreType.DMA, pltpu.SemaphoreType.DMA,
                       pltpu.SemaphoreType.DMA, pltpu.SemaphoreType.DMA],
    )
    def gather_kernel(table_hbm, idx_hbm, out_hbm, idx_v, rows0, rows1, g0, g1, w0, w1):
        wid = lax.axis_index("s") * SC_CORES + lax.axis_index("c")
        base = wid * per_w
        pltpu.sync_copy(idx_hbm.at[pl.ds(base, per_w)], idx_v)
        bufs = ((rows0, g0, w0), (rows1, g1, w1))

        def gather(i, b):
            return pltpu.make_async_copy(table_hbm.at[idx_v.at[pl.ds(i * ch, ch)]], bufs[b][0], bufs[b][1])

        def write(i, b):
            return pltpu.make_async_copy(bufs[b][0], out_hbm.at[pl.ds(base + i * ch, ch)], bufs[b][2])

        gather(0, 0).start()

        @pl.loop(0, steps, step=2)
        def _(i):
            for b in range(2):
                ii = i + b
                gather(ii, b).wait()

                @pl.when(ii >= 1)
                def _():
                    write(ii - 1, 1 - b).wait()

                @pl.when(ii + 1 < steps)
                def _():
                    gather(ii + 1, 1 - b).start()

                write(ii, b).start()

        write(steps - 1, 1).wait()

    return gather_kernel(table, idx)


def _dispatch_plan(ids, n, tm_e):
    n_asg = 2 * n
    mp = n_asg + N_EXPERTS * tm_e
    order = jnp.argsort(ids, stable=True).astype(jnp.int32)
    pos = jnp.argsort(order).astype(jnp.int32)
    onehot = (ids[:, None] == jnp.arange(N_EXPERTS, dtype=jnp.int32)[None, :]).astype(jnp.int32)
    counts = jnp.sum(onehot, axis=0)
    dense_start = jnp.cumsum(counts) - counts
    padded = ((counts + tm_e - 1) // tm_e) * tm_e
    row_end = jnp.cumsum(padded)
    row_start = row_end - padded
    row_of_asg = pos + jnp.sum(onehot * (row_start - dense_start)[None, :], axis=1)
    tile_start = jnp.arange(mp // tm_e, dtype=jnp.int32) * tm_e
    tile_expert = jnp.minimum(jnp.sum((tile_start[:, None] >= row_end[None, :]).astype(jnp.int32), axis=1),
                              N_EXPERTS - 1)
    shift = (dense_start - row_start)[tile_expert]
    src = (tile_start + shift)[:, None] + jnp.arange(tm_e, dtype=jnp.int32)[None, :]
    asg_of_row = order[jnp.clip(src.reshape(mp), 0, n_asg - 1)]
    tok_of_row = jnp.where(asg_of_row >= n, asg_of_row - n, asg_of_row)
    n_used = (row_end[-1] // tm_e).astype(jnp.int32).reshape(1)
    return tok_of_row, row_of_asg, tile_expert.astype(jnp.int32), n_used


def _moe_steps(x, xp, route, layer, w1, w3, w2, g, b, tm_e, tm, baton):
    n = x.shape[0]
    ids = route[0:2].astype(jnp.int32).reshape(2 * n)
    tok_of_row, row_of_asg, tile_expert, n_used = _dispatch_plan(ids, n, tm_e)
    xs = _sc_gather_rows(xp, tok_of_row)
    if baton is not None and baton["lead"]:
        baton["box"].append(tok_of_row)
    yield
    ys = _experts(xs, tile_expert + layer * N_EXPERTS, n_used, w1, w3, w2, tm_e)
    y01 = _sc_gather_rows(ys, row_of_asg)
    yield
    return _moe_ln(x, y01, route[2:4].T, g, b, tm)


def _router_weights(w_group, b_group, w_expert, b_expert):
    wr = jnp.zeros((ROUTER_ROWS, D_MODEL), F32)
    wr = wr.at[0:N_GROUPS].set(w_group.T).at[8:8 + N_EXPERTS].set(w_expert.T)
    br = jnp.zeros((ROUTER_ROWS, LANES), F32)
    br = br.at[0:N_GROUPS, :].set(b_group[:, None]).at[8:8 + N_EXPERTS, :].set(b_expert[:, None])
    return wr.astype(BF16), br


def _trunk_steps(x3, mem3, p, baton=None):
    batch, seq, _ = x3.shape
    n = batch * seq
    x = x3.reshape(n, D_MODEL)
    mem = mem3.reshape(batch * mem3.shape[1], D_MODEL)
    tm = 512
    tables = _rope_tables(seq)
    for layer in range(DEPTH):
        j = layer // 2
        row = lambda a: a.reshape(1, D_MODEL)
        if baton is not None and not baton["lead"] and baton["box"]:
            x, _ = lax.optimization_barrier((x, baton["box"].pop()))
        if layer % 2 == 0:
            *views, gu = _even_proj(x, p["ev_w_in"][j], tables, seq, tm)
            res = [_band_attention(v, batch, seq, dil) for v, (_, dil) in zip(views, DILATED_PATTERNS)]
            x = _even_out([o for o, _ in res], [l for _, l in res], gu, x, p["ev_w_out"][j],
                          p["ev_conv_w"][j], row(p["ln_g"][layer, 0]), row(p["ln_b"][layer, 0]), seq, tm)
        else:
            proj = _matmul(x, p["od_w_in"][j], BF16, tm, D_MODEL)
            o = _gla(proj, p["lb_logits"], p["od_norm_g"][j].reshape(1, D_MODEL), batch, seq)
            x = _proj_ln(o, p["od_w_out"][j], x, row(p["ln_g"][layer, 0]), row(p["ln_b"][layer, 0]), tm)
        kv = _matmul(mem, p["xa_w_kv"][layer], BF16, 256, D_MODEL).reshape(batch, mem3.shape[1], 2 * D_MODEL)
        wr, br = _router_weights(p["moe_w_group"][layer], p["moe_b_group"][layer],
                                 p["moe_w_expert"][layer], p["moe_b_expert"][layer])
        x, xp, route = _xattn(x, kv, p["xa_w_q"][layer], p["xa_w_out"][layer],
                              row(p["ln_g"][layer, 1]), row(p["ln_b"][layer, 1]), wr, br, seq, tm)
        x = yield from _moe_steps(x, xp, route, layer, p["moe_w1"], p["moe_w3"], p["moe_w2"],
                                  row(p["ln_g"][layer, 2]), row(p["ln_b"][layer, 2]), 512, tm, baton)
    return x.reshape(batch, seq, D_MODEL)


def _run_interleaved(generators):
    results = [None] * len(generators)
    live = list(range(len(generators)))
    while live:
        for k in list(live):
            try:
                next(generators[k])
            except StopIteration as stop:
                results[k] = stop.value
                live.remove(k)
    return results


def _trunk(x3, mem3, p):
    return _run_interleaved([_trunk_steps(x3, mem3, p)])[0]


def kernel(x_prompt, x_sample, mem_prompt, mem_sample, ev_w_in, ev_conv_w, ev_w_out, od_w_in, lb_logits,
           od_norm_g, od_w_out, xa_w_q, xa_w_kv, xa_w_out, moe_w_group, moe_b_group, moe_w_expert,
           moe_b_expert, moe_w1, moe_w3, moe_w2, ln_g, ln_b):
    ff = moe_w1.shape[-1]
    p = dict(
        ev_w_in=ev_w_in.astype(BF16), ev_conv_w=ev_conv_w, ev_w_out=ev_w_out.astype(BF16),
        od_w_in=od_w_in.astype(BF16), lb_logits=lb_logits, od_norm_g=od_norm_g,
        od_w_out=od_w_out.astype(BF16), xa_w_q=xa_w_q.astype(BF16), xa_w_kv=xa_w_kv.astype(BF16),
        xa_w_out=xa_w_out.astype(BF16), moe_w_group=moe_w_group, moe_b_group=moe_b_group,
        moe_w_expert=moe_w_expert, moe_b_expert=moe_b_expert,
        moe_w1=moe_w1.reshape(DEPTH * N_EXPERTS, D_MODEL, ff),
        moe_w3=moe_w3.reshape(DEPTH * N_EXPERTS, D_MODEL, ff),
        moe_w2=moe_w2.reshape(DEPTH * N_EXPERTS, ff, D_MODEL),
        ln_g=ln_g, ln_b=ln_b)
    box = []
    y_prompt, y_sample = _run_interleaved([
        _trunk_steps(x_prompt, mem_prompt, p, dict(lead=True, box=box)),
        _trunk_steps(x_sample, mem_sample, p, dict(lead=False, box=box))])
    return y_prompt, y_sample
```

```python
import functools
import math

import jax
import jax.numpy as jnp
from jax import lax
from jax.experimental import pallas as pl
from jax.experimental.pallas import tpu as pltpu
from jax.experimental.pallas import tpu_sc as plsc

F32 = jnp.float32
BF16 = jnp.bfloat16

D_MODEL = 1024
DEPTH = 2
A_HEADS = 8
A_HEAD_DIM = 64
A_WIDTH = A_HEADS * A_HEAD_DIM
DILATED_PATTERNS = ((128, 1), (512, 4), (2048, 16))
ROPE_THETA = 500000.0
ROPE_DIM = A_HEAD_DIM // 4
B_WIDTH = D_MODEL // 2
CONV_WIDTH = 3
HG_HEADS = 8
HG_DIM = D_MODEL // HG_HEADS
XA_HEADS = 4
XA_HEAD_DIM = D_MODEL // XA_HEADS
N_GROUPS = 4
EXPERTS_PER_GROUP = 8
N_EXPERTS = N_GROUPS * EXPERTS_PER_GROUP
EXPERT_FF = D_MODEL // 4
LN_EPS = 1e-5
RMS_EPS = 1e-6
DEEPNORM_ALPHA = (2 * DEPTH) ** 0.25

LANES = 128
BAND_RADIUS = 64
ATTN_QBLOCK = 128
NEG_BIG = -1e30
GLA_CHUNK = 64
GLA_SUB = 16
GLA_EXP2_CLAMP = 100.0
GLA_INTRA_UNROLL = 2
LOG2_E = 1.4426950408889634
VMEM_LIMIT = 56 * 1024 * 1024


def _cost(flops, nbytes, transcendentals=0):
    return pl.CostEstimate(flops=int(flops), transcendentals=int(transcendentals), bytes_accessed=int(nbytes))


def _cparams(*sem):
    return pltpu.CompilerParams(dimension_semantics=sem, vmem_limit_bytes=VMEM_LIMIT)


def _layer_norm(y, g, b):
    mu = jnp.mean(y, axis=-1, keepdims=True)
    d = y - mu
    var = jnp.mean(d * d, axis=-1, keepdims=True)
    return d * lax.rsqrt(var + LN_EPS) * g + b


def _sigmoid(z):
    return 1.0 / (1.0 + jnp.exp(-z))


def _pack_bf16_pair(y):
    w = y.shape[1] // 2
    hi = lax.bitcast_convert_type(y[:, :w].astype(BF16).astype(F32), jnp.int32)
    lo = lax.bitcast_convert_type(y[:, w:].astype(BF16).astype(F32), jnp.int32)
    return hi | lax.shift_right_logical(lo, 16)


def _unpack_bf16_pair(p):
    hi = lax.bitcast_convert_type(p & jnp.int32(-65536), F32)
    lo = lax.bitcast_convert_type(lax.shift_left(p, 16), F32)
    return hi, lo


def _mm_kernel(x_ref, w_ref, o_ref, *, chunk):
    xb = x_ref[...].astype(BF16)
    for c in range(w_ref.shape[1] // chunk):
        cols = slice(c * chunk, (c + 1) * chunk)
        o_ref[:, cols] = jnp.dot(xb, w_ref[:, cols], preferred_element_type=F32).astype(o_ref.dtype)


def _matmul(x, w, out_dtype, tm, chunk):
    n, k = x.shape
    m = w.shape[1]
    return pl.pallas_call(
        functools.partial(_mm_kernel, chunk=chunk),
        grid=(n // tm,),
        in_specs=[pl.BlockSpec((tm, k), lambda i: (i, 0)), pl.BlockSpec((k, m), lambda i: (0, 0))],
        out_specs=pl.BlockSpec((tm, m), lambda i: (i, 0)),
        out_shape=jax.ShapeDtypeStruct((n, m), out_dtype),
        compiler_params=_cparams("parallel"),
        cost_estimate=_cost(2 * n * k * m, n * k * x.dtype.itemsize + 2 * k * m + n * m * 2),
        name="matmul",
    )(x, w)


def _rope_tables(seq):
    half = ROPE_DIM // 2
    inv_freq = jnp.exp(-math.log(ROPE_THETA) * jnp.arange(half, dtype=F32) * (2.0 / ROPE_DIM))
    ang = jnp.arange(seq, dtype=F32)[:, None] * inv_freq[None, :]
    cos, sin = jnp.cos(ang), jnp.sin(ang)
    ones = jnp.ones((seq, A_HEAD_DIM - ROPE_DIM), F32)
    zeros = jnp.zeros((seq, A_HEAD_DIM - ROPE_DIM), F32)
    zh = jnp.zeros((seq, half), F32)
    c = jnp.concatenate([cos, cos, ones], -1)
    s_up = jnp.concatenate([-sin, zh, zeros], -1)
    s_dn = jnp.concatenate([zh, sin, zeros], -1)
    rep = LANES // A_HEAD_DIM
    return tuple(jnp.tile(t, (1, rep)) for t in (c, s_up, s_dn))


def _even_proj_kernel(x_ref, w_ref, c_ref, su_ref, sd_ref, qkv_ref, qkv4_ref, qkv16_ref, gu_ref, slab_ref):
    tm = x_ref.shape[0]
    xb = x_ref[...].astype(BF16)
    rep = A_WIDTH // LANES
    half = ROPE_DIM // 2
    c = jnp.tile(c_ref[...], (1, rep))
    su = jnp.tile(su_ref[...], (1, rep))
    sd = jnp.tile(sd_ref[...], (1, rep))

    def proj(j):
        return jnp.dot(xb, w_ref[:, j * A_WIDTH:(j + 1) * A_WIDTH], preferred_element_type=F32)

    def rope(t):
        up = pltpu.roll(t, A_WIDTH - half, axis=1)
        dn = pltpu.roll(t, half, axis=1)
        return t * c + up * su + dn * sd

    qkv = (rope(proj(0)) * (A_HEAD_DIM ** -0.5), rope(proj(1)), proj(2))
    per = A_WIDTH // LANES
    for j, part in enumerate(qkv):
        qkv_ref[:, j * A_WIDTH:(j + 1) * A_WIDTH] = part.astype(BF16)
        for s in range(per):
            slab_ref[j * per + s] = part[:, s * LANES:(s + 1) * LANES]
    for dil, out_ref in ((DILATED_PATTERNS[1][1], qkv4_ref), (DILATED_PATTERNS[2][1], qkv16_ref)):
        for r in range(dil):
            for s in range(3 * per):
                val = slab_ref[s, pl.ds(r, tm // dil, stride=dil), :]
                col = r * 3 * A_WIDTH + s * LANES
                out_ref[:, col:col + LANES] = val.astype(BF16)
    gu_ref[:, 0:B_WIDTH] = proj(3).astype(BF16)
    gu_ref[:, B_WIDTH:2 * B_WIDTH] = (proj(4) * proj(5)).astype(BF16)


def _even_proj(x, w_in, tables, seq, tm):
    n = x.shape[0]
    spt = seq // tm
    tab_spec = pl.BlockSpec((tm, LANES), lambda i: (i % spt, 0))
    d4, d16 = DILATED_PATTERNS[1][1], DILATED_PATTERNS[2][1]
    width = 3 * A_WIDTH
    return pl.pallas_call(
        _even_proj_kernel,
        grid=(n // tm,),
        in_specs=[pl.BlockSpec((tm, D_MODEL), lambda i: (i, 0)),
                  pl.BlockSpec(w_in.shape, lambda i: (0, 0)),
                  tab_spec, tab_spec, tab_spec],
        out_specs=[pl.BlockSpec((tm, width), lambda i: (i, 0)),
                   pl.BlockSpec((tm // d4, d4 * width), lambda i: (i, 0)),
                   pl.BlockSpec((tm // d16, d16 * width), lambda i: (i, 0)),
                   pl.BlockSpec((tm, 2 * B_WIDTH), lambda i: (i, 0))],
        out_shape=[jax.ShapeDtypeStruct((n, width), BF16),
                   jax.ShapeDtypeStruct((n // d4, d4 * width), BF16),
                   jax.ShapeDtypeStruct((n // d16, d16 * width), BF16),
                   jax.ShapeDtypeStruct((n, 2 * B_WIDTH), BF16)],
        scratch_shapes=[pltpu.VMEM((width // LANES, tm, LANES), F32)],
        compiler_params=_cparams("parallel"),
        cost_estimate=_cost(2 * n * D_MODEL * 6 * A_WIDTH, n * (4 * D_MODEL + 2 * (9 * A_WIDTH + 2 * B_WIDTH))),
        name="even_proj",
    )(x, w_in, *tables)


def _band_attn_kernel(q_ref, kp_ref, km_ref, kn_ref, vp_ref, vm_ref, vn_ref, o_ref, lse_ref,
                      kbuf, vbuf, *, tq, length):
    i = pl.program_id(2)
    r = BAND_RADIUS
    kbuf[0:r] = kp_ref[0]
    kbuf[r:r + tq] = km_ref[0]
    kbuf[r + tq:r + tq + r] = kn_ref[0]
    vbuf[0:r] = vp_ref[0]
    vbuf[r:r + tq] = vm_ref[0]
    vbuf[r + tq:r + tq + r] = vn_ref[0]

    qb = ATTN_QBLOCK
    kw = qb + 2 * r
    qi = lax.broadcasted_iota(jnp.int32, (qb, kw), 0)
    kj = lax.broadcasted_iota(jnp.int32, (qb, kw), 1)
    rel = kj - qi
    band = (rel >= 0) & (rel <= 2 * r)
    lane = lax.broadcasted_iota(jnp.int32, (qb, LANES), 1)
    low = lane < A_HEAD_DIM
    nt = (((1,), (1,)), ((), ()))

    for s in range(tq // qb):
        kpos = i * tq + (s * qb - r) + kj
        valid = band & (kpos >= 0) & (kpos < length)
        bias = jnp.where(valid, 0.0, NEG_BIG)
        rows = slice(s * qb, (s + 1) * qb)
        wrows = slice(s * qb, s * qb + kw)
        for p in range(A_WIDTH // LANES):
            cols = slice(p * LANES, (p + 1) * LANES)
            qp = q_ref[0, rows, cols]
            kwin = kbuf[wrows, cols]
            vwin = vbuf[wrows, cols]
            outs, lses = [], []
            for sel in (low, jnp.logical_not(low)):
                qm = jnp.where(sel, qp, jnp.zeros_like(qp))
                sc = lax.dot_general(qm, kwin, nt, preferred_element_type=F32) + bias
                m = jnp.max(sc, axis=-1, keepdims=True)
                pe = jnp.exp(sc - m)
                l = jnp.sum(pe, axis=-1, keepdims=True)
                pv = jnp.dot(pe.astype(BF16), vwin, preferred_element_type=F32)
                outs.append(pv / l)
                lses.append(jnp.broadcast_to(m + jnp.log(l), (qb, LANES)))
            o_ref[0, rows, cols] = jnp.where(low, outs[0], outs[1]).astype(o_ref.dtype)
            lse_ref[0, rows, cols] = jnp.where(low, lses[0], lses[1])


def _band_attention(qkv_view, batch, seq, dil):
    length = seq // dil
    tq = min(512, length)
    r = BAND_RADIUS
    view = qkv_view.reshape(batch, length, dil * 3 * A_WIDTH)
    nblk_h = length // r
    per = tq // r

    def main(j):
        return pl.BlockSpec((1, tq, A_WIDTH), lambda b, rr, i: (b, i, rr * 3 + j))

    def prev(j):
        return pl.BlockSpec((1, r, A_WIDTH), lambda b, rr, i: (b, jnp.maximum(i * per - 1, 0), rr * 3 + j))

    def nxt(j):
        return pl.BlockSpec((1, r, A_WIDTH),
                            lambda b, rr, i: (b, jnp.minimum((i + 1) * per, nblk_h - 1), rr * 3 + j))

    out_spec = pl.BlockSpec((1, tq, A_WIDTH), lambda b, rr, i: (b, i, rr))
    o, lse = pl.pallas_call(
        functools.partial(_band_attn_kernel, tq=tq, length=length),
        grid=(batch, dil, length // tq),
        in_specs=[main(0), prev(1), main(1), nxt(1), prev(2), main(2), nxt(2)],
        out_specs=[out_spec, out_spec],
        out_shape=[jax.ShapeDtypeStruct((batch, length, dil * A_WIDTH), BF16),
                   jax.ShapeDtypeStruct((batch, length, dil * A_WIDTH), F32)],
        scratch_shapes=[pltpu.VMEM((tq + 2 * r, A_WIDTH), BF16), pltpu.VMEM((tq + 2 * r, A_WIDTH), BF16)],
        compiler_params=_cparams("parallel", "parallel", "parallel"),
        cost_estimate=_cost(batch * seq * A_HEADS * 8 * (ATTN_QBLOCK + 2 * BAND_RADIUS) * LANES,
                            batch * seq * A_WIDTH * (2 * 4 + 2 + 4),
                            batch * seq * A_HEADS * (ATTN_QBLOCK + 2 * BAND_RADIUS)),
        name=f"band_attn_d{dil}",
    )(view, view, view, view, view, view, view)
    return o.reshape(batch * length, dil * A_WIDTH), lse.reshape(batch * length, dil * A_WIDTH)


def _even_out_kernel(o1, o4, o16, l1, l4, l16, gu_ref, up_ref, un_ref, x_ref, w_ref, cw_ref, g_ref, b_ref,
                     y_ref, so4, sl4, so16, sl16, *, tm, seq):
    i = pl.program_id(0)
    pos = (i * tm) % seq
    per = A_WIDTH // LANES
    for dil, o_ref, l_ref, so, sl in ((DILATED_PATTERNS[1][1], o4, l4, so4, sl4),
                                      (DILATED_PATTERNS[2][1], o16, l16, so16, sl16)):
        for r in range(dil):
            for s in range(per):
                cols = slice(r * A_WIDTH + s * LANES, r * A_WIDTH + (s + 1) * LANES)
                so[s, pl.ds(r, tm // dil, stride=dil), :] = o_ref[:, cols].astype(F32)
                sl[s, pl.ds(r, tm // dil, stride=dil), :] = l_ref[:, cols]
    slabs = []
    for s in range(per):
        cols = slice(s * LANES, (s + 1) * LANES)
        la, lb, lc = l1[:, cols], sl4[s], sl16[s]
        mx = jnp.maximum(jnp.maximum(la, lb), lc)
        ea, eb, ec = jnp.exp(la - mx), jnp.exp(lb - mx), jnp.exp(lc - mx)
        num = ea * o1[:, cols].astype(F32) + eb * so4[s] + ec * so16[s]
        slabs.append(num / (ea + eb + ec))
    a_out = jnp.concatenate(slabs, axis=1)

    gate_b = gu_ref[:, 0:B_WIDTH].astype(F32)
    u = gu_ref[:, B_WIDTH:2 * B_WIDTH].astype(F32)
    hrows = up_ref.shape[0]
    u_before = jnp.where(pos > 0, up_ref[hrows - 1:hrows, :].astype(F32), 0.0)
    u_after = jnp.where(pos + tm < seq, un_ref[0:1, :].astype(F32), 0.0)
    row = lax.broadcasted_iota(jnp.int32, (tm, B_WIDTH), 0)
    u_prev = jnp.where(row == 0, u_before, pltpu.roll(u, 1, axis=0))
    u_next = jnp.where(row == tm - 1, u_after, pltpu.roll(u, tm - 1, axis=0))
    conv = u_prev * cw_ref[0:1, :] + u * cw_ref[1:2, :] + u_next * cw_ref[2:3, :]
    b_out = gate_b * conv

    mix = jnp.dot(a_out.astype(BF16), w_ref[0:A_WIDTH, :], preferred_element_type=F32)
    mix = mix + jnp.dot(b_out.astype(BF16), w_ref[A_WIDTH:A_WIDTH + B_WIDTH, :], preferred_element_type=F32)
    y_ref[...] = _layer_norm(DEEPNORM_ALPHA * x_ref[...] + mix, g_ref[...], b_ref[...])


def _even_out(outs, lses, gu, x, w_out, conv_w, g, b, seq, tm):
    n = x.shape[0]
    hrows = 16
    nh = n // hrows
    per = tm // hrows
    d4, d16 = DILATED_PATTERNS[1][1], DILATED_PATTERNS[2][1]
    views = [pl.BlockSpec((tm // d, d * A_WIDTH), lambda i: (i, 0)) for d in (1, d4, d16)]
    full = lambda a: pl.BlockSpec(a.shape, lambda i: (0, 0))
    slab = pltpu.VMEM((A_WIDTH // LANES, tm, LANES), F32)
    return pl.pallas_call(
        functools.partial(_even_out_kernel, tm=tm, seq=seq),
        grid=(n // tm,),
        scratch_shapes=[slab, slab, slab, slab],
        in_specs=views + views + [
            pl.BlockSpec((tm, 2 * B_WIDTH), lambda i: (i, 0)),
            pl.BlockSpec((hrows, B_WIDTH), lambda i: (jnp.maximum(i * per - 1, 0), 1)),
            pl.BlockSpec((hrows, B_WIDTH), lambda i: (jnp.minimum((i + 1) * per, nh - 1), 1)),
            pl.BlockSpec((tm, D_MODEL), lambda i: (i, 0)),
            full(w_out), full(conv_w), full(g), full(b)],
        out_specs=pl.BlockSpec((tm, D_MODEL), lambda i: (i, 0)),
        out_shape=jax.ShapeDtypeStruct((n, D_MODEL), F32),
        compiler_params=_cparams("parallel"),
        cost_estimate=_cost(2 * n * D_MODEL * D_MODEL, n * (A_WIDTH * 18 + 4 * B_WIDTH + 8 * D_MODEL)),
        name="even_out",
    )(*outs, *lses, gu, gu, gu, x, w_out, conv_w, g, b)


def _gla_gates(z, lb, tri2):
    f = lb + (1.0 - lb) * _sigmoid(z)
    lf2 = jnp.log(f) * LOG2_E
    hi = lf2.astype(BF16)
    lo = (lf2 - hi.astype(F32)).astype(BF16)
    cum = jnp.dot(tri2, jnp.concatenate([hi, lo], axis=0), preferred_element_type=F32)
    return 1.0 - f, cum


def _gla_scores(q, kk, cum, *, reverse):
    c = GLA_CHUNK
    sb = GLA_SUB
    dk = q.shape[1]
    nt = (((1,), (1,)), ((), ()))
    ti = lax.broadcasted_iota(jnp.int32, (c, c), 0)
    si = lax.broadcasted_iota(jnp.int32, (c, c), 1)
    causal = (si >= ti) if reverse else (ti >= si)

    parts = []
    for blk in range(c // sb):
        rows = slice(blk * sb, (blk + 1) * sb)
        if reverse:
            edge = (blk + 1) * sb
            ref = cum[edge:edge + 1, :] if edge < c else jnp.zeros((1, dk), F32)
            other = slice((blk + 1) * sb, c)
        else:
            edge = blk * sb - 1
            ref = cum[edge:edge + 1, :] if edge >= 0 else jnp.zeros((1, dk), F32)
            other = slice(0, blk * sb)
        qs = q[rows] * jnp.exp2(cum[rows] - ref)
        k_own = (kk[rows] * jnp.exp2(jnp.minimum(ref - cum[rows], GLA_EXP2_CLAMP))).astype(BF16)
        pieces = [k_own]
        n_other = other.stop - other.start
        if n_other:
            k_other = (kk[other] * jnp.exp2(ref - cum[other])).astype(BF16)
            pieces = [k_own, k_other] if reverse else [k_other, k_own]
        if n_other + sb < c:
            pad = jnp.zeros((c - n_other - sb, dk), BF16)
            pieces = [pad] + pieces if reverse else pieces + [pad]
        ks = jnp.concatenate(pieces, axis=0) if len(pieces) > 1 else pieces[0]
        parts.append(lax.dot_general(qs.astype(BF16), ks, nt, preferred_element_type=F32))
    return jnp.where(causal, jnp.concatenate(parts, axis=0), 0.0).astype(BF16)


def _gla_state_terms(q, kk, cum, *, reverse):
    c = GLA_CHUNK
    total = cum[0:1, :] if reverse else cum[c - 1:c, :]
    qe = (q * jnp.exp2(cum)).astype(BF16)
    kd = (kk * jnp.exp2(total - cum)).astype(BF16)
    return qe, kd, jnp.exp2(total)


def _gla_kernel(q_ref, zf_ref, zb_ref, v_ref, g_ref, lbl_ref, ng_ref, o_ref,
                acc_f, acc_b, qe_f, qe_b, kd_f, kd_b, et_f, et_b, att_f, att_b, ring_a, ring_b, vt_s,
                *, seq):
    c = GLA_CHUNK
    nc = seq // c
    dk = HG_DIM
    nt = (((1,), (1,)), ((), ()))
    l0 = lbl_ref[0:1, :]
    l1 = lbl_ref[1:2, :]
    mx = jnp.maximum(l0, l1)
    e0, e1 = jnp.exp(l0 - mx), jnp.exp(l1 - mx)
    lb = e0 / (e0 + e1)

    ti = lax.broadcasted_iota(jnp.int32, (c, 2 * c), 0)
    si = lax.broadcasted_iota(jnp.int32, (c, 2 * c), 1) & (c - 1)
    tri_f = jnp.where(ti >= si, 1.0, 0.0).astype(BF16)
    tri_b = jnp.where(si >= ti, 1.0, 0.0).astype(BF16)

    dirs = ((zf_ref, tri_f, att_f, qe_f, kd_f, et_f, False),
            (zb_ref, tri_b, att_b, qe_b, kd_b, et_b, True))
    grp = GLA_INTRA_UNROLL
    per_step = 2 * grp
    n_steps = nc // per_step
    chains = [(u, d) for u in range(grp) for d in range(2)]

    def chunk_of(step, half, u, d):
        ci = step * per_step + half * grp + u
        return (nc - 1 - ci) if d else ci

    def park_gates(step, half, ring):
        for k, (u, d) in enumerate(chains):
            rows = pl.ds(pl.multiple_of(chunk_of(step, half, u, d) * c, c), c)
            kk, cum = _gla_gates(dirs[d][0][0, rows, :].astype(F32), lb, dirs[d][1])
            ring[0, k * c:(k + 1) * c, :] = kk
            ring[1, k * c:(k + 1) * c, :] = cum

    def scores_from_ring(step, half, ring):
        for k, (u, d) in enumerate(chains):
            ci = chunk_of(step, half, u, d)
            rows = pl.ds(pl.multiple_of(ci * c, c), c)
            q = q_ref[0, rows, :].astype(F32)
            kk = ring[0, k * c:(k + 1) * c, :]
            cum = ring[1, k * c:(k + 1) * c, :]
            dirs[d][2][rows, :] = _gla_scores(q, kk, cum, reverse=dirs[d][6])
            qe, kd, et = _gla_state_terms(q, kk, cum, reverse=dirs[d][6])
            dirs[d][3][rows, :] = qe
            dirs[d][4][rows, :] = kd
            dirs[d][5][pl.ds(ci, 1), :] = et
            vt_s[ci] = v_ref[0, rows, :].astype(F32).T.astype(BF16)

    def intra(step):
        scores_from_ring(step, 0, ring_a)
        scores_from_ring(step, 1, ring_b)
        nxt = jnp.minimum(step + 1, n_steps - 1)
        park_gates(nxt, 0, ring_a)
        park_gates(nxt, 1, ring_b)

    def scan(j, carry):
        sf, sr = carry
        cfs = [j * per_step + u for u in range(per_step)]
        crs = [nc - 1 - cf for cf in cfs]
        rfs = [pl.ds(pl.multiple_of(cf * c, c), c) for cf in cfs]
        rrs = [pl.ds(pl.multiple_of(cr * c, c), c) for cr in crs]
        upd_f = [jnp.dot(vt_s[cf], kd_f[r, :], preferred_element_type=F32) for cf, r in zip(cfs, rfs)]
        upd_r = [jnp.dot(vt_s[cr], kd_b[r, :], preferred_element_type=F32) for cr, r in zip(crs, rrs)]
        loc_f = [jnp.dot(att_f[r, :], v_ref[0, r, :], preferred_element_type=F32) for r in rfs]
        loc_r = [jnp.dot(att_b[r, :], v_ref[0, r, :], preferred_element_type=F32) for r in rrs]
        sfs, srs = [sf], [sr]
        for u in range(per_step):
            sfs.append(sfs[-1] * et_f[pl.ds(cfs[u], 1), :] + upd_f[u])
            srs.append(srs[-1] * et_b[pl.ds(crs[u], 1), :] + upd_r[u])
        for u in range(per_step):
            acc_f[rfs[u], :] = loc_f[u] + lax.dot_general(qe_f[rfs[u], :], sfs[u].astype(BF16), nt,
                                                          preferred_element_type=F32)
            acc_b[rrs[u], :] = loc_r[u] + lax.dot_general(qe_b[rrs[u], :], srs[u].astype(BF16), nt,
                                                          preferred_element_type=F32)
        return sfs[-1], srs[-1]

    park_gates(0, 0, ring_a)
    park_gates(0, 1, ring_b)
    intra(0)

    def step(j, carry):
        carry = scan(j - 1, carry)
        intra(j)
        return carry

    zero = jnp.zeros((dk, dk), F32)
    carry = lax.fori_loop(1, n_steps, step, (zero, zero))
    scan(n_steps - 1, carry)

    blk = 512
    ng = ng_ref[...]

    def fin(j, _):
        rows = pl.ds(pl.multiple_of(j * blk, blk), blk)
        o = acc_f[rows, :] + acc_b[rows, :]
        o = o * lax.rsqrt(jnp.mean(o * o, axis=-1, keepdims=True) + RMS_EPS) * ng
        g = g_ref[0, rows, :].astype(F32)
        o_ref[0, rows, :] = (o * (g * _sigmoid(g))).astype(o_ref.dtype)
        return 0

    lax.fori_loop(0, seq // blk, fin, 0)


def _gla(proj, lb_logits, norm_g, batch, seq):
    view = proj.reshape(batch, seq, 5 * D_MODEL)

    def col(seg):
        return pl.BlockSpec((1, seq, HG_DIM), lambda b, h: (b, 0, seg * HG_HEADS + h))

    return pl.pallas_call(
        functools.partial(_gla_kernel, seq=seq),
        grid=(batch, HG_HEADS),
        in_specs=[col(0), col(1), col(2), col(3), col(4),
                  pl.BlockSpec((DEPTH, HG_DIM), lambda b, h: (0, h)),
                  pl.BlockSpec((1, HG_DIM), lambda b, h: (0, h))],
        out_specs=pl.BlockSpec((1, seq, HG_DIM), lambda b, h: (b, 0, h)),
        out_shape=jax.ShapeDtypeStruct((batch, seq, D_MODEL), BF16),
        scratch_shapes=[pltpu.VMEM((seq, HG_DIM), F32), pltpu.VMEM((seq, HG_DIM), F32),
                        pltpu.VMEM((seq, HG_DIM), BF16), pltpu.VMEM((seq, HG_DIM), BF16),
                        pltpu.VMEM((seq, HG_DIM), BF16), pltpu.VMEM((seq, HG_DIM), BF16),
                        pltpu.VMEM((seq // GLA_CHUNK, HG_DIM), F32),
                        pltpu.VMEM((seq // GLA_CHUNK, HG_DIM), F32),
                        pltpu.VMEM((seq, GLA_CHUNK), BF16), pltpu.VMEM((seq, GLA_CHUNK), BF16),
                        pltpu.VMEM((2, 2 * GLA_INTRA_UNROLL * GLA_CHUNK, HG_DIM), F32),
                        pltpu.VMEM((2, 2 * GLA_INTRA_UNROLL * GLA_CHUNK, HG_DIM), F32),
                        pltpu.VMEM((seq // GLA_CHUNK, HG_DIM, GLA_CHUNK), BF16)],
        compiler_params=_cparams("parallel", "parallel"),
        cost_estimate=_cost(batch * seq * HG_HEADS * 2 * 2 * HG_DIM * (5 * GLA_CHUNK + 2 * HG_DIM) // 2,
                            batch * seq * D_MODEL * 12, batch * seq * D_MODEL * 16),
        name="gla",
    )(view, view, view, view, view, lb_logits, norm_g).reshape(batch * seq, D_MODEL)


def _proj_ln_kernel(a_ref, w_ref, x_ref, g_ref, b_ref, y_ref):
    mix = jnp.dot(a_ref[...], w_ref[...], preferred_element_type=F32)
    y_ref[...] = _layer_norm(DEEPNORM_ALPHA * x_ref[...] + mix, g_ref[...], b_ref[...])


def _proj_ln(a, w, x, g, b, tm):
    n = x.shape[0]
    full = lambda t: pl.BlockSpec(t.shape, lambda i: (0, 0))
    row = pl.BlockSpec((tm, D_MODEL), lambda i: (i, 0))
    return pl.pallas_call(
        _proj_ln_kernel,
        grid=(n // tm,),
        in_specs=[row, full(w), row, full(g), full(b)],
        out_specs=row,
        out_shape=jax.ShapeDtypeStruct((n, D_MODEL), F32),
        compiler_params=_cparams("parallel"),
        cost_estimate=_cost(2 * n * D_MODEL * D_MODEL, n * D_MODEL * 10),
        name="proj_ln",
    )(a, w, x, g, b)


ROUTER_ROWS = 64
ROUTE_OUT_ROWS = 8


def _first_index_of(vals, target, n_rows):
    idx = lax.broadcasted_iota(jnp.int32, vals.shape, 0)
    return jnp.min(jnp.where(vals == target, idx, n_rows), axis=0, keepdims=True)


def _xattn_kernel(x_ref, wq_ref, kv_ref, wo_ref, g_ref, b_ref, wr_ref, br_ref, y_ref, yb_ref, route_ref):
    x = x_ref[...]
    q = jnp.dot(x.astype(BF16), wq_ref[...], preferred_element_type=F32) * (XA_HEAD_DIM ** -0.5)
    qb = q.astype(BF16)
    nt = (((1,), (1,)), ((), ()))
    heads = []
    for h in range(XA_HEADS):
        cols = slice(h * XA_HEAD_DIM, (h + 1) * XA_HEAD_DIM)
        k = kv_ref[0, :, cols]
        v = kv_ref[0, :, D_MODEL + h * XA_HEAD_DIM:D_MODEL + (h + 1) * XA_HEAD_DIM]
        sc = lax.dot_general(qb[:, cols], k, nt, preferred_element_type=F32)
        m = jnp.max(sc, axis=-1, keepdims=True)
        pe = jnp.exp(sc - m)
        p = pe / jnp.sum(pe, axis=-1, keepdims=True)
        heads.append(jnp.dot(p.astype(BF16), v, preferred_element_type=F32).astype(BF16))
    o = jnp.concatenate(heads, axis=1)
    xa = jnp.dot(o, wo_ref[...], preferred_element_type=F32)
    y = _layer_norm(DEEPNORM_ALPHA * x + xa, g_ref[...], b_ref[...])
    y_ref[...] = y
    yb_ref[...] = _pack_bf16_pair(y)

    lg = lax.dot_general(wr_ref[...], y.astype(BF16), nt, preferred_element_type=F32) + br_ref[:, 0:1]
    gl = lg[0:N_GROUPS, :]
    gmax = jnp.max(gl, axis=0, keepdims=True)
    g_w = 1.0 / jnp.sum(jnp.exp(gl - gmax), axis=0, keepdims=True)
    g_sel = _first_index_of(gl, gmax, N_GROUPS)
    el = jnp.zeros((EXPERTS_PER_GROUP, gl.shape[1]), F32)
    for grp in range(N_GROUPS):
        rows = slice(8 + grp * EXPERTS_PER_GROUP, 8 + (grp + 1) * EXPERTS_PER_GROUP)
        el = el + jnp.where(g_sel == grp, lg[rows, :], 0.0)
    m1 = jnp.max(el, axis=0, keepdims=True)
    i1 = _first_index_of(el, m1, EXPERTS_PER_GROUP)
    eidx = lax.broadcasted_iota(jnp.int32, el.shape, 0)
    el2 = jnp.where(eidx == i1, -jnp.inf, el)
    m2 = jnp.max(el2, axis=0, keepdims=True)
    i2 = _first_index_of(el2, m2, EXPERTS_PER_GROUP)
    e2 = jnp.exp(m2 - m1)
    den = 1.0 + e2
    w1 = g_w / den
    w2 = g_w * e2 / den
    base = g_sel * EXPERTS_PER_GROUP
    zero = jnp.zeros_like(w1)
    route_ref[...] = jnp.concatenate(
        [(base + i1).astype(F32), (base + i2).astype(F32), w1, w2, zero, zero, zero, zero], axis=0)


def _xattn(x, kv, wq, wo, g, b, wr, br, seq, tm):
    n = x.shape[0]
    spt = seq // tm
    full = lambda t: pl.BlockSpec(t.shape, lambda i: (0, 0))
    row = pl.BlockSpec((tm, D_MODEL), lambda i: (i, 0))
    return pl.pallas_call(
        _xattn_kernel,
        grid=(n // tm,),
        in_specs=[row, full(wq),
                  pl.BlockSpec((1,) + kv.shape[1:], lambda i: (i // spt, 0, 0)),
                  full(wo), full(g), full(b), full(wr), full(br)],
        out_specs=[row, pl.BlockSpec((tm, D_MODEL // 2), lambda i: (i, 0)),
                   pl.BlockSpec((ROUTE_OUT_ROWS, tm), lambda i: (0, i))],
        out_shape=[jax.ShapeDtypeStruct((n, D_MODEL), F32),
                   jax.ShapeDtypeStruct((n, D_MODEL // 2), jnp.int32),
                   jax.ShapeDtypeStruct((ROUTE_OUT_ROWS, n), F32)],
        compiler_params=_cparams("parallel"),
        cost_estimate=_cost(n * (4 * D_MODEL * D_MODEL + 4 * D_MODEL * kv.shape[1] + 2 * ROUTER_ROWS * D_MODEL),
                            n * D_MODEL * 10, n * XA_HEADS * kv.shape[1]),
        name="xattn_router",
    )(x, wq, kv, wo, g, b, wr, br)


def _expert_kernel(te_ref, nu_ref, xs_ref, w1_ref, w3_ref, w2_ref, ys_ref):
    j = pl.program_id(0)

    @pl.when(j < nu_ref[0])
    def _():
        half = D_MODEL // 2
        x_hi, x_lo = _unpack_bf16_pair(xs_ref[...])
        x_hi, x_lo = x_hi.astype(BF16), x_lo.astype(BF16)

        def up(w_ref):
            return (jnp.dot(x_hi, w_ref[0, 0:half, :].astype(BF16), preferred_element_type=F32)
                    + jnp.dot(x_lo, w_ref[0, half:D_MODEL, :].astype(BF16), preferred_element_type=F32))

        h1 = up(w1_ref)
        hid = h1 * _sigmoid(h1) * up(w3_ref)
        ys_ref[...] = _pack_bf16_pair(
            jnp.dot(hid.astype(BF16), w2_ref[0].astype(BF16), preferred_element_type=F32))

    @pl.when(j >= nu_ref[0])
    def _():
        ys_ref[...] = jnp.zeros_like(ys_ref)


def _experts(xs, tile_expert, n_used, w1, w3, w2, tm):
    mp = xs.shape[0]
    w_in = pl.BlockSpec((1, D_MODEL, EXPERT_FF), lambda j, te, nu: (te[j], 0, 0))
    rows = pl.BlockSpec((tm, D_MODEL // 2), lambda j, te, nu: (j, 0))
    grid_spec = pltpu.PrefetchScalarGridSpec(
        num_scalar_prefetch=2,
        grid=(mp // tm,),
        in_specs=[rows, w_in, w_in,
                  pl.BlockSpec((1, EXPERT_FF, D_MODEL), lambda j, te, nu: (te[j], 0, 0))],
        out_specs=rows,
    )
    return pl.pallas_call(
        _expert_kernel,
        grid_spec=grid_spec,
        out_shape=jax.ShapeDtypeStruct((mp, D_MODEL // 2), jnp.int32),
        compiler_params=_cparams("arbitrary"),
        cost_estimate=_cost(6 * mp * D_MODEL * EXPERT_FF, mp * D_MODEL * 4 + 12 * N_EXPERTS * D_MODEL * EXPERT_FF,
                            mp * EXPERT_FF),
        name="experts",
    )(tile_expert, n_used, xs, w1, w3, w2)


def _moe_ln_kernel(x_ref, y0_ref, y1_ref, gate_ref, g_ref, b_ref, o_ref):
    g0, g1 = gate_ref[:, 0:1], gate_ref[:, 1:2]
    hi0, lo0 = _unpack_bf16_pair(y0_ref[...])
    hi1, lo1 = _unpack_bf16_pair(y1_ref[...])
    ff = jnp.concatenate([g0 * hi0 + g1 * hi1, g0 * lo0 + g1 * lo1], axis=1)
    o_ref[...] = _layer_norm(DEEPNORM_ALPHA * x_ref[...] + ff, g_ref[...], b_ref[...])


def _moe_ln(x, y01, gates, g, b, tm):
    n = x.shape[0]
    full = lambda t: pl.BlockSpec(t.shape, lambda i: (0, 0))
    row = pl.BlockSpec((tm, D_MODEL), lambda i: (i, 0))
    second = n // tm
    return pl.pallas_call(
        _moe_ln_kernel,
        grid=(n // tm,),
        in_specs=[row,
                  pl.BlockSpec((tm, D_MODEL // 2), lambda i: (i, 0)),
                  pl.BlockSpec((tm, D_MODEL // 2), lambda i: (i + second, 0)),
                  pl.BlockSpec((tm, 2), lambda i: (i, 0)), full(g), full(b)],
        out_specs=row,
        out_shape=jax.ShapeDtypeStruct((n, D_MODEL), F32),
        compiler_params=_cparams("parallel"),
        cost_estimate=_cost(0, n * D_MODEL * 12),
        name="moe_ln",
    )(x, y01, y01, gates, g, b)


SC_CORES = 2
SC_SUBCORES = 16
SC_GATHER_ROWS = 32
SC_GATHER_COST_SCALE = 4


def _sc_gather_rows(table, idx):
    n_out = idx.shape[0]
    width = table.shape[1]
    workers = SC_CORES * SC_SUBCORES
    ch = SC_GATHER_ROWS
    per_w = n_out // workers
    steps = per_w // ch
    assert per_w * workers == n_out and steps * ch == per_w and steps % 2 == 0
    mesh = plsc.VectorSubcoreMesh(core_axis_name="c", subcore_axis_name="s")

    @functools.partial(
        pl.kernel, mesh=mesh,
        out_type=jax.ShapeDtypeStruct((n_out, width), table.dtype),
        cost_estimate=_cost(0, SC_GATHER_COST_SCALE * n_out * (2 * width * table.dtype.itemsize + 4)),
        scratch_types=[pltpu.VMEM((per_w,), jnp.int32),
                       pltpu.VMEM((ch, width), table.dtype), pltpu.VMEM((ch, width), table.dtype),
                       pltpu.SemaphoreType.DMA, pltpu.SemaphoreType.DMA,
                       pltpu.SemaphoreType.DMA, pltpu.SemaphoreType.DMA],
    )
    def gather_kernel(table_hbm, idx_hbm, out_hbm, idx_v, rows0, rows1, g0, g1, w0, w1):
        wid = lax.axis_index("s") * SC_CORES + lax.axis_index("c")
        base = wid * per_w
        pltpu.sync_copy(idx_hbm.at[pl.ds(base, per_w)], idx_v)
        bufs = ((rows0, g0, w0), (rows1, g1, w1))

        def gather(i, b):
            return pltpu.make_async_copy(table_hbm.at[idx_v.at[pl.ds(i * ch, ch)]], bufs[b][0], bufs[b][1])

        def write(i, b):
            return pltpu.make_async_copy(bufs[b][0], out_hbm.at[pl.ds(base + i * ch, ch)], bufs[b][2])

        gather(0, 0).start()

        @pl.loop(0, steps, step=2)
        def _(i):
            for b in range(2):
                ii = i + b
                gather(ii, b).wait()

                @pl.when(ii >= 1)
                def _():
                    write(ii - 1, 1 - b).wait()

                @pl.when(ii + 1 < steps)
                def _():
                    gather(ii + 1, 1 - b).start()

                write(ii, b).start()

        write(steps - 1, 1).wait()

    return gather_kernel(table, idx)


def _dispatch_plan(ids, n, tm_e):
    n_asg = 2 * n
    mp = n_asg + N_EXPERTS * tm_e
    order = jnp.argsort(ids, stable=True).astype(jnp.int32)
    pos = jnp.argsort(order).astype(jnp.int32)
    onehot = (ids[:, None] == jnp.arange(N_EXPERTS, dtype=jnp.int32)[None, :]).astype(jnp.int32)
    counts = jnp.sum(onehot, axis=0)
    dense_start = jnp.cumsum(counts) - counts
    padded = ((counts + tm_e - 1) // tm_e) * tm_e
    row_end = jnp.cumsum(padded)
    row_start = row_end - padded
    row_of_asg = pos + jnp.sum(onehot * (row_start - dense_start)[None, :], axis=1)
    tile_start = jnp.arange(mp // tm_e, dtype=jnp.int32) * tm_e
    tile_expert = jnp.minimum(jnp.sum((tile_start[:, None] >= row_end[None, :]).astype(jnp.int32), axis=1),
                              N_EXPERTS - 1)
    shift = (dense_start - row_start)[tile_expert]
    src = (tile_start + shift)[:, None] + jnp.arange(tm_e, dtype=jnp.int32)[None, :]
    asg_of_row = order[jnp.clip(src.reshape(mp), 0, n_asg - 1)]
    tok_of_row = jnp.where(asg_of_row >= n, asg_of_row - n, asg_of_row)
    n_used = (row_end[-1] // tm_e).astype(jnp.int32).reshape(1)
    return tok_of_row, row_of_asg, tile_expert.astype(jnp.int32), n_used


def _moe_steps(x, xp, route, layer, w1, w3, w2, g, b, tm_e, tm, baton):
    n = x.shape[0]
    ids = route[0:2].astype(jnp.int32).reshape(2 * n)
    tok_of_row, row_of_asg, tile_expert, n_used = _dispatch_plan(ids, n, tm_e)
    xs = _sc_gather_rows(xp, tok_of_row)
    if baton is not None and baton["lead"]:
        baton["box"].append(tok_of_row)
    yield
    ys = _experts(xs, tile_expert + layer * N_EXPERTS, n_used, w1, w3, w2, tm_e)
    y01 = _sc_gather_rows(ys, row_of_asg)
    yield
    return _moe_ln(x, y01, route[2:4].T, g, b, tm)


def _router_weights(w_group, b_group, w_expert, b_expert):
    wr = jnp.zeros((ROUTER_ROWS, D_MODEL), F32)
    wr = wr.at[0:N_GROUPS].set(w_group.T).at[8:8 + N_EXPERTS].set(w_expert.T)
    br = jnp.zeros((ROUTER_ROWS, LANES), F32)
    br = br.at[0:N_GROUPS, :].set(b_group[:, None]).at[8:8 + N_EXPERTS, :].set(b_expert[:, None])
    return wr.astype(BF16), br


def _trunk_steps(x3, mem3, p, baton=None):
    batch, seq, _ = x3.shape
    n = batch * seq
    x = x3.reshape(n, D_MODEL)
    mem = mem3.reshape(batch * mem3.shape[1], D_MODEL)
    tm = 512
    tables = _rope_tables(seq)
    for layer in range(DEPTH):
        j = layer // 2
        row = lambda a: a.reshape(1, D_MODEL)
        if baton is not None and not baton["lead"] and baton["box"]:
            x, _ = lax.optimization_barrier((x, baton["box"].pop()))
        if layer % 2 == 0:
            *views, gu = _even_proj(x, p["ev_w_in"][j], tables, seq, tm)
            res = [_band_attention(v, batch, seq, dil) for v, (_, dil) in zip(views, DILATED_PATTERNS)]
            x = _even_out([o for o, _ in res], [l for _, l in res], gu, x, p["ev_w_out"][j],
                          p["ev_conv_w"][j], row(p["ln_g"][layer, 0]), row(p["ln_b"][layer, 0]), seq, tm)
        else:
            proj = _matmul(x, p["od_w_in"][j], BF16, tm, D_MODEL)
            o = _gla(proj, p["lb_logits"], p["od_norm_g"][j].reshape(1, D_MODEL), batch, seq)
            x = _proj_ln(o, p["od_w_out"][j], x, row(p["ln_g"][layer, 0]), row(p["ln_b"][layer, 0]), tm)
        kv = _matmul(mem, p["xa_w_kv"][layer], BF16, 256, D_MODEL).reshape(batch, mem3.shape[1], 2 * D_MODEL)
        wr, br = _router_weights(p["moe_w_group"][layer], p["moe_b_group"][layer],
                                 p["moe_w_expert"][layer], p["moe_b_expert"][layer])
        x, xp, route = _xattn(x, kv, p["xa_w_q"][layer], p["xa_w_out"][layer],
                              row(p["ln_g"][layer, 1]), row(p["ln_b"][layer, 1]), wr, br, seq, tm)
        x = yield from _moe_steps(x, xp, route, layer, p["moe_w1"], p["moe_w3"], p["moe_w2"],
                                  row(p["ln_g"][layer, 2]), row(p["ln_b"][layer, 2]), 512, tm, baton)
    return x.reshape(batch, seq, D_MODEL)


def _run_interleaved(generators):
    results = [None] * len(generators)
    live = list(range(len(generators)))
    while live:
        for k in list(live):
            try:
                next(generators[k])
            except StopIteration as stop:
                results[k] = stop.value
                live.remove(k)
    return results


def _trunk(x3, mem3, p):
    return _run_interleaved([_trunk_steps(x3, mem3, p)])[0]


def kernel(x_prompt, x_sample, mem_prompt, mem_sample, ev_w_in, ev_conv_w, ev_w_out, od_w_in, lb_logits,
           od_norm_g, od_w_out, xa_w_q, xa_w_kv, xa_w_out, moe_w_group, moe_b_group, moe_w_expert,
           moe_b_expert, moe_w1, moe_w3, moe_w2, ln_g, ln_b):
    ff = moe_w1.shape[-1]
    p = dict(
        ev_w_in=ev_w_in.astype(BF16), ev_conv_w=ev_conv_w, ev_w_out=ev_w_out.astype(BF16),
        od_w_in=od_w_in.astype(BF16), lb_logits=lb_logits, od_norm_g=od_norm_g,
        od_w_out=od_w_out.astype(BF16), xa_w_q=xa_w_q.astype(BF16), xa_w_kv=xa_w_kv.astype(BF16),
        xa_w_out=xa_w_out.astype(BF16), moe_w_group=moe_w_group, moe_b_group=moe_b_group,
        moe_w_expert=moe_w_expert, moe_b_expert=moe_b_expert,
        moe_w1=moe_w1.reshape(DEPTH * N_EXPERTS, D_MODEL, ff),
        moe_w3=moe_w3.reshape(DEPTH * N_EXPERTS, D_MODEL, ff),
        moe_w2=moe_w2.reshape(DEPTH * N_EXPERTS, ff, D_MODEL),
        ln_g=ln_g, ln_b=ln_b)
    box = []
    y_prompt, y_sample = _run_interleaved([
        _trunk_steps(x_prompt, mem_prompt, p, dict(lead=True, box=box)),
        _trunk_steps(x_sample, mem_sample, p, dict(lead=False, box=box))])
    return y_prompt, y_sample
```

```python
import functools
import math

import jax
import jax.numpy as jnp
from jax import lax
from jax.experimental import pallas as pl
from jax.experimental.pallas import tpu as pltpu
from jax.experimental.pallas import tpu_sc as plsc

F32 = jnp.float32
BF16 = jnp.bfloat16

D_MODEL = 1024
DEPTH = 2
A_HEADS = 8
A_HEAD_DIM = 64
A_WIDTH = A_HEADS * A_HEAD_DIM
DILATED_PATTERNS = ((128, 1), (512, 4), (2048, 16))
ROPE_THETA = 500000.0
ROPE_DIM = A_HEAD_DIM // 4
B_WIDTH = D_MODEL // 2
CONV_WIDTH = 3
HG_HEADS = 8
HG_DIM = D_MODEL // HG_HEADS
XA_HEADS = 4
XA_HEAD_DIM = D_MODEL // XA_HEADS
N_GROUPS = 4
EXPERTS_PER_GROUP = 8
N_EXPERTS = N_GROUPS * EXPERTS_PER_GROUP
EXPERT_FF = D_MODEL // 4
LN_EPS = 1e-5
RMS_EPS = 1e-6
DEEPNORM_ALPHA = (2 * DEPTH) ** 0.25

LANES = 128
BAND_RADIUS = 64
ATTN_QBLOCK = 128
NEG_BIG = -1e30
GLA_CHUNK = 64
GLA_SUB = 16
GLA_EXP2_CLAMP = 100.0
GLA_INTRA_UNROLL = 2
LOG2_E = 1.4426950408889634
VMEM_LIMIT = 56 * 1024 * 1024


def _cost(flops, nbytes, transcendentals=0):
    return pl.CostEstimate(flops=int(flops), transcendentals=int(transcendentals), bytes_accessed=int(nbytes))


def _cparams(*sem):
    return pltpu.CompilerParams(dimension_semantics=sem, vmem_limit_bytes=VMEM_LIMIT)


def _layer_norm(y, g, b):
    mu = jnp.mean(y, axis=-1, keepdims=True)
    d = y - mu
    var = jnp.mean(d * d, axis=-1, keepdims=True)
    return d * lax.rsqrt(var + LN_EPS) * g + b


def _sigmoid(z):
    return 1.0 / (1.0 + jnp.exp(-z))


def _pack_bf16_pair(y):
    w = y.shape[1] // 2
    hi = lax.bitcast_convert_type(y[:, :w].astype(BF16).astype(F32), jnp.int32)
    lo = lax.bitcast_convert_type(y[:, w:].astype(BF16).astype(F32), jnp.int32)
    return hi | lax.shift_right_logical(lo, 16)


def _unpack_bf16_pair(p):
    hi = lax.bitcast_convert_type(p & jnp.int32(-65536), F32)
    lo = lax.bitcast_convert_type(lax.shift_left(p, 16), F32)
    return hi, lo


def _mm_kernel(x_ref, w_ref, o_ref, *, chunk):
    xb = x_ref[...].astype(BF16)
    for c in range(w_ref.shape[1] // chunk):
        cols = slice(c * chunk, (c + 1) * chunk)
        o_ref[:, cols] = jnp.dot(xb, w_ref[:, cols], preferred_element_type=F32).astype(o_ref.dtype)


def _matmul(x, w, out_dtype, tm, chunk):
    n, k = x.shape
    m = w.shape[1]
    return pl.pallas_call(
        functools.partial(_mm_kernel, chunk=chunk),
        grid=(n // tm,),
        in_specs=[pl.BlockSpec((tm, k), lambda i: (i, 0)), pl.BlockSpec((k, m), lambda i: (0, 0))],
        out_specs=pl.BlockSpec((tm, m), lambda i: (i, 0)),
        out_shape=jax.ShapeDtypeStruct((n, m), out_dtype),
        compiler_params=_cparams("parallel"),
        cost_estimate=_cost(2 * n * k * m, n * k * x.dtype.itemsize + 2 * k * m + n * m * 2),
        name="matmul",
    )(x, w)


def _rope_tables(seq):
    half = ROPE_DIM // 2
    inv_freq = jnp.exp(-math.log(ROPE_THETA) * jnp.arange(half, dtype=F32) * (2.0 / ROPE_DIM))
    ang = jnp.arange(seq, dtype=F32)[:, None] * inv_freq[None, :]
    cos, sin = jnp.cos(ang), jnp.sin(ang)
    ones = jnp.ones((seq, A_HEAD_DIM - ROPE_DIM), F32)
    zeros = jnp.zeros((seq, A_HEAD_DIM - ROPE_DIM), F32)
    zh = jnp.zeros((seq, half), F32)
    c = jnp.concatenate([cos, cos, ones], -1)
    s_up = jnp.concatenate([-sin, zh, zeros], -1)
    s_dn = jnp.concatenate([zh, sin, zeros], -1)
    rep = LANES // A_HEAD_DIM
    return tuple(jnp.tile(t, (1, rep)) for t in (c, s_up, s_dn))


def _even_proj_kernel(x_ref, w_ref, c_ref, su_ref, sd_ref, qkv_ref, qkv4_ref, qkv16_ref, gu_ref, slab_ref):
    tm = x_ref.shape[0]
    xb = x_ref[...].astype(BF16)
    rep = A_WIDTH // LANES
    half = ROPE_DIM // 2
    c = jnp.tile(c_ref[...], (1, rep))
    su = jnp.tile(su_ref[...], (1, rep))
    sd = jnp.tile(sd_ref[...], (1, rep))

    def proj(j):
        return jnp.dot(xb, w_ref[:, j * A_WIDTH:(j + 1) * A_WIDTH], preferred_element_type=F32)

    def rope(t):
        up = pltpu.roll(t, A_WIDTH - half, axis=1)
        dn = pltpu.roll(t, half, axis=1)
        return t * c + up * su + dn * sd

    qkv = (rope(proj(0)) * (A_HEAD_DIM ** -0.5), rope(proj(1)), proj(2))
    per = A_WIDTH // LANES
    for j, part in enumerate(qkv):
        qkv_ref[:, j * A_WIDTH:(j + 1) * A_WIDTH] = part.astype(BF16)
        for s in range(per):
            slab_ref[j * per + s] = part[:, s * LANES:(s + 1) * LANES]
    for dil, out_ref in ((DILATED_PATTERNS[1][1], qkv4_ref), (DILATED_PATTERNS[2][1], qkv16_ref)):
        for r in range(dil):
            for s in range(3 * per):
                val = slab_ref[s, pl.ds(r, tm // dil, stride=dil), :]
                col = r * 3 * A_WIDTH + s * LANES
                out_ref[:, col:col + LANES] = val.astype(BF16)
    gu_ref[:, 0:B_WIDTH] = proj(3).astype(BF16)
    gu_ref[:, B_WIDTH:2 * B_WIDTH] = (proj(4) * proj(5)).astype(BF16)


def _even_proj(x, w_in, tables, seq, tm):
    n = x.shape[0]
    spt = seq // tm
    tab_spec = pl.BlockSpec((tm, LANES), lambda i: (i % spt, 0))
    d4, d16 = DILATED_PATTERNS[1][1], DILATED_PATTERNS[2][1]
    width = 3 * A_WIDTH
    return pl.pallas_call(
        _even_proj_kernel,
        grid=(n // tm,),
        in_specs=[pl.BlockSpec((tm, D_MODEL), lambda i: (i, 0)),
                  pl.BlockSpec(w_in.shape, lambda i: (0, 0)),
                  tab_spec, tab_spec, tab_spec],
        out_specs=[pl.BlockSpec((tm, width), lambda i: (i, 0)),
                   pl.BlockSpec((tm // d4, d4 * width), lambda i: (i, 0)),
                   pl.BlockSpec((tm // d16, d16 * width), lambda i: (i, 0)),
                   pl.BlockSpec((tm, 2 * B_WIDTH), lambda i: (i, 0))],
        out_shape=[jax.ShapeDtypeStruct((n, width), BF16),
                   jax.ShapeDtypeStruct((n // d4, d4 * width), BF16),
                   jax.ShapeDtypeStruct((n // d16, d16 * width), BF16),
                   jax.ShapeDtypeStruct((n, 2 * B_WIDTH), BF16)],
        scratch_shapes=[pltpu.VMEM((width // LANES, tm, LANES), F32)],
        compiler_params=_cparams("parallel"),
        cost_estimate=_cost(2 * n * D_MODEL * 6 * A_WIDTH, n * (4 * D_MODEL + 2 * (9 * A_WIDTH + 2 * B_WIDTH))),
        name="even_proj",
    )(x, w_in, *tables)


def _band_attn_kernel(q_ref, kp_ref, km_ref, kn_ref, vp_ref, vm_ref, vn_ref, o_ref, lse_ref,
                      kbuf, vbuf, *, tq, length):
    i = pl.program_id(2)
    r = BAND_RADIUS
    kbuf[0:r] = kp_ref[0]
    kbuf[r:r + tq] = km_ref[0]
    kbuf[r + tq:r + tq + r] = kn_ref[0]
    vbuf[0:r] = vp_ref[0]
    vbuf[r:r + tq] = vm_ref[0]
    vbuf[r + tq:r + tq + r] = vn_ref[0]

    qb = ATTN_QBLOCK
    kw = qb + 2 * r
    qi = lax.broadcasted_iota(jnp.int32, (qb, kw), 0)
    kj = lax.broadcasted_iota(jnp.int32, (qb, kw), 1)
    rel = kj - qi
    band = (rel >= 0) & (rel <= 2 * r)
    lane = lax.broadcasted_iota(jnp.int32, (qb, LANES), 1)
    low = lane < A_HEAD_DIM
    nt = (((1,), (1,)), ((), ()))

    for s in range(tq // qb):
        kpos = i * tq + (s * qb - r) + kj
        valid = band & (kpos >= 0) & (kpos < length)
        bias = jnp.where(valid, 0.0, NEG_BIG)
        rows = slice(s * qb, (s + 1) * qb)
        wrows = slice(s * qb, s * qb + kw)
        for p in range(A_WIDTH // LANES):
            cols = slice(p * LANES, (p + 1) * LANES)
            qp = q_ref[0, rows, cols]
            kwin = kbuf[wrows, cols]
            vwin = vbuf[wrows, cols]
            outs, lses = [], []
            for sel in (low, jnp.logical_not(low)):
                qm = jnp.where(sel, qp, jnp.zeros_like(qp))
                sc = lax.dot_general(qm, kwin, nt, preferred_element_type=F32) + bias
                m = jnp.max(sc, axis=-1, keepdims=True)
                pe = jnp.exp(sc - m)
                l = jnp.sum(pe, axis=-1, keepdims=True)
                pv = jnp.dot(pe.astype(BF16), vwin, preferred_element_type=F32)
                outs.append(pv / l)
                lses.append(jnp.broadcast_to(m + jnp.log(l), (qb, LANES)))
            o_ref[0, rows, cols] = jnp.where(low, outs[0], outs[1]).astype(o_ref.dtype)
            lse_ref[0, rows, cols] = jnp.where(low, lses[0], lses[1])


def _band_attention(qkv_view, batch, seq, dil):
    length = seq // dil
    tq = min(512, length)
    r = BAND_RADIUS
    view = qkv_view.reshape(batch, length, dil * 3 * A_WIDTH)
    nblk_h = length // r
    per = tq // r

    def main(j):
        return pl.BlockSpec((1, tq, A_WIDTH), lambda b, rr, i: (b, i, rr * 3 + j))

    def prev(j):
        return pl.BlockSpec((1, r, A_WIDTH), lambda b, rr, i: (b, jnp.maximum(i * per - 1, 0), rr * 3 + j))

    def nxt(j):
        return pl.BlockSpec((1, r, A_WIDTH),
                            lambda b, rr, i: (b, jnp.minimum((i + 1) * per, nblk_h - 1), rr * 3 + j))

    out_spec = pl.BlockSpec((1, tq, A_WIDTH), lambda b, rr, i: (b, i, rr))
    o, lse = pl.pallas_call(
        functools.partial(_band_attn_kernel, tq=tq, length=length),
        grid=(batch, dil, length // tq),
        in_specs=[main(0), prev(1), main(1), nxt(1), prev(2), main(2), nxt(2)],
        out_specs=[out_spec, out_spec],
        out_shape=[jax.ShapeDtypeStruct((batch, length, dil * A_WIDTH), BF16),
                   jax.ShapeDtypeStruct((batch, length, dil * A_WIDTH), F32)],
        scratch_shapes=[pltpu.VMEM((tq + 2 * r, A_WIDTH), BF16), pltpu.VMEM((tq + 2 * r, A_WIDTH), BF16)],
        compiler_params=_cparams("parallel", "parallel", "parallel"),
        cost_estimate=_cost(batch * seq * A_HEADS * 8 * (ATTN_QBLOCK + 2 * BAND_RADIUS) * LANES,
                            batch * seq * A_WIDTH * (2 * 4 + 2 + 4),
                            batch * seq * A_HEADS * (ATTN_QBLOCK + 2 * BAND_RADIUS)),
        name=f"band_attn_d{dil}",
    )(view, view, view, view, view, view, view)
    return o.reshape(batch * length, dil * A_WIDTH), lse.reshape(batch * length, dil * A_WIDTH)


def _even_out_kernel(o1, o4, o16, l1, l4, l16, gu_ref, up_ref, un_ref, x_ref, w_ref, cw_ref, g_ref, b_ref,
                     y_ref, so4, sl4, so16, sl16, *, tm, seq):
    i = pl.program_id(0)
    pos = (i * tm) % seq
    per = A_WIDTH // LANES
    for dil, o_ref, l_ref, so, sl in ((DILATED_PATTERNS[1][1], o4, l4, so4, sl4),
                                      (DILATED_PATTERNS[2][1], o16, l16, so16, sl16)):
        for r in range(dil):
            for s in range(per):
                cols = slice(r * A_WIDTH + s * LANES, r * A_WIDTH + (s + 1) * LANES)
                so[s, pl.ds(r, tm // dil, stride=dil), :] = o_ref[:, cols].astype(F32)
                sl[s, pl.ds(r, tm // dil, stride=dil), :] = l_ref[:, cols]
    slabs = []
    for s in range(per):
        cols = slice(s * LANES, (s + 1) * LANES)
        la, lb, lc = l1[:, cols], sl4[s], sl16[s]
        mx = jnp.maximum(jnp.maximum(la, lb), lc)
        ea, eb, ec = jnp.exp(la - mx), jnp.exp(lb - mx), jnp.exp(lc - mx)
        num = ea * o1[:, cols].astype(F32) + eb * so4[s] + ec * so16[s]
        slabs.append(num / (ea + eb + ec))
    a_out = jnp.concatenate(slabs, axis=1)

    gate_b = gu_ref[:, 0:B_WIDTH].astype(F32)
    u = gu_ref[:, B_WIDTH:2 * B_WIDTH].astype(F32)
    hrows = up_ref.shape[0]
    u_before = jnp.where(pos > 0, up_ref[hrows - 1:hrows, :].astype(F32), 0.0)
    u_after = jnp.where(pos + tm < seq, un_ref[0:1, :].astype(F32), 0.0)
    row = lax.broadcasted_iota(jnp.int32, (tm, B_WIDTH), 0)
    u_prev = jnp.where(row == 0, u_before, pltpu.roll(u, 1, axis=0))
    u_next = jnp.where(row == tm - 1, u_after, pltpu.roll(u, tm - 1, axis=0))
    conv = u_prev * cw_ref[0:1, :] + u * cw_ref[1:2, :] + u_next * cw_ref[2:3, :]
    b_out = gate_b * conv

    mix = jnp.dot(a_out.astype(BF16), w_ref[0:A_WIDTH, :], preferred_element_type=F32)
    mix = mix + jnp.dot(b_out.astype(BF16), w_ref[A_WIDTH:A_WIDTH + B_WIDTH, :], preferred_element_type=F32)
    y_ref[...] = _layer_norm(DEEPNORM_ALPHA * x_ref[...] + mix, g_ref[...], b_ref[...])


def _even_out(outs, lses, gu, x, w_out, conv_w, g, b, seq, tm):
    n = x.shape[0]
    hrows = 16
    nh = n // hrows
    per = tm // hrows
    d4, d16 = DILATED_PATTERNS[1][1], DILATED_PATTERNS[2][1]
    views = [pl.BlockSpec((tm // d, d * A_WIDTH), lambda i: (i, 0)) for d in (1, d4, d16)]
    full = lambda a: pl.BlockSpec(a.shape, lambda i: (0, 0))
    slab = pltpu.VMEM((A_WIDTH // LANES, tm, LANES), F32)
    return pl.pallas_call(
        functools.partial(_even_out_kernel, tm=tm, seq=seq),
        grid=(n // tm,),
        scratch_shapes=[slab, slab, slab, slab],
        in_specs=views + views + [
            pl.BlockSpec((tm, 2 * B_WIDTH), lambda i: (i, 0)),
            pl.BlockSpec((hrows, B_WIDTH), lambda i: (jnp.maximum(i * per - 1, 0), 1)),
            pl.BlockSpec((hrows, B_WIDTH), lambda i: (jnp.minimum((i + 1) * per, nh - 1), 1)),
            pl.BlockSpec((tm, D_MODEL), lambda i: (i, 0)),
            full(w_out), full(conv_w), full(g), full(b)],
        out_specs=pl.BlockSpec((tm, D_MODEL), lambda i: (i, 0)),
        out_shape=jax.ShapeDtypeStruct((n, D_MODEL), F32),
        compiler_params=_cparams("parallel"),
        cost_estimate=_cost(2 * n * D_MODEL * D_MODEL, n * (A_WIDTH * 18 + 4 * B_WIDTH + 8 * D_MODEL)),
        name="even_out",
    )(*outs, *lses, gu, gu, gu, x, w_out, conv_w, g, b)


def _gla_gates(z, lb, tri2):
    f = lb + (1.0 - lb) * _sigmoid(z)
    lf2 = jnp.log(f) * LOG2_E
    hi = lf2.astype(BF16)
    lo = (lf2 - hi.astype(F32)).astype(BF16)
    cum = jnp.dot(tri2, jnp.concatenate([hi, lo], axis=0), preferred_element_type=F32)
    return 1.0 - f, cum


def _gla_scores(q, kk, cum, *, reverse):
    c = GLA_CHUNK
    sb = GLA_SUB
    dk = q.shape[1]
    nt = (((1,), (1,)), ((), ()))
    ti = lax.broadcasted_iota(jnp.int32, (c, c), 0)
    si = lax.broadcasted_iota(jnp.int32, (c, c), 1)
    causal = (si >= ti) if reverse else (ti >= si)

    parts = []
    for blk in range(c // sb):
        rows = slice(blk * sb, (blk + 1) * sb)
        if reverse:
            edge = (blk + 1) * sb
            ref = cum[edge:edge + 1, :] if edge < c else jnp.zeros((1, dk), F32)
            other = slice((blk + 1) * sb, c)
        else:
            edge = blk * sb - 1
            ref = cum[edge:edge + 1, :] if edge >= 0 else jnp.zeros((1, dk), F32)
            other = slice(0, blk * sb)
        qs = q[rows] * jnp.exp2(cum[rows] - ref)
        k_own = (kk[rows] * jnp.exp2(jnp.minimum(ref - cum[rows], GLA_EXP2_CLAMP))).astype(BF16)
        pieces = [k_own]
        n_other = other.stop - other.start
        if n_other:
            k_other = (kk[other] * jnp.exp2(ref - cum[other])).astype(BF16)
            pieces = [k_own, k_other] if reverse else [k_other, k_own]
        if n_other + sb < c:
            pad = jnp.zeros((c - n_other - sb, dk), BF16)
            pieces = [pad] + pieces if reverse else pieces + [pad]
        ks = jnp.concatenate(pieces, axis=0) if len(pieces) > 1 else pieces[0]
        parts.append(lax.dot_general(qs.astype(BF16), ks, nt, preferred_element_type=F32))
    return jnp.where(causal, jnp.concatenate(parts, axis=0), 0.0).astype(BF16)


def _gla_state_terms(q, kk, cum, *, reverse):
    c = GLA_CHUNK
    total = cum[0:1, :] if reverse else cum[c - 1:c, :]
    qe = (q * jnp.exp2(cum)).astype(BF16)
    kd = (kk * jnp.exp2(total - cum)).astype(BF16)
    return qe, kd, jnp.exp2(total)


def _gla_kernel(q_ref, zf_ref, zb_ref, v_ref, g_ref, lbl_ref, ng_ref, o_ref,
                acc_f, acc_b, qe_f, qe_b, kd_f, kd_b, et_f, et_b, att_f, att_b, ring_a, ring_b, vt_s,
                *, seq):
    c = GLA_CHUNK
    nc = seq // c
    dk = HG_DIM
    nt = (((1,), (1,)), ((), ()))
    l0 = lbl_ref[0:1, :]
    l1 = lbl_ref[1:2, :]
    mx = jnp.maximum(l0, l1)
    e0, e1 = jnp.exp(l0 - mx), jnp.exp(l1 - mx)
    lb = e0 / (e0 + e1)

    ti = lax.broadcasted_iota(jnp.int32, (c, 2 * c), 0)
    si = lax.broadcasted_iota(jnp.int32, (c, 2 * c), 1) & (c - 1)
    tri_f = jnp.where(ti >= si, 1.0, 0.0).astype(BF16)
    tri_b = jnp.where(si >= ti, 1.0, 0.0).astype(BF16)

    dirs = ((zf_ref, tri_f, att_f, qe_f, kd_f, et_f, False),
            (zb_ref, tri_b, att_b, qe_b, kd_b, et_b, True))
    grp = GLA_INTRA_UNROLL
    per_step = 2 * grp
    n_steps = nc // per_step
    chains = [(u, d) for u in range(grp) for d in range(2)]

    def chunk_of(step, half, u, d):
        ci = step * per_step + half * grp + u
        return (nc - 1 - ci) if d else ci

    def park_gates(step, half, ring):
        for k, (u, d) in enumerate(chains):
            rows = pl.ds(pl.multiple_of(chunk_of(step, half, u, d) * c, c), c)
            kk, cum = _gla_gates(dirs[d][0][0, rows, :].astype(F32), lb, dirs[d][1])
            ring[0, k * c:(k + 1) * c, :] = kk
            ring[1, k * c:(k + 1) * c, :] = cum

    def scores_from_ring(step, half, ring):
        for k, (u, d) in enumerate(chains):
            ci = chunk_of(step, half, u, d)
            rows = pl.ds(pl.multiple_of(ci * c, c), c)
            q = q_ref[0, rows, :].astype(F32)
            kk = ring[0, k * c:(k + 1) * c, :]
            cum = ring[1, k * c:(k + 1) * c, :]
            dirs[d][2][rows, :] = _gla_scores(q, kk, cum, reverse=dirs[d][6])
            qe, kd, et = _gla_state_terms(q, kk, cum, reverse=dirs[d][6])
            dirs[d][3][rows, :] = qe
            dirs[d][4][rows, :] = kd
            dirs[d][5][pl.ds(ci, 1), :] = et
            vt_s[ci] = v_ref[0, rows, :].astype(F32).T.astype(BF16)

    def intra(step):
        scores_from_ring(step, 0, ring_a)
        scores_from_ring(step, 1, ring_b)
        nxt = jnp.minimum(step + 1, n_steps - 1)
        park_gates(nxt, 0, ring_a)
        park_gates(nxt, 1, ring_b)

    def scan(j, carry):
        sf, sr = carry
        cfs = [j * per_step + u for u in range(per_step)]
        crs = [nc - 1 - cf for cf in cfs]
        rfs = [pl.ds(pl.multiple_of(cf * c, c), c) for cf in cfs]
        rrs = [pl.ds(pl.multiple_of(cr * c, c), c) for cr in crs]
        upd_f = [jnp.dot(vt_s[cf], kd_f[r, :], preferred_element_type=F32) for cf, r in zip(cfs, rfs)]
        upd_r = [jnp.dot(vt_s[cr], kd_b[r, :], preferred_element_type=F32) for cr, r in zip(crs, rrs)]
        loc_f = [jnp.dot(att_f[r, :], v_ref[0, r, :], preferred_element_type=F32) for r in rfs]
        loc_r = [jnp.dot(att_b[r, :], v_ref[0, r, :], preferred_element_type=F32) for r in rrs]
        sfs, srs = [sf], [sr]
        for u in range(per_step):
            sfs.append(sfs[-1] * et_f[pl.ds(cfs[u], 1), :] + upd_f[u])
            srs.append(srs[-1] * et_b[pl.ds(crs[u], 1), :] + upd_r[u])
        for u in range(per_step):
            acc_f[rfs[u], :] = loc_f[u] + lax.dot_general(qe_f[rfs[u], :], sfs[u].astype(BF16), nt,
                                                          preferred_element_type=F32)
            acc_b[rrs[u], :] = loc_r[u] + lax.dot_general(qe_b[rrs[u], :], srs[u].astype(BF16), nt,
                                                          preferred_element_type=F32)
        return sfs[-1], srs[-1]

    park_gates(0, 0, ring_a)
    park_gates(0, 1, ring_b)
    intra(0)

    def step(j, carry):
        carry = scan(j - 1, carry)
        intra(j)
        return carry

    zero = jnp.zeros((dk, dk), F32)
    carry = lax.fori_loop(1, n_steps, step, (zero, zero))
    scan(n_steps - 1, carry)

    blk = 512
    ng = ng_ref[...]

    def fin(j, _):
        rows = pl.ds(pl.multiple_of(j * blk, blk), blk)
        o = acc_f[rows, :] + acc_b[rows, :]
        o = o * lax.rsqrt(jnp.mean(o * o, axis=-1, keepdims=True) + RMS_EPS) * ng
        g = g_ref[0, rows, :].astype(F32)
        o_ref[0, rows, :] = (o * (g * _sigmoid(g))).astype(o_ref.dtype)
        return 0

    lax.fori_loop(0, seq // blk, fin, 0)


def _gla(proj, lb_logits, norm_g, batch, seq):
    view = proj.reshape(batch, seq, 5 * D_MODEL)

    def col(seg):
        return pl.BlockSpec((1, seq, HG_DIM), lambda b, h: (b, 0, seg * HG_HEADS + h))

    return pl.pallas_call(
        functools.partial(_gla_kernel, seq=seq),
        grid=(batch, HG_HEADS),
        in_specs=[col(0), col(1), col(2), col(3), col(4),
                  pl.BlockSpec((DEPTH, HG_DIM), lambda b, h: (0, h)),
                  pl.BlockSpec((1, HG_DIM), lambda b, h: (0, h))],
        out_specs=pl.BlockSpec((1, seq, HG_DIM), lambda b, h: (b, 0, h)),
        out_shape=jax.ShapeDtypeStruct((batch, seq, D_MODEL), BF16),
        scratch_shapes=[pltpu.VMEM((seq, HG_DIM), F32), pltpu.VMEM((seq, HG_DIM), F32),
                        pltpu.VMEM((seq, HG_DIM), BF16), pltpu.VMEM((seq, HG_DIM), BF16),
                        pltpu.VMEM((seq, HG_DIM), BF16), pltpu.VMEM((seq, HG_DIM), BF16),
                        pltpu.VMEM((seq // GLA_CHUNK, HG_DIM), F32),
                        pltpu.VMEM((seq // GLA_CHUNK, HG_DIM), F32),
                        pltpu.VMEM((seq, GLA_CHUNK), BF16), pltpu.VMEM((seq, GLA_CHUNK), BF16),
                        pltpu.VMEM((2, 2 * GLA_INTRA_UNROLL * GLA_CHUNK, HG_DIM), F32),
                        pltpu.VMEM((2, 2 * GLA_INTRA_UNROLL * GLA_CHUNK, HG_DIM), F32),
                        pltpu.VMEM((seq // GLA_CHUNK, HG_DIM, GLA_CHUNK), BF16)],
        compiler_params=_cparams("parallel", "parallel"),
        cost_estimate=_cost(batch * seq * HG_HEADS * 2 * 2 * HG_DIM * (5 * GLA_CHUNK + 2 * HG_DIM) // 2,
                            batch * seq * D_MODEL * 12, batch * seq * D_MODEL * 16),
        name="gla",
    )(view, view, view, view, view, lb_logits, norm_g).reshape(batch * seq, D_MODEL)


def _proj_ln_kernel(a_ref, w_ref, x_ref, g_ref, b_ref, y_ref):
    mix = jnp.dot(a_ref[...], w_ref[...], preferred_element_type=F32)
    y_ref[...] = _layer_norm(DEEPNORM_ALPHA * x_ref[...] + mix, g_ref[...], b_ref[...])


def _proj_ln(a, w, x, g, b, tm):
    n = x.shape[0]
    full = lambda t: pl.BlockSpec(t.shape, lambda i: (0, 0))
    row = pl.BlockSpec((tm, D_MODEL), lambda i: (i, 0))
    return pl.pallas_call(
        _proj_ln_kernel,
        grid=(n // tm,),
        in_specs=[row, full(w), row, full(g), full(b)],
        out_specs=row,
        out_shape=jax.ShapeDtypeStruct((n, D_MODEL), F32),
        compiler_params=_cparams("parallel"),
        cost_estimate=_cost(2 * n * D_MODEL * D_MODEL, n * D_MODEL * 10),
        name="proj_ln",
    )(a, w, x, g, b)


ROUTER_ROWS = 64
ROUTE_OUT_ROWS = 8


def _first_index_of(vals, target, n_rows):
    idx = lax.broadcasted_iota(jnp.int32, vals.shape, 0)
    return jnp.min(jnp.where(vals == target, idx, n_rows), axis=0, keepdims=True)


def _xattn_kernel(x_ref, wq_ref, kv_ref, wo_ref, g_ref, b_ref, wr_ref, br_ref, y_ref, yb_ref, route_ref):
    x = x_ref[...]
    q = jnp.dot(x.astype(BF16), wq_ref[...], preferred_element_type=F32) * (XA_HEAD_DIM ** -0.5)
    qb = q.astype(BF16)
    nt = (((1,), (1,)), ((), ()))
    heads = []
    for h in range(XA_HEADS):
        cols = slice(h * XA_HEAD_DIM, (h + 1) * XA_HEAD_DIM)
        k = kv_ref[0, :, cols]
        v = kv_ref[0, :, D_MODEL + h * XA_HEAD_DIM:D_MODEL + (h + 1) * XA_HEAD_DIM]
        sc = lax.dot_general(qb[:, cols], k, nt, preferred_element_type=F32)
        m = jnp.max(sc, axis=-1, keepdims=True)
        pe = jnp.exp(sc - m)
        p = pe / jnp.sum(pe, axis=-1, keepdims=True)
        heads.append(jnp.dot(p.astype(BF16), v, preferred_element_type=F32).astype(BF16))
    o = jnp.concatenate(heads, axis=1)
    xa = jnp.dot(o, wo_ref[...], preferred_element_type=F32)
    y = _layer_norm(DEEPNORM_ALPHA * x + xa, g_ref[...], b_ref[...])
    y_ref[...] = y
    yb_ref[...] = _pack_bf16_pair(y)

    lg = lax.dot_general(wr_ref[...], y.astype(BF16), nt, preferred_element_type=F32) + br_ref[:, 0:1]
    gl = lg[0:N_GROUPS, :]
    gmax = jnp.max(gl, axis=0, keepdims=True)
    g_w = 1.0 / jnp.sum(jnp.exp(gl - gmax), axis=0, keepdims=True)
    g_sel = _first_index_of(gl, gmax, N_GROUPS)
    el = jnp.zeros((EXPERTS_PER_GROUP, gl.shape[1]), F32)
    for grp in range(N_GROUPS):
        rows = slice(8 + grp * EXPERTS_PER_GROUP, 8 + (grp + 1) * EXPERTS_PER_GROUP)
        el = el + jnp.where(g_sel == grp, lg[rows, :], 0.0)
    m1 = jnp.max(el, axis=0, keepdims=True)
    i1 = _first_index_of(el, m1, EXPERTS_PER_GROUP)
    eidx = lax.broadcasted_iota(jnp.int32, el.shape, 0)
    el2 = jnp.where(eidx == i1, -jnp.inf, el)
    m2 = jnp.max(el2, axis=0, keepdims=True)
    i2 = _first_index_of(el2, m2, EXPERTS_PER_GROUP)
    e2 = jnp.exp(m2 - m1)
    den = 1.0 + e2
    w1 = g_w / den
    w2 = g_w * e2 / den
    base = g_sel * EXPERTS_PER_GROUP
    zero = jnp.zeros_like(w1)
    route_ref[...] = jnp.concatenate(
        [(base + i1).astype(F32), (base + i2).astype(F32), w1, w2, zero, zero, zero, zero], axis=0)


def _xattn(x, kv, wq, wo, g, b, wr, br, seq, tm):
    n = x.shape[0]
    spt = seq // tm
    full = lambda t: pl.BlockSpec(t.shape, lambda i: (0, 0))
    row = pl.BlockSpec((tm, D_MODEL), lambda i: (i, 0))
    return pl.pallas_call(
        _xattn_kernel,
        grid=(n // tm,),
        in_specs=[row, full(wq),
                  pl.BlockSpec((1,) + kv.shape[1:], lambda i: (i // spt, 0, 0)),
                  full(wo), full(g), full(b), full(wr), full(br)],
        out_specs=[row, pl.BlockSpec((tm, D_MODEL // 2), lambda i: (i, 0)),
                   pl.BlockSpec((ROUTE_OUT_ROWS, tm), lambda i: (0, i))],
        out_shape=[jax.ShapeDtypeStruct((n, D_MODEL), F32),
                   jax.ShapeDtypeStruct((n, D_MODEL // 2), jnp.int32),
                   jax.ShapeDtypeStruct((ROUTE_OUT_ROWS, n), F32)],
        compiler_params=_cparams("parallel"),
        cost_estimate=_cost(n * (4 * D_MODEL * D_MODEL + 4 * D_MODEL * kv.shape[1] + 2 * ROUTER_ROWS * D_MODEL),
                            n * D_MODEL * 10, n * XA_HEADS * kv.shape[1]),
        name="xattn_router",
    )(x, wq, kv, wo, g, b, wr, br)


def _expert_kernel(te_ref, nu_ref, xs_ref, w1_ref, w3_ref, w2_ref, ys_ref):
    j = pl.program_id(0)

    @pl.when(j < nu_ref[0])
    def _():
        half = D_MODEL // 2
        x_hi, x_lo = _unpack_bf16_pair(xs_ref[...])
        x_hi, x_lo = x_hi.astype(BF16), x_lo.astype(BF16)

        def up(w_ref):
            return (jnp.dot(x_hi, w_ref[0, 0:half, :].astype(BF16), preferred_element_type=F32)
                    + jnp.dot(x_lo, w_ref[0, half:D_MODEL, :].astype(BF16), preferred_element_type=F32))

        h1 = up(w1_ref)
        hid = h1 * _sigmoid(h1) * up(w3_ref)
        ys_ref[...] = _pack_bf16_pair(
            jnp.dot(hid.astype(BF16), w2_ref[0].astype(BF16), preferred_element_type=F32))

    @pl.when(j >= nu_ref[0])
    def _():
        ys_ref[...] = jnp.zeros_like(ys_ref)


def _experts(xs, tile_expert, n_used, w1, w3, w2, tm):
    mp = xs.shape[0]
    w_in = pl.BlockSpec((1, D_MODEL, EXPERT_FF), lambda j, te, nu: (te[j], 0, 0))
    rows = pl.BlockSpec((tm, D_MODEL // 2), lambda j, te, nu: (j, 0))
    grid_spec = pltpu.PrefetchScalarGridSpec(
        num_scalar_prefetch=2,
        grid=(mp // tm,),
        in_specs=[rows, w_in, w_in,
                  pl.BlockSpec((1, EXPERT_FF, D_MODEL), lambda j, te, nu: (te[j], 0, 0))],
        out_specs=rows,
    )
    return pl.pallas_call(
        _expert_kernel,
        grid_spec=grid_spec,
        out_shape=jax.ShapeDtypeStruct((mp, D_MODEL // 2), jnp.int32),
        compiler_params=_cparams("arbitrary"),
        cost_estimate=_cost(6 * mp * D_MODEL * EXPERT_FF, mp * D_MODEL * 4 + 12 * N_EXPERTS * D_MODEL * EXPERT_FF,
                            mp * EXPERT_FF),
        name="experts",
    )(tile_expert, n_used, xs, w1, w3, w2)


def _moe_ln_kernel(x_ref, y0_ref, y1_ref, gate_ref, g_ref, b_ref, o_ref):
    g0, g1 = gate_ref[:, 0:1], gate_ref[:, 1:2]
    hi0, lo0 = _unpack_bf16_pair(y0_ref[...])
    hi1, lo1 = _unpack_bf16_pair(y1_ref[...])
    ff = jnp.concatenate([g0 * hi0 + g1 * hi1, g0 * lo0 + g1 * lo1], axis=1)
    o_ref[...] = _layer_norm(DEEPNORM_ALPHA * x_ref[...] + ff, g_ref[...], b_ref[...])


def _moe_ln(x, y01, gates, g, b, tm):
    n = x.shape[0]
    full = lambda t: pl.BlockSpec(t.shape, lambda i: (0, 0))
    row = pl.BlockSpec((tm, D_MODEL), lambda i: (i, 0))
    second = n // tm
    return pl.pallas_call(
        _moe_ln_kernel,
        grid=(n // tm,),
        in_specs=[row,
                  pl.BlockSpec((tm, D_MODEL // 2), lambda i: (i, 0)),
                  pl.BlockSpec((tm, D_MODEL // 2), lambda i: (i + second, 0)),
                  pl.BlockSpec((tm, 2), lambda i: (i, 0)), full(g), full(b)],
        out_specs=row,
        out_shape=jax.ShapeDtypeStruct((n, D_MODEL), F32),
        compiler_params=_cparams("parallel"),
        cost_estimate=_cost(0, n * D_MODEL * 12),
        name="moe_ln",
    )(x, y01, y01, gates, g, b)


SC_CORES = 2
SC_SUBCORES = 16
SC_GATHER_ROWS = 32
SC_GATHER_BUFFERS = 4


def _sc_gather_rows(table, idx):
    n_out = idx.shape[0]
    width = table.shape[1]
    workers = SC_CORES * SC_SUBCORES
    ch = SC_GATHER_ROWS
    per_w = n_out // workers
    steps = per_w // ch
    nb = SC_GATHER_BUFFERS
    assert per_w * workers == n_out and steps * ch == per_w and steps % nb == 0
    mesh = plsc.VectorSubcoreMesh(core_axis_name="c", subcore_axis_name="s")

    @functools.partial(
        pl.kernel, mesh=mesh,
        out_type=jax.ShapeDtypeStruct((n_out, width), table.dtype),
        cost_estimate=_cost(0, n_out * (2 * width * table.dtype.itemsize + 4)),
        scratch_types=[pltpu.VMEM((per_w,), jnp.int32)]
                      + [pltpu.VMEM((ch, width), table.dtype)] * nb
                      + [pltpu.SemaphoreType.DMA] * (2 * nb),
    )
    def gather_kernel(table_hbm, idx_hbm, out_hbm, idx_v, *scratch):
        rows, gather_sems, write_sems = scratch[:nb], scratch[nb:2 * nb], scratch[2 * nb:]
        wid = lax.axis_index("s") * SC_CORES + lax.axis_index("c")
        base = wid * per_w
        pltpu.sync_copy(idx_hbm.at[pl.ds(base, per_w)], idx_v)

        def gather(i, b):
            return pltpu.make_async_copy(table_hbm.at[idx_v.at[pl.ds(i * ch, ch)]], rows[b], gather_sems[b])

        def write(i, b):
            return pltpu.make_async_copy(rows[b], out_hbm.at[pl.ds(base + i * ch, ch)], write_sems[b])

        for b in range(nb - 1):
            gather(b, b).start()

        @pl.loop(0, steps, step=nb)
        def _(i):
            for b in range(nb):
                ii = i + b
                gather(ii, b).wait()
                write(ii, b).start()
                prev = (b + nb - 1) % nb

                @pl.when(ii >= 1)
                def _():
                    write(ii - 1, prev).wait()

                @pl.when(ii + nb - 1 < steps)
                def _():
                    gather(ii + nb - 1, prev).start()

        write(steps - 1, (steps - 1) % nb).wait()

    return gather_kernel(table, idx)


def _dispatch_plan(ids, n, tm_e):
    n_asg = 2 * n
    mp = n_asg + N_EXPERTS * tm_e
    order = jnp.argsort(ids, stable=True).astype(jnp.int32)
    pos = jnp.argsort(order).astype(jnp.int32)
    onehot = (ids[:, None] == jnp.arange(N_EXPERTS, dtype=jnp.int32)[None, :]).astype(jnp.int32)
    counts = jnp.sum(onehot, axis=0)
    dense_start = jnp.cumsum(counts) - counts
    padded = ((counts + tm_e - 1) // tm_e) * tm_e
    row_end = jnp.cumsum(padded)
    row_start = row_end - padded
    row_of_asg = pos + jnp.sum(onehot * (row_start - dense_start)[None, :], axis=1)
    tile_start = jnp.arange(mp // tm_e, dtype=jnp.int32) * tm_e
    tile_expert = jnp.minimum(jnp.sum((tile_start[:, None] >= row_end[None, :]).astype(jnp.int32), axis=1),
                              N_EXPERTS - 1)
    shift = (dense_start - row_start)[tile_expert]
    src = (tile_start + shift)[:, None] + jnp.arange(tm_e, dtype=jnp.int32)[None, :]
    asg_of_row = order[jnp.clip(src.reshape(mp), 0, n_asg - 1)]
    tok_of_row = jnp.where(asg_of_row >= n, asg_of_row - n, asg_of_row)
    n_used = (row_end[-1] // tm_e).astype(jnp.int32).reshape(1)
    return tok_of_row, row_of_asg, tile_expert.astype(jnp.int32), n_used


def _moe_steps(x, xp, route, layer, w1, w3, w2, g, b, tm_e, tm, baton):
    n = x.shape[0]
    ids = route[0:2].astype(jnp.int32).reshape(2 * n)
    tok_of_row, row_of_asg, tile_expert, n_used = _dispatch_plan(ids, n, tm_e)
    xs = _sc_gather_rows(xp, tok_of_row)
    if baton is not None and baton["lead"]:
        baton["box"].append(tok_of_row)
    yield
    ys = _experts(xs, tile_expert + layer * N_EXPERTS, n_used, w1, w3, w2, tm_e)
    y01 = _sc_gather_rows(ys, row_of_asg)
    yield
    return _moe_ln(x, y01, route[2:4].T, g, b, tm)


def _router_weights(w_group, b_group, w_expert, b_expert):
    wr = jnp.zeros((ROUTER_ROWS, D_MODEL), F32)
    wr = wr.at[0:N_GROUPS].set(w_group.T).at[8:8 + N_EXPERTS].set(w_expert.T)
    br = jnp.zeros((ROUTER_ROWS, LANES), F32)
    br = br.at[0:N_GROUPS, :].set(b_group[:, None]).at[8:8 + N_EXPERTS, :].set(b_expert[:, None])
    return wr.astype(BF16), br


def _trunk_steps(x3, mem3, p, baton=None):
    batch, seq, _ = x3.shape
    n = batch * seq
    x = x3.reshape(n, D_MODEL)
    mem = mem3.reshape(batch * mem3.shape[1], D_MODEL)
    tm = 512
    tables = _rope_tables(seq)
    for layer in range(DEPTH):
        j = layer // 2
        row = lambda a: a.reshape(1, D_MODEL)
        if baton is not None and not baton["lead"] and baton["box"]:
            x, _ = lax.optimization_barrier((x, baton["box"].pop()))
        if layer % 2 == 0:
            *views, gu = _even_proj(x, p["ev_w_in"][j], tables, seq, tm)
            res = [_band_attention(v, batch, seq, dil) for v, (_, dil) in zip(views, DILATED_PATTERNS)]
            x = _even_out([o for o, _ in res], [l for _, l in res], gu, x, p["ev_w_out"][j],
                          p["ev_conv_w"][j], row(p["ln_g"][layer, 0]), row(p["ln_b"][layer, 0]), seq, tm)
        else:
            proj = _matmul(x, p["od_w_in"][j], BF16, tm, D_MODEL)
            o = _gla(proj, p["lb_logits"], p["od_norm_g"][j].reshape(1, D_MODEL), batch, seq)
            x = _proj_ln(o, p["od_w_out"][j], x, row(p["ln_g"][layer, 0]), row(p["ln_b"][layer, 0]), tm)
        kv = _matmul(mem, p["xa_w_kv"][layer], BF16, 256, D_MODEL).reshape(batch, mem3.shape[1], 2 * D_MODEL)
        wr, br = _router_weights(p["moe_w_group"][layer], p["moe_b_group"][layer],
                                 p["moe_w_expert"][layer], p["moe_b_expert"][layer])
        x, xp, route = _xattn(x, kv, p["xa_w_q"][layer], p["xa_w_out"][layer],
                              row(p["ln_g"][layer, 1]), row(p["ln_b"][layer, 1]), wr, br, seq, tm)
        x = yield from _moe_steps(x, xp, route, layer, p["moe_w1"], p["moe_w3"], p["moe_w2"],
                                  row(p["ln_g"][layer, 2]), row(p["ln_b"][layer, 2]), 512, tm, baton)
    return x.reshape(batch, seq, D_MODEL)


def _run_interleaved(generators):
    results = [None] * len(generators)
    live = list(range(len(generators)))
    while live:
        for k in list(live):
            try:
                next(generators[k])
            except StopIteration as stop:
                results[k] = stop.value
                live.remove(k)
    return results


def _trunk(x3, mem3, p):
    return _run_interleaved([_trunk_steps(x3, mem3, p)])[0]


def kernel(x_prompt, x_sample, mem_prompt, mem_sample, ev_w_in, ev_conv_w, ev_w_out, od_w_in, lb_logits,
           od_norm_g, od_w_out, xa_w_q, xa_w_kv, xa_w_out, moe_w_group, moe_b_group, moe_w_expert,
           moe_b_expert, moe_w1, moe_w3, moe_w2, ln_g, ln_b):
    ff = moe_w1.shape[-1]
    p = dict(
        ev_w_in=ev_w_in.astype(BF16), ev_conv_w=ev_conv_w, ev_w_out=ev_w_out.astype(BF16),
        od_w_in=od_w_in.astype(BF16), lb_logits=lb_logits, od_norm_g=od_norm_g,
        od_w_out=od_w_out.astype(BF16), xa_w_q=xa_w_q.astype(BF16), xa_w_kv=xa_w_kv.astype(BF16),
        xa_w_out=xa_w_out.astype(BF16), moe_w_group=moe_w_group, moe_b_group=moe_b_group,
        moe_w_expert=moe_w_expert, moe_b_expert=moe_b_expert,
        moe_w1=moe_w1.reshape(DEPTH * N_EXPERTS, D_MODEL, ff),
        moe_w3=moe_w3.reshape(DEPTH * N_EXPERTS, D_MODEL, ff),
        moe_w2=moe_w2.reshape(DEPTH * N_EXPERTS, ff, D_MODEL),
        ln_g=ln_g, ln_b=ln_b)
    box = []
    y_prompt, y_sample = _run_interleaved([
        _trunk_steps(x_prompt, mem_prompt, p, dict(lead=True, box=box)),
        _trunk_steps(x_sample, mem_sample, p, dict(lead=False, box=box))])
    return y_prompt, y_sample
```

```python
import functools
import math

import jax
import jax.numpy as jnp
from jax import lax
from jax.experimental import pallas as pl
from jax.experimental.pallas import tpu as pltpu
from jax.experimental.pallas import tpu_sc as plsc

F32 = jnp.float32
BF16 = jnp.bfloat16

D_MODEL = 1024
DEPTH = 2
A_HEADS = 8
A_HEAD_DIM = 64
A_WIDTH = A_HEADS * A_HEAD_DIM
DILATED_PATTERNS = ((128, 1), (512, 4), (2048, 16))
ROPE_THETA = 500000.0
ROPE_DIM = A_HEAD_DIM // 4
B_WIDTH = D_MODEL // 2
CONV_WIDTH = 3
HG_HEADS = 8
HG_DIM = D_MODEL // HG_HEADS
XA_HEADS = 4
XA_HEAD_DIM = D_MODEL // XA_HEADS
N_GROUPS = 4
EXPERTS_PER_GROUP = 8
N_EXPERTS = N_GROUPS * EXPERTS_PER_GROUP
EXPERT_FF = D_MODEL // 4
LN_EPS = 1e-5
RMS_EPS = 1e-6
DEEPNORM_ALPHA = (2 * DEPTH) ** 0.25

LANES = 128
BAND_RADIUS = 64
ATTN_QBLOCK = 128
NEG_BIG = -1e30
GLA_CHUNK = 64
GLA_SUB = 16
GLA_EXP2_CLAMP = 100.0
GLA_INTRA_UNROLL = 2
LOG2_E = 1.4426950408889634
VMEM_LIMIT = 56 * 1024 * 1024


def _cost(flops, nbytes, transcendentals=0):
    return pl.CostEstimate(flops=int(flops), transcendentals=int(transcendentals), bytes_accessed=int(nbytes))


def _cparams(*sem):
    return pltpu.CompilerParams(dimension_semantics=sem, vmem_limit_bytes=VMEM_LIMIT)


def _layer_norm(y, g, b):
    mu = jnp.mean(y, axis=-1, keepdims=True)
    d = y - mu
    var = jnp.mean(d * d, axis=-1, keepdims=True)
    return d * lax.rsqrt(var + LN_EPS) * g + b


def _sigmoid(z):
    return 1.0 / (1.0 + jnp.exp(-z))


def _pack_bf16_pair(y):
    w = y.shape[1] // 2
    hi = lax.bitcast_convert_type(y[:, :w].astype(BF16).astype(F32), jnp.int32)
    lo = lax.bitcast_convert_type(y[:, w:].astype(BF16).astype(F32), jnp.int32)
    return hi | lax.shift_right_logical(lo, 16)


def _unpack_bf16_pair(p):
    hi = lax.bitcast_convert_type(p & jnp.int32(-65536), F32)
    lo = lax.bitcast_convert_type(lax.shift_left(p, 16), F32)
    return hi, lo


def _mm_kernel(x_ref, w_ref, o_ref, *, chunk):
    xb = x_ref[...].astype(BF16)
    for c in range(w_ref.shape[1] // chunk):
        cols = slice(c * chunk, (c + 1) * chunk)
        o_ref[:, cols] = jnp.dot(xb, w_ref[:, cols], preferred_element_type=F32).astype(o_ref.dtype)


def _matmul(x, w, out_dtype, tm, chunk):
    n, k = x.shape
    m = w.shape[1]
    return pl.pallas_call(
        functools.partial(_mm_kernel, chunk=chunk),
        grid=(n // tm,),
        in_specs=[pl.BlockSpec((tm, k), lambda i: (i, 0)), pl.BlockSpec((k, m), lambda i: (0, 0))],
        out_specs=pl.BlockSpec((tm, m), lambda i: (i, 0)),
        out_shape=jax.ShapeDtypeStruct((n, m), out_dtype),
        compiler_params=_cparams("parallel"),
        cost_estimate=_cost(2 * n * k * m, n * k * x.dtype.itemsize + 2 * k * m + n * m * 2),
        name="matmul",
    )(x, w)


def _rope_tables(seq):
    half = ROPE_DIM // 2
    inv_freq = jnp.exp(-math.log(ROPE_THETA) * jnp.arange(half, dtype=F32) * (2.0 / ROPE_DIM))
    ang = jnp.arange(seq, dtype=F32)[:, None] * inv_freq[None, :]
    cos, sin = jnp.cos(ang), jnp.sin(ang)
    ones = jnp.ones((seq, A_HEAD_DIM - ROPE_DIM), F32)
    zeros = jnp.zeros((seq, A_HEAD_DIM - ROPE_DIM), F32)
    zh = jnp.zeros((seq, half), F32)
    c = jnp.concatenate([cos, cos, ones], -1)
    s_up = jnp.concatenate([-sin, zh, zeros], -1)
    s_dn = jnp.concatenate([zh, sin, zeros], -1)
    rep = LANES // A_HEAD_DIM
    return tuple(jnp.tile(t, (1, rep)) for t in (c, s_up, s_dn))


def _even_proj_kernel(x_ref, w_ref, c_ref, su_ref, sd_ref, qkv_ref, qkv4_ref, qkv16_ref, gu_ref, slab_ref):
    tm = x_ref.shape[0]
    xb = x_ref[...].astype(BF16)
    rep = A_WIDTH // LANES
    half = ROPE_DIM // 2
    c = jnp.tile(c_ref[...], (1, rep))
    su = jnp.tile(su_ref[...], (1, rep))
    sd = jnp.tile(sd_ref[...], (1, rep))

    def proj(j):
        return jnp.dot(xb, w_ref[:, j * A_WIDTH:(j + 1) * A_WIDTH], preferred_element_type=F32)

    def rope(t):
        up = pltpu.roll(t, A_WIDTH - half, axis=1)
        dn = pltpu.roll(t, half, axis=1)
        return t * c + up * su + dn * sd

    qkv = (rope(proj(0)) * (A_HEAD_DIM ** -0.5), rope(proj(1)), proj(2))
    per = A_WIDTH // LANES
    for j, part in enumerate(qkv):
        qkv_ref[:, j * A_WIDTH:(j + 1) * A_WIDTH] = part.astype(BF16)
        for s in range(per):
            slab_ref[j * per + s] = part[:, s * LANES:(s + 1) * LANES]
    for dil, out_ref in ((DILATED_PATTERNS[1][1], qkv4_ref), (DILATED_PATTERNS[2][1], qkv16_ref)):
        for r in range(dil):
            for s in range(3 * per):
                val = slab_ref[s, pl.ds(r, tm // dil, stride=dil), :]
                col = r * 3 * A_WIDTH + s * LANES
                out_ref[:, col:col + LANES] = val.astype(BF16)
    gu_ref[:, 0:B_WIDTH] = proj(3).astype(BF16)
    gu_ref[:, B_WIDTH:2 * B_WIDTH] = (proj(4) * proj(5)).astype(BF16)


def _even_proj(x, w_in, tables, seq, tm):
    n = x.shape[0]
    spt = seq // tm
    tab_spec = pl.BlockSpec((tm, LANES), lambda i: (i % spt, 0))
    d4, d16 = DILATED_PATTERNS[1][1], DILATED_PATTERNS[2][1]
    width = 3 * A_WIDTH
    return pl.pallas_call(
        _even_proj_kernel,
        grid=(n // tm,),
        in_specs=[pl.BlockSpec((tm, D_MODEL), lambda i: (i, 0)),
                  pl.BlockSpec(w_in.shape, lambda i: (0, 0)),
                  tab_spec, tab_spec, tab_spec],
        out_specs=[pl.BlockSpec((tm, width), lambda i: (i, 0)),
                   pl.BlockSpec((tm // d4, d4 * width), lambda i: (i, 0)),
                   pl.BlockSpec((tm // d16, d16 * width), lambda i: (i, 0)),
                   pl.BlockSpec((tm, 2 * B_WIDTH), lambda i: (i, 0))],
        out_shape=[jax.ShapeDtypeStruct((n, width), BF16),
                   jax.ShapeDtypeStruct((n // d4, d4 * width), BF16),
                   jax.ShapeDtypeStruct((n // d16, d16 * width), BF16),
                   jax.ShapeDtypeStruct((n, 2 * B_WIDTH), BF16)],
        scratch_shapes=[pltpu.VMEM((width // LANES, tm, LANES), F32)],
        compiler_params=_cparams("parallel"),
        cost_estimate=_cost(2 * n * D_MODEL * 6 * A_WIDTH, n * (4 * D_MODEL + 2 * (9 * A_WIDTH + 2 * B_WIDTH))),
        name="even_proj",
    )(x, w_in, *tables)


def _band_attn_kernel(q_ref, kp_ref, km_ref, kn_ref, vp_ref, vm_ref, vn_ref, o_ref, lse_ref,
                      kbuf, vbuf, *, tq, length):
    i = pl.program_id(2)
    r = BAND_RADIUS
    kbuf[0:r] = kp_ref[0]
    kbuf[r:r + tq] = km_ref[0]
    kbuf[r + tq:r + tq + r] = kn_ref[0]
    vbuf[0:r] = vp_ref[0]
    vbuf[r:r + tq] = vm_ref[0]
    vbuf[r + tq:r + tq + r] = vn_ref[0]

    qb = ATTN_QBLOCK
    kw = qb + 2 * r
    qi = lax.broadcasted_iota(jnp.int32, (qb, kw), 0)
    kj = lax.broadcasted_iota(jnp.int32, (qb, kw), 1)
    rel = kj - qi
    band = (rel >= 0) & (rel <= 2 * r)
    lane = lax.broadcasted_iota(jnp.int32, (qb, LANES), 1)
    low = lane < A_HEAD_DIM
    nt = (((1,), (1,)), ((), ()))

    for s in range(tq // qb):
        kpos = i * tq + (s * qb - r) + kj
        valid = band & (kpos >= 0) & (kpos < length)
        bias = jnp.where(valid, 0.0, NEG_BIG)
        rows = slice(s * qb, (s + 1) * qb)
        wrows = slice(s * qb, s * qb + kw)
        for p in range(A_WIDTH // LANES):
            cols = slice(p * LANES, (p + 1) * LANES)
            qp = q_ref[0, rows, cols]
            kwin = kbuf[wrows, cols]
            vwin = vbuf[wrows, cols]
            outs, lses = [], []
            for sel in (low, jnp.logical_not(low)):
                qm = jnp.where(sel, qp, jnp.zeros_like(qp))
                sc = lax.dot_general(qm, kwin, nt, preferred_element_type=F32) + bias
                m = jnp.max(sc, axis=-1, keepdims=True)
                pe = jnp.exp(sc - m)
                l = jnp.sum(pe, axis=-1, keepdims=True)
                pv = jnp.dot(pe.astype(BF16), vwin, preferred_element_type=F32)
                outs.append(pv / l)
                lses.append(jnp.broadcast_to(m + jnp.log(l), (qb, LANES)))
            o_ref[0, rows, cols] = jnp.where(low, outs[0], outs[1]).astype(o_ref.dtype)
            lse_ref[0, rows, cols] = jnp.where(low, lses[0], lses[1])


def _band_attention(qkv_view, batch, seq, dil):
    length = seq // dil
    tq = min(512, length)
    r = BAND_RADIUS
    view = qkv_view.reshape(batch, length, dil * 3 * A_WIDTH)
    nblk_h = length // r
    per = tq // r

    def main(j):
        return pl.BlockSpec((1, tq, A_WIDTH), lambda b, rr, i: (b, i, rr * 3 + j))

    def prev(j):
        return pl.BlockSpec((1, r, A_WIDTH), lambda b, rr, i: (b, jnp.maximum(i * per - 1, 0), rr * 3 + j))

    def nxt(j):
        return pl.BlockSpec((1, r, A_WIDTH),
                            lambda b, rr, i: (b, jnp.minimum((i + 1) * per, nblk_h - 1), rr * 3 + j))

    out_spec = pl.BlockSpec((1, tq, A_WIDTH), lambda b, rr, i: (b, i, rr))
    o, lse = pl.pallas_call(
        functools.partial(_band_attn_kernel, tq=tq, length=length),
        grid=(batch, dil, length // tq),
        in_specs=[main(0), prev(1), main(1), nxt(1), prev(2), main(2), nxt(2)],
        out_specs=[out_spec, out_spec],
        out_shape=[jax.ShapeDtypeStruct((batch, length, dil * A_WIDTH), BF16),
                   jax.ShapeDtypeStruct((batch, length, dil * A_WIDTH), F32)],
        scratch_shapes=[pltpu.VMEM((tq + 2 * r, A_WIDTH), BF16), pltpu.VMEM((tq + 2 * r, A_WIDTH), BF16)],
        compiler_params=_cparams("parallel", "parallel", "parallel"),
        cost_estimate=_cost(batch * seq * A_HEADS * 8 * (ATTN_QBLOCK + 2 * BAND_RADIUS) * LANES,
                            batch * seq * A_WIDTH * (2 * 4 + 2 + 4),
                            batch * seq * A_HEADS * (ATTN_QBLOCK + 2 * BAND_RADIUS)),
        name=f"band_attn_d{dil}",
    )(view, view, view, view, view, view, view)
    return o.reshape(batch * length, dil * A_WIDTH), lse.reshape(batch * length, dil * A_WIDTH)


def _even_out_kernel(o1, o4, o16, l1, l4, l16, gu_ref, up_ref, un_ref, x_ref, w_ref, cw_ref, g_ref, b_ref,
                     y_ref, so4, sl4, so16, sl16, *, tm, seq):
    i = pl.program_id(0)
    pos = (i * tm) % seq
    per = A_WIDTH // LANES
    for dil, o_ref, l_ref, so, sl in ((DILATED_PATTERNS[1][1], o4, l4, so4, sl4),
                                      (DILATED_PATTERNS[2][1], o16, l16, so16, sl16)):
        for r in range(dil):
            for s in range(per):
                cols = slice(r * A_WIDTH + s * LANES, r * A_WIDTH + (s + 1) * LANES)
                so[s, pl.ds(r, tm // dil, stride=dil), :] = o_ref[:, cols].astype(F32)
                sl[s, pl.ds(r, tm // dil, stride=dil), :] = l_ref[:, cols]
    slabs = []
    for s in range(per):
        cols = slice(s * LANES, (s + 1) * LANES)
        la, lb, lc = l1[:, cols], sl4[s], sl16[s]
        mx = jnp.maximum(jnp.maximum(la, lb), lc)
        ea, eb, ec = jnp.exp(la - mx), jnp.exp(lb - mx), jnp.exp(lc - mx)
        num = ea * o1[:, cols].astype(F32) + eb * so4[s] + ec * so16[s]
        slabs.append(num / (ea + eb + ec))
    a_out = jnp.concatenate(slabs, axis=1)

    gate_b = gu_ref[:, 0:B_WIDTH].astype(F32)
    u = gu_ref[:, B_WIDTH:2 * B_WIDTH].astype(F32)
    hrows = up_ref.shape[0]
    u_before = jnp.where(pos > 0, up_ref[hrows - 1:hrows, :].astype(F32), 0.0)
    u_after = jnp.where(pos + tm < seq, un_ref[0:1, :].astype(F32), 0.0)
    row = lax.broadcasted_iota(jnp.int32, (tm, B_WIDTH), 0)
    u_prev = jnp.where(row == 0, u_before, pltpu.roll(u, 1, axis=0))
    u_next = jnp.where(row == tm - 1, u_after, pltpu.roll(u, tm - 1, axis=0))
    conv = u_prev * cw_ref[0:1, :] + u * cw_ref[1:2, :] + u_next * cw_ref[2:3, :]
    b_out = gate_b * conv

    mix = jnp.dot(a_out.astype(BF16), w_ref[0:A_WIDTH, :], preferred_element_type=F32)
    mix = mix + jnp.dot(b_out.astype(BF16), w_ref[A_WIDTH:A_WIDTH + B_WIDTH, :], preferred_element_type=F32)
    y_ref[...] = _layer_norm(DEEPNORM_ALPHA * x_ref[...] + mix, g_ref[...], b_ref[...])


def _even_out(outs, lses, gu, x, w_out, conv_w, g, b, seq, tm):
    n = x.shape[0]
    hrows = 16
    nh = n // hrows
    per = tm // hrows
    d4, d16 = DILATED_PATTERNS[1][1], DILATED_PATTERNS[2][1]
    views = [pl.BlockSpec((tm // d, d * A_WIDTH), lambda i: (i, 0)) for d in (1, d4, d16)]
    full = lambda a: pl.BlockSpec(a.shape, lambda i: (0, 0))
    slab = pltpu.VMEM((A_WIDTH // LANES, tm, LANES), F32)
    return pl.pallas_call(
        functools.partial(_even_out_kernel, tm=tm, seq=seq),
        grid=(n // tm,),
        scratch_shapes=[slab, slab, slab, slab],
        in_specs=views + views + [
            pl.BlockSpec((tm, 2 * B_WIDTH), lambda i: (i, 0)),
            pl.BlockSpec((hrows, B_WIDTH), lambda i: (jnp.maximum(i * per - 1, 0), 1)),
            pl.BlockSpec((hrows, B_WIDTH), lambda i: (jnp.minimum((i + 1) * per, nh - 1), 1)),
            pl.BlockSpec((tm, D_MODEL), lambda i: (i, 0)),
            full(w_out), full(conv_w), full(g), full(b)],
        out_specs=pl.BlockSpec((tm, D_MODEL), lambda i: (i, 0)),
        out_shape=jax.ShapeDtypeStruct((n, D_MODEL), F32),
        compiler_params=_cparams("parallel"),
        cost_estimate=_cost(2 * n * D_MODEL * D_MODEL, n * (A_WIDTH * 18 + 4 * B_WIDTH + 8 * D_MODEL)),
        name="even_out",
    )(*outs, *lses, gu, gu, gu, x, w_out, conv_w, g, b)


def _gla_gates(z, lb, tri2):
    f = lb + (1.0 - lb) * _sigmoid(z)
    lf2 = jnp.log(f) * LOG2_E
    hi = lf2.astype(BF16)
    lo = (lf2 - hi.astype(F32)).astype(BF16)
    cum = jnp.dot(tri2, jnp.concatenate([hi, lo], axis=0), preferred_element_type=F32)
    return 1.0 - f, cum


def _gla_scores(q, kk, cum, *, reverse):
    c = GLA_CHUNK
    sb = GLA_SUB
    dk = q.shape[1]
    nt = (((1,), (1,)), ((), ()))
    ti = lax.broadcasted_iota(jnp.int32, (c, c), 0)
    si = lax.broadcasted_iota(jnp.int32, (c, c), 1)
    causal = (si >= ti) if reverse else (ti >= si)

    parts = []
    for blk in range(c // sb):
        rows = slice(blk * sb, (blk + 1) * sb)
        if reverse:
            edge = (blk + 1) * sb
            ref = cum[edge:edge + 1, :] if edge < c else jnp.zeros((1, dk), F32)
            other = slice((blk + 1) * sb, c)
        else:
            edge = blk * sb - 1
            ref = cum[edge:edge + 1, :] if edge >= 0 else jnp.zeros((1, dk), F32)
            other = slice(0, blk * sb)
        qs = q[rows] * jnp.exp2(cum[rows] - ref)
        k_own = (kk[rows] * jnp.exp2(jnp.minimum(ref - cum[rows], GLA_EXP2_CLAMP))).astype(BF16)
        pieces = [k_own]
        n_other = other.stop - other.start
        if n_other:
            k_other = (kk[other] * jnp.exp2(ref - cum[other])).astype(BF16)
            pieces = [k_own, k_other] if reverse else [k_other, k_own]
        if n_other + sb < c:
            pad = jnp.zeros((c - n_other - sb, dk), BF16)
            pieces = [pad] + pieces if reverse else pieces + [pad]
        ks = jnp.concatenate(pieces, axis=0) if len(pieces) > 1 else pieces[0]
        parts.append(lax.dot_general(qs.astype(BF16), ks, nt, preferred_element_type=F32))
    return jnp.where(causal, jnp.concatenate(parts, axis=0), 0.0).astype(BF16)


def _gla_state_terms(q, kk, cum, *, reverse):
    c = GLA_CHUNK
    total = cum[0:1, :] if reverse else cum[c - 1:c, :]
    qe = (q * jnp.exp2(cum)).astype(BF16)
    kd = (kk * jnp.exp2(total - cum)).astype(BF16)
    return qe, kd, jnp.exp2(total)


def _gla_kernel(q_ref, zf_ref, zb_ref, v_ref, g_ref, lbl_ref, ng_ref, o_ref,
                acc_f, acc_b, qe_f, qe_b, kd_f, kd_b, et_f, et_b, att_f, att_b, ring_a, ring_b, vt_s,
                *, seq):
    c = GLA_CHUNK
    nc = seq // c
    dk = HG_DIM
    nt = (((1,), (1,)), ((), ()))
    l0 = lbl_ref[0:1, :]
    l1 = lbl_ref[1:2, :]
    mx = jnp.maximum(l0, l1)
    e0, e1 = jnp.exp(l0 - mx), jnp.exp(l1 - mx)
    lb = e0 / (e0 + e1)

    ti = lax.broadcasted_iota(jnp.int32, (c, 2 * c), 0)
    si = lax.broadcasted_iota(jnp.int32, (c, 2 * c), 1) & (c - 1)
    tri_f = jnp.where(ti >= si, 1.0, 0.0).astype(BF16)
    tri_b = jnp.where(si >= ti, 1.0, 0.0).astype(BF16)

    dirs = ((zf_ref, tri_f, att_f, qe_f, kd_f, et_f, False),
            (zb_ref, tri_b, att_b, qe_b, kd_b, et_b, True))
    grp = GLA_INTRA_UNROLL
    per_step = 2 * grp
    n_steps = nc // per_step
    chains = [(u, d) for u in range(grp) for d in range(2)]

    def chunk_of(step, half, u, d):
        ci = step * per_step + half * grp + u
        return (nc - 1 - ci) if d else ci

    def park_gates(step, half, ring):
        for k, (u, d) in enumerate(chains):
            rows = pl.ds(pl.multiple_of(chunk_of(step, half, u, d) * c, c), c)
            kk, cum = _gla_gates(dirs[d][0][0, rows, :].astype(F32), lb, dirs[d][1])
            ring[0, k * c:(k + 1) * c, :] = kk
            ring[1, k * c:(k + 1) * c, :] = cum

    def scores_from_ring(step, half, ring):
        for k, (u, d) in enumerate(chains):
            ci = chunk_of(step, half, u, d)
            rows = pl.ds(pl.multiple_of(ci * c, c), c)
            q = q_ref[0, rows, :].astype(F32)
            kk = ring[0, k * c:(k + 1) * c, :]
            cum = ring[1, k * c:(k + 1) * c, :]
            dirs[d][2][rows, :] = _gla_scores(q, kk, cum, reverse=dirs[d][6])
            qe, kd, et = _gla_state_terms(q, kk, cum, reverse=dirs[d][6])
            dirs[d][3][rows, :] = qe
            dirs[d][4][rows, :] = kd
            dirs[d][5][pl.ds(ci, 1), :] = et
            vt_s[ci] = v_ref[0, rows, :].astype(F32).T.astype(BF16)

    def intra(step):
        scores_from_ring(step, 0, ring_a)
        scores_from_ring(step, 1, ring_b)
        nxt = jnp.minimum(step + 1, n_steps - 1)
        park_gates(nxt, 0, ring_a)
        park_gates(nxt, 1, ring_b)

    def scan(j, carry):
        sf, sr = carry
        cfs = [j * per_step + u for u in range(per_step)]
        crs = [nc - 1 - cf for cf in cfs]
        rfs = [pl.ds(pl.multiple_of(cf * c, c), c) for cf in cfs]
        rrs = [pl.ds(pl.multiple_of(cr * c, c), c) for cr in crs]
        upd_f = [jnp.dot(vt_s[cf], kd_f[r, :], preferred_element_type=F32) for cf, r in zip(cfs, rfs)]
        upd_r = [jnp.dot(vt_s[cr], kd_b[r, :], preferred_element_type=F32) for cr, r in zip(crs, rrs)]
        loc_f = [jnp.dot(att_f[r, :], v_ref[0, r, :], preferred_element_type=F32) for r in rfs]
        loc_r = [jnp.dot(att_b[r, :], v_ref[0, r, :], preferred_element_type=F32) for r in rrs]
        sfs, srs = [sf], [sr]
        for u in range(per_step):
            sfs.append(sfs[-1] * et_f[pl.ds(cfs[u], 1), :] + upd_f[u])
            srs.append(srs[-1] * et_b[pl.ds(crs[u], 1), :] + upd_r[u])
        for u in range(per_step):
            acc_f[rfs[u], :] = loc_f[u] + lax.dot_general(qe_f[rfs[u], :], sfs[u].astype(BF16), nt,
                                                          preferred_element_type=F32)
            acc_b[rrs[u], :] = loc_r[u] + lax.dot_general(qe_b[rrs[u], :], srs[u].astype(BF16), nt,
                                                          preferred_element_type=F32)
        return sfs[-1], srs[-1]

    park_gates(0, 0, ring_a)
    park_gates(0, 1, ring_b)
    intra(0)

    def step(j, carry):
        carry = scan(j - 1, carry)
        intra(j)
        return carry

    zero = jnp.zeros((dk, dk), F32)
    carry = lax.fori_loop(1, n_steps, step, (zero, zero))
    scan(n_steps - 1, carry)

    blk = 512
    ng = ng_ref[...]

    def fin(j, _):
        rows = pl.ds(pl.multiple_of(j * blk, blk), blk)
        o = acc_f[rows, :] + acc_b[rows, :]
        o = o * lax.rsqrt(jnp.mean(o * o, axis=-1, keepdims=True) + RMS_EPS) * ng
        g = g_ref[0, rows, :].astype(F32)
        o_ref[0, rows, :] = (o * (g * _sigmoid(g))).astype(o_ref.dtype)
        return 0

    lax.fori_loop(0, seq // blk, fin, 0)


def _gla(proj, lb_logits, norm_g, batch, seq):
    view = proj.reshape(batch, seq, 5 * D_MODEL)

    def col(seg):
        return pl.BlockSpec((1, seq, HG_DIM), lambda b, h: (b, 0, seg * HG_HEADS + h))

    return pl.pallas_call(
        functools.partial(_gla_kernel, seq=seq),
        grid=(batch, HG_HEADS),
        in_specs=[col(0), col(1), col(2), col(3), col(4),
                  pl.BlockSpec((DEPTH, HG_DIM), lambda b, h: (0, h)),
                  pl.BlockSpec((1, HG_DIM), lambda b, h: (0, h))],
        out_specs=pl.BlockSpec((1, seq, HG_DIM), lambda b, h: (b, 0, h)),
        out_shape=jax.ShapeDtypeStruct((batch, seq, D_MODEL), BF16),
        scratch_shapes=[pltpu.VMEM((seq, HG_DIM), F32), pltpu.VMEM((seq, HG_DIM), F32),
                        pltpu.VMEM((seq, HG_DIM), BF16), pltpu.VMEM((seq, HG_DIM), BF16),
                        pltpu.VMEM((seq, HG_DIM), BF16), pltpu.VMEM((seq, HG_DIM), BF16),
                        pltpu.VMEM((seq // GLA_CHUNK, HG_DIM), F32),
                        pltpu.VMEM((seq // GLA_CHUNK, HG_DIM), F32),
                        pltpu.VMEM((seq, GLA_CHUNK), BF16), pltpu.VMEM((seq, GLA_CHUNK), BF16),
                        pltpu.VMEM((2, 2 * GLA_INTRA_UNROLL * GLA_CHUNK, HG_DIM), F32),
                        pltpu.VMEM((2, 2 * GLA_INTRA_UNROLL * GLA_CHUNK, HG_DIM), F32),
                        pltpu.VMEM((seq // GLA_CHUNK, HG_DIM, GLA_CHUNK), BF16)],
        compiler_params=_cparams("parallel", "parallel"),
        cost_estimate=_cost(batch * seq * HG_HEADS * 2 * 2 * HG_DIM * (5 * GLA_CHUNK + 2 * HG_DIM) // 2,
                            batch * seq * D_MODEL * 12, batch * seq * D_MODEL * 16),
        name="gla",
    )(view, view, view, view, view, lb_logits, norm_g).reshape(batch * seq, D_MODEL)


def _proj_ln_kernel(a_ref, w_ref, x_ref, g_ref, b_ref, y_ref):
    mix = jnp.dot(a_ref[...], w_ref[...], preferred_element_type=F32)
    y_ref[...] = _layer_norm(DEEPNORM_ALPHA * x_ref[...] + mix, g_ref[...], b_ref[...])


def _proj_ln(a, w, x, g, b, tm):
    n = x.shape[0]
    full = lambda t: pl.BlockSpec(t.shape, lambda i: (0, 0))
    row = pl.BlockSpec((tm, D_MODEL), lambda i: (i, 0))
    return pl.pallas_call(
        _proj_ln_kernel,
        grid=(n // tm,),
        in_specs=[row, full(w), row, full(g), full(b)],
        out_specs=row,
        out_shape=jax.ShapeDtypeStruct((n, D_MODEL), F32),
        compiler_params=_cparams("parallel"),
        cost_estimate=_cost(2 * n * D_MODEL * D_MODEL, n * D_MODEL * 10),
        name="proj_ln",
    )(a, w, x, g, b)


ROUTER_ROWS = 64
ROUTE_OUT_ROWS = 8


def _first_index_of(vals, target, n_rows):
    idx = lax.broadcasted_iota(jnp.int32, vals.shape, 0)
    return jnp.min(jnp.where(vals == target, idx, n_rows), axis=0, keepdims=True)


def _xattn_kernel(x_ref, wq_ref, kv_ref, wo_ref, g_ref, b_ref, wr_ref, br_ref, y_ref, yb_ref, route_ref):
    x = x_ref[...]
    q = jnp.dot(x.astype(BF16), wq_ref[...], preferred_element_type=F32) * (XA_HEAD_DIM ** -0.5)
    qb = q.astype(BF16)
    nt = (((1,), (1,)), ((), ()))
    heads = []
    for h in range(XA_HEADS):
        cols = slice(h * XA_HEAD_DIM, (h + 1) * XA_HEAD_DIM)
        k = kv_ref[0, :, cols]
        v = kv_ref[0, :, D_MODEL + h * XA_HEAD_DIM:D_MODEL + (h + 1) * XA_HEAD_DIM]
        sc = lax.dot_general(qb[:, cols], k, nt, preferred_element_type=F32)
        m = jnp.max(sc, axis=-1, keepdims=True)
        pe = jnp.exp(sc - m)
        p = pe / jnp.sum(pe, axis=-1, keepdims=True)
        heads.append(jnp.dot(p.astype(BF16), v, preferred_element_type=F32).astype(BF16))
    o = jnp.concatenate(heads, axis=1)
    xa = jnp.dot(o, wo_ref[...], preferred_element_type=F32)
    y = _layer_norm(DEEPNORM_ALPHA * x + xa, g_ref[...], b_ref[...])
    y_ref[...] = y
    yb_ref[...] = _pack_bf16_pair(y)

    lg = lax.dot_general(wr_ref[...], y.astype(BF16), nt, preferred_element_type=F32) + br_ref[:, 0:1]
    gl = lg[0:N_GROUPS, :]
    gmax = jnp.max(gl, axis=0, keepdims=True)
    g_w = 1.0 / jnp.sum(jnp.exp(gl - gmax), axis=0, keepdims=True)
    g_sel = _first_index_of(gl, gmax, N_GROUPS)
    el = jnp.zeros((EXPERTS_PER_GROUP, gl.shape[1]), F32)
    for grp in range(N_GROUPS):
        rows = slice(8 + grp * EXPERTS_PER_GROUP, 8 + (grp + 1) * EXPERTS_PER_GROUP)
        el = el + jnp.where(g_sel == grp, lg[rows, :], 0.0)
    m1 = jnp.max(el, axis=0, keepdims=True)
    i1 = _first_index_of(el, m1, EXPERTS_PER_GROUP)
    eidx = lax.broadcasted_iota(jnp.int32, el.shape, 0)
    el2 = jnp.where(eidx == i1, -jnp.inf, el)
    m2 = jnp.max(el2, axis=0, keepdims=True)
    i2 = _first_index_of(el2, m2, EXPERTS_PER_GROUP)
    e2 = jnp.exp(m2 - m1)
    den = 1.0 + e2
    w1 = g_w / den
    w2 = g_w * e2 / den
    base = g_sel * EXPERTS_PER_GROUP
    zero = jnp.zeros_like(w1)
    route_ref[...] = jnp.concatenate(
        [(base + i1).astype(F32), (base + i2).astype(F32), w1, w2, zero, zero, zero, zero], axis=0)


def _xattn(x, kv, wq, wo, g, b, wr, br, seq, tm):
    n = x.shape[0]
    spt = seq // tm
    full = lambda t: pl.BlockSpec(t.shape, lambda i: (0, 0))
    row = pl.BlockSpec((tm, D_MODEL), lambda i: (i, 0))
    return pl.pallas_call(
        _xattn_kernel,
        grid=(n // tm,),
        in_specs=[row, full(wq),
                  pl.BlockSpec((1,) + kv.shape[1:], lambda i: (i // spt, 0, 0)),
                  full(wo), full(g), full(b), full(wr), full(br)],
        out_specs=[row, pl.BlockSpec((tm, D_MODEL // 2), lambda i: (i, 0)),
                   pl.BlockSpec((ROUTE_OUT_ROWS, tm), lambda i: (0, i))],
        out_shape=[jax.ShapeDtypeStruct((n, D_MODEL), F32),
                   jax.ShapeDtypeStruct((n, D_MODEL // 2), jnp.int32),
                   jax.ShapeDtypeStruct((ROUTE_OUT_ROWS, n), F32)],
        compiler_params=_cparams("parallel"),
        cost_estimate=_cost(n * (4 * D_MODEL * D_MODEL + 4 * D_MODEL * kv.shape[1] + 2 * ROUTER_ROWS * D_MODEL),
                            n * D_MODEL * 10, n * XA_HEADS * kv.shape[1]),
        name="xattn_router",
    )(x, wq, kv, wo, g, b, wr, br)


def _expert_kernel(te_ref, nu_ref, xs_ref, w1_ref, w3_ref, w2_ref, ys_ref):
    j = pl.program_id(0)

    @pl.when(j < nu_ref[0])
    def _():
        half = D_MODEL // 2
        x_hi, x_lo = _unpack_bf16_pair(xs_ref[...])
        x_hi, x_lo = x_hi.astype(BF16), x_lo.astype(BF16)

        def up(w_ref):
            return (jnp.dot(x_hi, w_ref[0, 0:half, :].astype(BF16), preferred_element_type=F32)
                    + jnp.dot(x_lo, w_ref[0, half:D_MODEL, :].astype(BF16), preferred_element_type=F32))

        h1 = up(w1_ref)
        hid = h1 * _sigmoid(h1) * up(w3_ref)
        ys_ref[...] = _pack_bf16_pair(
            jnp.dot(hid.astype(BF16), w2_ref[0].astype(BF16), preferred_element_type=F32))

    @pl.when(j >= nu_ref[0])
    def _():
        ys_ref[...] = jnp.zeros_like(ys_ref)


def _experts(xs, tile_expert, n_used, w1, w3, w2, tm):
    mp = xs.shape[0]
    w_in = pl.BlockSpec((1, D_MODEL, EXPERT_FF), lambda j, te, nu: (te[j], 0, 0))
    rows = pl.BlockSpec((tm, D_MODEL // 2), lambda j, te, nu: (j, 0))
    grid_spec = pltpu.PrefetchScalarGridSpec(
        num_scalar_prefetch=2,
        grid=(mp // tm,),
        in_specs=[rows, w_in, w_in,
                  pl.BlockSpec((1, EXPERT_FF, D_MODEL), lambda j, te, nu: (te[j], 0, 0))],
        out_specs=rows,
    )
    return pl.pallas_call(
        _expert_kernel,
        grid_spec=grid_spec,
        out_shape=jax.ShapeDtypeStruct((mp, D_MODEL // 2), jnp.int32),
        compiler_params=_cparams("arbitrary"),
        cost_estimate=_cost(6 * mp * D_MODEL * EXPERT_FF, mp * D_MODEL * 4 + 12 * N_EXPERTS * D_MODEL * EXPERT_FF,
                            mp * EXPERT_FF),
        name="experts",
    )(tile_expert, n_used, xs, w1, w3, w2)


def _moe_ln_kernel(x_ref, y0_ref, y1_ref, gate_ref, g_ref, b_ref, o_ref):
    g0, g1 = gate_ref[:, 0:1], gate_ref[:, 1:2]
    hi0, lo0 = _unpack_bf16_pair(y0_ref[...])
    hi1, lo1 = _unpack_bf16_pair(y1_ref[...])
    ff = jnp.concatenate([g0 * hi0 + g1 * hi1, g0 * lo0 + g1 * lo1], axis=1)
    o_ref[...] = _layer_norm(DEEPNORM_ALPHA * x_ref[...] + ff, g_ref[...], b_ref[...])


def _moe_ln(x, y01, gates, g, b, tm):
    n = x.shape[0]
    full = lambda t: pl.BlockSpec(t.shape, lambda i: (0, 0))
    row = pl.BlockSpec((tm, D_MODEL), lambda i: (i, 0))
    second = n // tm
    return pl.pallas_call(
        _moe_ln_kernel,
        grid=(n // tm,),
        in_specs=[row,
                  pl.BlockSpec((tm, D_MODEL // 2), lambda i: (i, 0)),
                  pl.BlockSpec((tm, D_MODEL // 2), lambda i: (i + second, 0)),
                  pl.BlockSpec((tm, 2), lambda i: (i, 0)), full(g), full(b)],
        out_specs=row,
        out_shape=jax.ShapeDtypeStruct((n, D_MODEL), F32),
        compiler_params=_cparams("parallel"),
        cost_estimate=_cost(0, n * D_MODEL * 12),
        name="moe_ln",
    )(x, y01, y01, gates, g, b)


SC_CORES = 2
SC_SUBCORES = 16
SC_GATHER_ROWS = 32
SC_GATHER_BUFFERS = 4


def _sc_gather_rows(table, idx):
    n_out = idx.shape[0]
    width = table.shape[1]
    workers = SC_CORES * SC_SUBCORES
    ch = SC_GATHER_ROWS
    per_w = n_out // workers
    steps = per_w // ch
    nb = SC_GATHER_BUFFERS
    assert per_w * workers == n_out and steps * ch == per_w and steps % nb == 0
    mesh = plsc.VectorSubcoreMesh(core_axis_name="c", subcore_axis_name="s")

    @functools.partial(
        pl.kernel, mesh=mesh,
        out_type=jax.ShapeDtypeStruct((n_out, width), table.dtype),
        cost_estimate=_cost(0, n_out * (2 * width * table.dtype.itemsize + 4)),
        scratch_types=[pltpu.VMEM((per_w,), jnp.int32)]
                      + [pltpu.VMEM((ch, width), table.dtype)] * nb
                      + [pltpu.SemaphoreType.DMA] * (2 * nb),
    )
    def gather_kernel(table_hbm, idx_hbm, out_hbm, idx_v, *scratch):
        rows, gather_sems, write_sems = scratch[:nb], scratch[nb:2 * nb], scratch[2 * nb:]
        wid = lax.axis_index("s") * SC_CORES + lax.axis_index("c")
        base = wid * per_w
        pltpu.sync_copy(idx_hbm.at[pl.ds(base, per_w)], idx_v)

        def gather(i, b):
            return pltpu.make_async_copy(table_hbm.at[idx_v.at[pl.ds(i * ch, ch)]], rows[b], gather_sems[b])

        def write(i, b):
            return pltpu.make_async_copy(rows[b], out_hbm.at[pl.ds(base + i * ch, ch)], write_sems[b])

        for b in range(nb - 1):
            gather(b, b).start()

        @pl.loop(0, steps, step=nb)
        def _(i):
            for b in range(nb):
                ii = i + b
                gather(ii, b).wait()
                write(ii, b).start()
                prev = (b + nb - 1) % nb

                @pl.when(ii >= 1)
                def _():
                    write(ii - 1, prev).wait()

                @pl.when(ii + nb - 1 < steps)
                def _():
                    gather(ii + nb - 1, prev).start()

        write(steps - 1, (steps - 1) % nb).wait()

    return gather_kernel(table, idx)


def _dispatch_plan(ids, n, tm_e):
    n_asg = 2 * n
    mp = n_asg + N_EXPERTS * tm_e
    order = jnp.argsort(ids, stable=True).astype(jnp.int32)
    pos = jnp.argsort(order).astype(jnp.int32)
    onehot = (ids[:, None] == jnp.arange(N_EXPERTS, dtype=jnp.int32)[None, :]).astype(jnp.int32)
    counts = jnp.sum(onehot, axis=0)
    dense_start = jnp.cumsum(counts) - counts
    padded = ((counts + tm_e - 1) // tm_e) * tm_e
    row_end = jnp.cumsum(padded)
    row_start = row_end - padded
    row_of_asg = pos + jnp.sum(onehot * (row_start - dense_start)[None, :], axis=1)
    tile_start = jnp.arange(mp // tm_e, dtype=jnp.int32) * tm_e
    tile_expert = jnp.minimum(jnp.sum((tile_start[:, None] >= row_end[None, :]).astype(jnp.int32), axis=1),
                              N_EXPERTS - 1)
    shift = (dense_start - row_start)[tile_expert]
    src = (tile_start + shift)[:, None] + jnp.arange(tm_e, dtype=jnp.int32)[None, :]
    asg_of_row = order[jnp.clip(src.reshape(mp), 0, n_asg - 1)]
    tok_of_row = jnp.where(asg_of_row >= n, asg_of_row - n, asg_of_row)
    n_used = (row_end[-1] // tm_e).astype(jnp.int32).reshape(1)
    return tok_of_row, row_of_asg, tile_expert.astype(jnp.int32), n_used


def _moe_steps(x, xp, route, layer, w1, w3, w2, g, b, tm_e, tm, baton):
    n = x.shape[0]
    ids = route[0:2].astype(jnp.int32).reshape(2 * n)
    tok_of_row, row_of_asg, tile_expert, n_used = _dispatch_plan(ids, n, tm_e)
    if baton is not None and not baton["lead"] and baton["rows"]:
        tok_of_row, _ = lax.optimization_barrier((tok_of_row, baton["rows"].pop()))
    xs = _sc_gather_rows(xp, tok_of_row)
    if baton is not None and baton["lead"]:
        baton["box"].append(tok_of_row)
        baton["rows"].append(xs)
    yield
    ys = _experts(xs, tile_expert + layer * N_EXPERTS, n_used, w1, w3, w2, tm_e)
    y01 = _sc_gather_rows(ys, row_of_asg)
    yield
    return _moe_ln(x, y01, route[2:4].T, g, b, tm)


def _router_weights(w_group, b_group, w_expert, b_expert):
    wr = jnp.zeros((ROUTER_ROWS, D_MODEL), F32)
    wr = wr.at[0:N_GROUPS].set(w_group.T).at[8:8 + N_EXPERTS].set(w_expert.T)
    br = jnp.zeros((ROUTER_ROWS, LANES), F32)
    br = br.at[0:N_GROUPS, :].set(b_group[:, None]).at[8:8 + N_EXPERTS, :].set(b_expert[:, None])
    return wr.astype(BF16), br


def _trunk_steps(x3, mem3, p, baton=None):
    batch, seq, _ = x3.shape
    n = batch * seq
    x = x3.reshape(n, D_MODEL)
    mem = mem3.reshape(batch * mem3.shape[1], D_MODEL)
    tm = 512
    tables = _rope_tables(seq)
    for layer in range(DEPTH):
        j = layer // 2
        row = lambda a: a.reshape(1, D_MODEL)
        if baton is not None and not baton["lead"] and baton["box"]:
            x, _ = lax.optimization_barrier((x, baton["box"].pop()))
        if layer % 2 == 0:
            *views, gu = _even_proj(x, p["ev_w_in"][j], tables, seq, tm)
            res = [_band_attention(v, batch, seq, dil) for v, (_, dil) in zip(views, DILATED_PATTERNS)]
            x = _even_out([o for o, _ in res], [l for _, l in res], gu, x, p["ev_w_out"][j],
                          p["ev_conv_w"][j], row(p["ln_g"][layer, 0]), row(p["ln_b"][layer, 0]), seq, tm)
        else:
            proj = _matmul(x, p["od_w_in"][j], BF16, tm, D_MODEL)
            o = _gla(proj, p["lb_logits"], p["od_norm_g"][j].reshape(1, D_MODEL), batch, seq)
            x = _proj_ln(o, p["od_w_out"][j], x, row(p["ln_g"][layer, 0]), row(p["ln_b"][layer, 0]), tm)
        kv = _matmul(mem, p["xa_w_kv"][layer], BF16, 256, D_MODEL).reshape(batch, mem3.shape[1], 2 * D_MODEL)
        wr, br = _router_weights(p["moe_w_group"][layer], p["moe_b_group"][layer],
                                 p["moe_w_expert"][layer], p["moe_b_expert"][layer])
        x, xp, route = _xattn(x, kv, p["xa_w_q"][layer], p["xa_w_out"][layer],
                              row(p["ln_g"][layer, 1]), row(p["ln_b"][layer, 1]), wr, br, seq, tm)
        x = yield from _moe_steps(x, xp, route, layer, p["moe_w1"], p["moe_w3"], p["moe_w2"],
                                  row(p["ln_g"][layer, 2]), row(p["ln_b"][layer, 2]), 512, tm, baton)
    return x.reshape(batch, seq, D_MODEL)


def _run_interleaved(generators):
    results = [None] * len(generators)
    live = list(range(len(generators)))
    while live:
        for k in list(live):
            try:
                next(generators[k])
            except StopIteration as stop:
                results[k] = stop.value
                live.remove(k)
    return results


def _trunk(x3, mem3, p):
    return _run_interleaved([_trunk_steps(x3, mem3, p)])[0]


def kernel(x_prompt, x_sample, mem_prompt, mem_sample, ev_w_in, ev_conv_w, ev_w_out, od_w_in, lb_logits,
           od_norm_g, od_w_out, xa_w_q, xa_w_kv, xa_w_out, moe_w_group, moe_b_group, moe_w_expert,
           moe_b_expert, moe_w1, moe_w3, moe_w2, ln_g, ln_b):
    ff = moe_w1.shape[-1]
    p = dict(
        ev_w_in=ev_w_in.astype(BF16), ev_conv_w=ev_conv_w, ev_w_out=ev_w_out.astype(BF16),
        od_w_in=od_w_in.astype(BF16), lb_logits=lb_logits, od_norm_g=od_norm_g,
        od_w_out=od_w_out.astype(BF16), xa_w_q=xa_w_q.astype(BF16), xa_w_kv=xa_w_kv.astype(BF16),
        xa_w_out=xa_w_out.astype(BF16), moe_w_group=moe_w_group, moe_b_group=moe_b_group,
        moe_w_expert=moe_w_expert, moe_b_expert=moe_b_expert,
        moe_w1=moe_w1.reshape(DEPTH * N_EXPERTS, D_MODEL, ff),
        moe_w3=moe_w3.reshape(DEPTH * N_EXPERTS, D_MODEL, ff),
        moe_w2=moe_w2.reshape(DEPTH * N_EXPERTS, ff, D_MODEL),
        ln_g=ln_g, ln_b=ln_b)
    box, rows = [], []
    y_prompt, y_sample = _run_interleaved([
        _trunk_steps(x_prompt, mem_prompt, p, dict(lead=True, box=box, rows=rows)),
        _trunk_steps(x_sample, mem_sample, p, dict(lead=False, box=box, rows=rows))])
    return y_prompt, y_sample
```

```python
import functools
import math

import jax
import jax.numpy as jnp
from jax import lax
from jax.experimental import pallas as pl
from jax.experimental.pallas import tpu as pltpu
from jax.experimental.pallas import tpu_sc as plsc

F32 = jnp.float32
BF16 = jnp.bfloat16

D_MODEL = 1024
DEPTH = 2
A_HEADS = 8
A_HEAD_DIM = 64
A_WIDTH = A_HEADS * A_HEAD_DIM
DILATED_PATTERNS = ((128, 1), (512, 4), (2048, 16))
ROPE_THETA = 500000.0
ROPE_DIM = A_HEAD_DIM // 4
B_WIDTH = D_MODEL // 2
CONV_WIDTH = 3
HG_HEADS = 8
HG_DIM = D_MODEL // HG_HEADS
XA_HEADS = 4
XA_HEAD_DIM = D_MODEL // XA_HEADS
N_GROUPS = 4
EXPERTS_PER_GROUP = 8
N_EXPERTS = N_GROUPS * EXPERTS_PER_GROUP
EXPERT_FF = D_MODEL // 4
LN_EPS = 1e-5
RMS_EPS = 1e-6
DEEPNORM_ALPHA = (2 * DEPTH) ** 0.25

LANES = 128
BAND_RADIUS = 64
ATTN_QBLOCK = 128
NEG_BIG = -1e30
GLA_CHUNK = 64
GLA_SUB = 16
GLA_EXP2_CLAMP = 100.0
GLA_INTRA_UNROLL = 2
LOG2_E = 1.4426950408889634
VMEM_LIMIT = 56 * 1024 * 1024


def _cost(flops, nbytes, transcendentals=0):
    return pl.CostEstimate(flops=int(flops), transcendentals=int(transcendentals), bytes_accessed=int(nbytes))


def _cparams(*sem):
    return pltpu.CompilerParams(dimension_semantics=sem, vmem_limit_bytes=VMEM_LIMIT)


def _layer_norm(y, g, b):
    mu = jnp.mean(y, axis=-1, keepdims=True)
    d = y - mu
    var = jnp.mean(d * d, axis=-1, keepdims=True)
    return d * lax.rsqrt(var + LN_EPS) * g + b


def _sigmoid(z):
    return 1.0 / (1.0 + jnp.exp(-z))


def _pack_bf16_pair(y):
    w = y.shape[1] // 2
    hi = lax.bitcast_convert_type(y[:, :w].astype(BF16).astype(F32), jnp.int32)
    lo = lax.bitcast_convert_type(y[:, w:].astype(BF16).astype(F32), jnp.int32)
    return hi | lax.shift_right_logical(lo, 16)


def _unpack_bf16_pair(p):
    hi = lax.bitcast_convert_type(p & jnp.int32(-65536), F32)
    lo = lax.bitcast_convert_type(lax.shift_left(p, 16), F32)
    return hi, lo


def _mm_kernel(x_ref, w_ref, o_ref, *, chunk):
    xb = x_ref[...].astype(BF16)
    for c in range(w_ref.shape[1] // chunk):
        cols = slice(c * chunk, (c + 1) * chunk)
        o_ref[:, cols] = jnp.dot(xb, w_ref[:, cols], preferred_element_type=F32).astype(o_ref.dtype)


def _matmul(x, w, out_dtype, tm, chunk):
    n, k = x.shape
    m = w.shape[1]
    return pl.pallas_call(
        functools.partial(_mm_kernel, chunk=chunk),
        grid=(n // tm,),
        in_specs=[pl.BlockSpec((tm, k), lambda i: (i, 0)), pl.BlockSpec((k, m), lambda i: (0, 0))],
        out_specs=pl.BlockSpec((tm, m), lambda i: (i, 0)),
        out_shape=jax.ShapeDtypeStruct((n, m), out_dtype),
        compiler_params=_cparams("parallel"),
        cost_estimate=_cost(2 * n * k * m, n * k * x.dtype.itemsize + 2 * k * m + n * m * 2),
        name="matmul",
    )(x, w)


def _rope_tables(seq):
    half = ROPE_DIM // 2
    inv_freq = jnp.exp(-math.log(ROPE_THETA) * jnp.arange(half, dtype=F32) * (2.0 / ROPE_DIM))
    ang = jnp.arange(seq, dtype=F32)[:, None] * inv_freq[None, :]
    cos, sin = jnp.cos(ang), jnp.sin(ang)
    ones = jnp.ones((seq, A_HEAD_DIM - ROPE_DIM), F32)
    zeros = jnp.zeros((seq, A_HEAD_DIM - ROPE_DIM), F32)
    zh = jnp.zeros((seq, half), F32)
    c = jnp.concatenate([cos, cos, ones], -1)
    s_up = jnp.concatenate([-sin, zh, zeros], -1)
    s_dn = jnp.concatenate([zh, sin, zeros], -1)
    rep = LANES // A_HEAD_DIM
    return tuple(jnp.tile(t, (1, rep)) for t in (c, s_up, s_dn))


def _even_proj_kernel(x_ref, w_ref, c_ref, su_ref, sd_ref, qkv_ref, qkv4_ref, qkv16_ref, gu_ref, slab_ref):
    tm = x_ref.shape[0]
    xb = x_ref[...].astype(BF16)
    rep = A_WIDTH // LANES
    half = ROPE_DIM // 2
    c = jnp.tile(c_ref[...], (1, rep))
    su = jnp.tile(su_ref[...], (1, rep))
    sd = jnp.tile(sd_ref[...], (1, rep))

    def proj(j):
        return jnp.dot(xb, w_ref[:, j * A_WIDTH:(j + 1) * A_WIDTH], preferred_element_type=F32)

    def rope(t):
        up = pltpu.roll(t, A_WIDTH - half, axis=1)
        dn = pltpu.roll(t, half, axis=1)
        return t * c + up * su + dn * sd

    qkv = (rope(proj(0)) * (A_HEAD_DIM ** -0.5), rope(proj(1)), proj(2))
    per = A_WIDTH // LANES
    for j, part in enumerate(qkv):
        qkv_ref[:, j * A_WIDTH:(j + 1) * A_WIDTH] = part.astype(BF16)
        for s in range(per):
            slab_ref[j * per + s] = part[:, s * LANES:(s + 1) * LANES]
    for dil, out_ref in ((DILATED_PATTERNS[1][1], qkv4_ref), (DILATED_PATTERNS[2][1], qkv16_ref)):
        for r in range(dil):
            for s in range(3 * per):
                val = slab_ref[s, pl.ds(r, tm // dil, stride=dil), :]
                col = r * 3 * A_WIDTH + s * LANES
                out_ref[:, col:col + LANES] = val.astype(BF16)
    gu_ref[:, 0:B_WIDTH] = proj(3).astype(BF16)
    gu_ref[:, B_WIDTH:2 * B_WIDTH] = (proj(4) * proj(5)).astype(BF16)


def _even_proj(x, w_in, tables, seq, tm):
    n = x.shape[0]
    spt = seq // tm
    tab_spec = pl.BlockSpec((tm, LANES), lambda i: (i % spt, 0))
    d4, d16 = DILATED_PATTERNS[1][1], DILATED_PATTERNS[2][1]
    width = 3 * A_WIDTH
    return pl.pallas_call(
        _even_proj_kernel,
        grid=(n // tm,),
        in_specs=[pl.BlockSpec((tm, D_MODEL), lambda i: (i, 0)),
                  pl.BlockSpec(w_in.shape, lambda i: (0, 0)),
                  tab_spec, tab_spec, tab_spec],
        out_specs=[pl.BlockSpec((tm, width), lambda i: (i, 0)),
                   pl.BlockSpec((tm // d4, d4 * width), lambda i: (i, 0)),
                   pl.BlockSpec((tm // d16, d16 * width), lambda i: (i, 0)),
                   pl.BlockSpec((tm, 2 * B_WIDTH), lambda i: (i, 0))],
        out_shape=[jax.ShapeDtypeStruct((n, width), BF16),
                   jax.ShapeDtypeStruct((n // d4, d4 * width), BF16),
                   jax.ShapeDtypeStruct((n // d16, d16 * width), BF16),
                   jax.ShapeDtypeStruct((n, 2 * B_WIDTH), BF16)],
        scratch_shapes=[pltpu.VMEM((width // LANES, tm, LANES), F32)],
        compiler_params=_cparams("parallel"),
        cost_estimate=_cost(2 * n * D_MODEL * 6 * A_WIDTH, n * (4 * D_MODEL + 2 * (9 * A_WIDTH + 2 * B_WIDTH))),
        name="even_proj",
    )(x, w_in, *tables)


def _band_attn_kernel(q_ref, kp_ref, km_ref, kn_ref, vp_ref, vm_ref, vn_ref, o_ref, lse_ref,
                      kbuf, vbuf, *, tq, length):
    i = pl.program_id(2)
    r = BAND_RADIUS
    kbuf[0:r] = kp_ref[0]
    kbuf[r:r + tq] = km_ref[0]
    kbuf[r + tq:r + tq + r] = kn_ref[0]
    vbuf[0:r] = vp_ref[0]
    vbuf[r:r + tq] = vm_ref[0]
    vbuf[r + tq:r + tq + r] = vn_ref[0]

    qb = ATTN_QBLOCK
    kw = qb + 2 * r
    qi = lax.broadcasted_iota(jnp.int32, (qb, kw), 0)
    kj = lax.broadcasted_iota(jnp.int32, (qb, kw), 1)
    rel = kj - qi
    band = (rel >= 0) & (rel <= 2 * r)
    lane = lax.broadcasted_iota(jnp.int32, (qb, LANES), 1)
    low = lane < A_HEAD_DIM
    nt = (((1,), (1,)), ((), ()))

    for s in range(tq // qb):
        kpos = i * tq + (s * qb - r) + kj
        valid = band & (kpos >= 0) & (kpos < length)
        bias = jnp.where(valid, 0.0, NEG_BIG)
        rows = slice(s * qb, (s + 1) * qb)
        wrows = slice(s * qb, s * qb + kw)
        for p in range(A_WIDTH // LANES):
            cols = slice(p * LANES, (p + 1) * LANES)
            qp = q_ref[0, rows, cols]
            kwin = kbuf[wrows, cols]
            vwin = vbuf[wrows, cols]
            outs, lses = [], []
            for sel in (low, jnp.logical_not(low)):
                qm = jnp.where(sel, qp, jnp.zeros_like(qp))
                sc = lax.dot_general(qm, kwin, nt, preferred_element_type=F32) + bias
                m = jnp.max(sc, axis=-1, keepdims=True)
                pe = jnp.exp(sc - m)
                l = jnp.sum(pe, axis=-1, keepdims=True)
                pv = jnp.dot(pe.astype(BF16), vwin, preferred_element_type=F32)
                outs.append(pv / l)
                lses.append(jnp.broadcast_to(m + jnp.log(l), (qb, LANES)))
            o_ref[0, rows, cols] = jnp.where(low, outs[0], outs[1]).astype(o_ref.dtype)
            lse_ref[0, rows, cols] = jnp.where(low, lses[0], lses[1])


def _band_attention(qkv_view, batch, seq, dil):
    length = seq // dil
    tq = min(512, length)
    r = BAND_RADIUS
    view = qkv_view.reshape(batch, length, dil * 3 * A_WIDTH)
    nblk_h = length // r
    per = tq // r

    def main(j):
        return pl.BlockSpec((1, tq, A_WIDTH), lambda b, rr, i: (b, i, rr * 3 + j))

    def prev(j):
        return pl.BlockSpec((1, r, A_WIDTH), lambda b, rr, i: (b, jnp.maximum(i * per - 1, 0), rr * 3 + j))

    def nxt(j):
        return pl.BlockSpec((1, r, A_WIDTH),
                            lambda b, rr, i: (b, jnp.minimum((i + 1) * per, nblk_h - 1), rr * 3 + j))

    out_spec = pl.BlockSpec((1, tq, A_WIDTH), lambda b, rr, i: (b, i, rr))
    o, lse = pl.pallas_call(
        functools.partial(_band_attn_kernel, tq=tq, length=length),
        grid=(batch, dil, length // tq),
        in_specs=[main(0), prev(1), main(1), nxt(1), prev(2), main(2), nxt(2)],
        out_specs=[out_spec, out_spec],
        out_shape=[jax.ShapeDtypeStruct((batch, length, dil * A_WIDTH), BF16),
                   jax.ShapeDtypeStruct((batch, length, dil * A_WIDTH), F32)],
        scratch_shapes=[pltpu.VMEM((tq + 2 * r, A_WIDTH), BF16), pltpu.VMEM((tq + 2 * r, A_WIDTH), BF16)],
        compiler_params=_cparams("parallel", "parallel", "parallel"),
        cost_estimate=_cost(batch * seq * A_HEADS * 8 * (ATTN_QBLOCK + 2 * BAND_RADIUS) * LANES,
                            batch * seq * A_WIDTH * (2 * 4 + 2 + 4),
                            batch * seq * A_HEADS * (ATTN_QBLOCK + 2 * BAND_RADIUS)),
        name=f"band_attn_d{dil}",
    )(view, view, view, view, view, view, view)
    return o.reshape(batch * length, dil * A_WIDTH), lse.reshape(batch * length, dil * A_WIDTH)


def _even_out_kernel(o1, o4, o16, l1, l4, l16, gu_ref, up_ref, un_ref, x_ref, w_ref, cw_ref, g_ref, b_ref,
                     y_ref, so4, sl4, so16, sl16, *, tm, seq):
    i = pl.program_id(0)
    pos = (i * tm) % seq
    per = A_WIDTH // LANES
    for dil, o_ref, l_ref, so, sl in ((DILATED_PATTERNS[1][1], o4, l4, so4, sl4),
                                      (DILATED_PATTERNS[2][1], o16, l16, so16, sl16)):
        for r in range(dil):
            for s in range(per):
                cols = slice(r * A_WIDTH + s * LANES, r * A_WIDTH + (s + 1) * LANES)
                so[s, pl.ds(r, tm // dil, stride=dil), :] = o_ref[:, cols].astype(F32)
                sl[s, pl.ds(r, tm // dil, stride=dil), :] = l_ref[:, cols]
    slabs = []
    for s in range(per):
        cols = slice(s * LANES, (s + 1) * LANES)
        la, lb, lc = l1[:, cols], sl4[s], sl16[s]
        mx = jnp.maximum(jnp.maximum(la, lb), lc)
        ea, eb, ec = jnp.exp(la - mx), jnp.exp(lb - mx), jnp.exp(lc - mx)
        num = ea * o1[:, cols].astype(F32) + eb * so4[s] + ec * so16[s]
        slabs.append(num / (ea + eb + ec))
    a_out = jnp.concatenate(slabs, axis=1)

    gate_b = gu_ref[:, 0:B_WIDTH].astype(F32)
    u = gu_ref[:, B_WIDTH:2 * B_WIDTH].astype(F32)
    hrows = up_ref.shape[0]
    u_before = jnp.where(pos > 0, up_ref[hrows - 1:hrows, :].astype(F32), 0.0)
    u_after = jnp.where(pos + tm < seq, un_ref[0:1, :].astype(F32), 0.0)
    row = lax.broadcasted_iota(jnp.int32, (tm, B_WIDTH), 0)
    u_prev = jnp.where(row == 0, u_before, pltpu.roll(u, 1, axis=0))
    u_next = jnp.where(row == tm - 1, u_after, pltpu.roll(u, tm - 1, axis=0))
    conv = u_prev * cw_ref[0:1, :] + u * cw_ref[1:2, :] + u_next * cw_ref[2:3, :]
    b_out = gate_b * conv

    mix = jnp.dot(a_out.astype(BF16), w_ref[0:A_WIDTH, :], preferred_element_type=F32)
    mix = mix + jnp.dot(b_out.astype(BF16), w_ref[A_WIDTH:A_WIDTH + B_WIDTH, :], preferred_element_type=F32)
    y_ref[...] = _layer_norm(DEEPNORM_ALPHA * x_ref[...] + mix, g_ref[...], b_ref[...])


def _even_out(outs, lses, gu, x, w_out, conv_w, g, b, seq, tm):
    n = x.shape[0]
    hrows = 16
    nh = n // hrows
    per = tm // hrows
    d4, d16 = DILATED_PATTERNS[1][1], DILATED_PATTERNS[2][1]
    views = [pl.BlockSpec((tm // d, d * A_WIDTH), lambda i: (i, 0)) for d in (1, d4, d16)]
    full = lambda a: pl.BlockSpec(a.shape, lambda i: (0, 0))
    slab = pltpu.VMEM((A_WIDTH // LANES, tm, LANES), F32)
    return pl.pallas_call(
        functools.partial(_even_out_kernel, tm=tm, seq=seq),
        grid=(n // tm,),
        scratch_shapes=[slab, slab, slab, slab],
        in_specs=views + views + [
            pl.BlockSpec((tm, 2 * B_WIDTH), lambda i: (i, 0)),
            pl.BlockSpec((hrows, B_WIDTH), lambda i: (jnp.maximum(i * per - 1, 0), 1)),
            pl.BlockSpec((hrows, B_WIDTH), lambda i: (jnp.minimum((i + 1) * per, nh - 1), 1)),
            pl.BlockSpec((tm, D_MODEL), lambda i: (i, 0)),
            full(w_out), full(conv_w), full(g), full(b)],
        out_specs=pl.BlockSpec((tm, D_MODEL), lambda i: (i, 0)),
        out_shape=jax.ShapeDtypeStruct((n, D_MODEL), F32),
        compiler_params=_cparams("parallel"),
        cost_estimate=_cost(2 * n * D_MODEL * D_MODEL, n * (A_WIDTH * 18 + 4 * B_WIDTH + 8 * D_MODEL)),
        name="even_out",
    )(*outs, *lses, gu, gu, gu, x, w_out, conv_w, g, b)


def _gla_gates(z, lb, tri2):
    f = lb + (1.0 - lb) * _sigmoid(z)
    lf2 = jnp.log(f) * LOG2_E
    hi = lf2.astype(BF16)
    lo = (lf2 - hi.astype(F32)).astype(BF16)
    cum = jnp.dot(tri2, jnp.concatenate([hi, lo], axis=0), preferred_element_type=F32)
    return 1.0 - f, cum


def _gla_scores(q, kk, cum, *, reverse):
    c = GLA_CHUNK
    sb = GLA_SUB
    dk = q.shape[1]
    nt = (((1,), (1,)), ((), ()))
    ti = lax.broadcasted_iota(jnp.int32, (c, c), 0)
    si = lax.broadcasted_iota(jnp.int32, (c, c), 1)
    causal = (si >= ti) if reverse else (ti >= si)

    parts = []
    for blk in range(c // sb):
        rows = slice(blk * sb, (blk + 1) * sb)
        if reverse:
            edge = (blk + 1) * sb
            ref = cum[edge:edge + 1, :] if edge < c else jnp.zeros((1, dk), F32)
            other = slice((blk + 1) * sb, c)
        else:
            edge = blk * sb - 1
            ref = cum[edge:edge + 1, :] if edge >= 0 else jnp.zeros((1, dk), F32)
            other = slice(0, blk * sb)
        qs = q[rows] * jnp.exp2(cum[rows] - ref)
        k_own = (kk[rows] * jnp.exp2(jnp.minimum(ref - cum[rows], GLA_EXP2_CLAMP))).astype(BF16)
        pieces = [k_own]
        n_other = other.stop - other.start
        if n_other:
            k_other = (kk[other] * jnp.exp2(ref - cum[other])).astype(BF16)
            pieces = [k_own, k_other] if reverse else [k_other, k_own]
        if n_other + sb < c:
            pad = jnp.zeros((c - n_other - sb, dk), BF16)
            pieces = [pad] + pieces if reverse else pieces + [pad]
        ks = jnp.concatenate(pieces, axis=0) if len(pieces) > 1 else pieces[0]
        parts.append(lax.dot_general(qs.astype(BF16), ks, nt, preferred_element_type=F32))
    return jnp.where(causal, jnp.concatenate(parts, axis=0), 0.0).astype(BF16)


def _gla_state_terms(q, kk, cum, *, reverse):
    c = GLA_CHUNK
    total = cum[0:1, :] if reverse else cum[c - 1:c, :]
    qe = (q * jnp.exp2(cum)).astype(BF16)
    kd = (kk * jnp.exp2(total - cum)).astype(BF16)
    return qe, kd, jnp.exp2(total)


def _gla_kernel(q_ref, zf_ref, zb_ref, v_ref, g_ref, lbl_ref, ng_ref, o_ref,
                acc_f, acc_b, qe_f, qe_b, kd_f, kd_b, et_f, et_b, att_f, att_b, ring_a, ring_b, vt_s,
                *, seq):
    c = GLA_CHUNK
    nc = seq // c
    dk = HG_DIM
    nt = (((1,), (1,)), ((), ()))
    l0 = lbl_ref[0:1, :]
    l1 = lbl_ref[1:2, :]
    mx = jnp.maximum(l0, l1)
    e0, e1 = jnp.exp(l0 - mx), jnp.exp(l1 - mx)
    lb = e0 / (e0 + e1)

    ti = lax.broadcasted_iota(jnp.int32, (c, 2 * c), 0)
    si = lax.broadcasted_iota(jnp.int32, (c, 2 * c), 1) & (c - 1)
    tri_f = jnp.where(ti >= si, 1.0, 0.0).astype(BF16)
    tri_b = jnp.where(si >= ti, 1.0, 0.0).astype(BF16)

    dirs = ((zf_ref, tri_f, att_f, qe_f, kd_f, et_f, False),
            (zb_ref, tri_b, att_b, qe_b, kd_b, et_b, True))
    grp = GLA_INTRA_UNROLL
    per_step = 2 * grp
    n_steps = nc // per_step
    chains = [(u, d) for u in range(grp) for d in range(2)]

    def chunk_of(step, half, u, d):
        ci = step * per_step + half * grp + u
        return (nc - 1 - ci) if d else ci

    def park_gates(step, half, ring):
        for k, (u, d) in enumerate(chains):
            rows = pl.ds(pl.multiple_of(chunk_of(step, half, u, d) * c, c), c)
            kk, cum = _gla_gates(dirs[d][0][0, rows, :].astype(F32), lb, dirs[d][1])
            ring[0, k * c:(k + 1) * c, :] = kk
            ring[1, k * c:(k + 1) * c, :] = cum

    def scores_from_ring(step, half, ring):
        for k, (u, d) in enumerate(chains):
            ci = chunk_of(step, half, u, d)
            rows = pl.ds(pl.multiple_of(ci * c, c), c)
            q = q_ref[0, rows, :].astype(F32)
            kk = ring[0, k * c:(k + 1) * c, :]
            cum = ring[1, k * c:(k + 1) * c, :]
            dirs[d][2][rows, :] = _gla_scores(q, kk, cum, reverse=dirs[d][6])
            qe, kd, et = _gla_state_terms(q, kk, cum, reverse=dirs[d][6])
            dirs[d][3][rows, :] = qe
            dirs[d][4][rows, :] = kd
            dirs[d][5][pl.ds(ci, 1), :] = et
            vt_s[ci] = v_ref[0, rows, :].astype(F32).T.astype(BF16)

    def intra(step):
        scores_from_ring(step, 0, ring_a)
        scores_from_ring(step, 1, ring_b)
        nxt = jnp.minimum(step + 1, n_steps - 1)
        park_gates(nxt, 0, ring_a)
        park_gates(nxt, 1, ring_b)

    def scan(j, carry):
        sf, sr = carry
        cfs = [j * per_step + u for u in range(per_step)]
        crs = [nc - 1 - cf for cf in cfs]
        rfs = [pl.ds(pl.multiple_of(cf * c, c), c) for cf in cfs]
        rrs = [pl.ds(pl.multiple_of(cr * c, c), c) for cr in crs]
        upd_f = [jnp.dot(vt_s[cf], kd_f[r, :], preferred_element_type=F32) for cf, r in zip(cfs, rfs)]
        upd_r = [jnp.dot(vt_s[cr], kd_b[r, :], preferred_element_type=F32) for cr, r in zip(crs, rrs)]
        loc_f = [jnp.dot(att_f[r, :], v_ref[0, r, :], preferred_element_type=F32) for r in rfs]
        loc_r = [jnp.dot(att_b[r, :], v_ref[0, r, :], preferred_element_type=F32) for r in rrs]
        sfs, srs = [sf], [sr]
        for u in range(per_step):
            sfs.append(sfs[-1] * et_f[pl.ds(cfs[u], 1), :] + upd_f[u])
            srs.append(srs[-1] * et_b[pl.ds(crs[u], 1), :] + upd_r[u])
        for u in range(per_step):
            acc_f[rfs[u], :] = loc_f[u] + lax.dot_general(qe_f[rfs[u], :], sfs[u].astype(BF16), nt,
                                                          preferred_element_type=F32)
            acc_b[rrs[u], :] = loc_r[u] + lax.dot_general(qe_b[rrs[u], :], srs[u].astype(BF16), nt,
                                                          preferred_element_type=F32)
        return sfs[-1], srs[-1]

    park_gates(0, 0, ring_a)
    park_gates(0, 1, ring_b)
    intra(0)

    def step(j, carry):
        carry = scan(j - 1, carry)
        intra(j)
        return carry

    zero = jnp.zeros((dk, dk), F32)
    carry = lax.fori_loop(1, n_steps, step, (zero, zero))
    scan(n_steps - 1, carry)

    blk = 512
    ng = ng_ref[...]

    def fin(j, _):
        rows = pl.ds(pl.multiple_of(j * blk, blk), blk)
        o = acc_f[rows, :] + acc_b[rows, :]
        o = o * lax.rsqrt(jnp.mean(o * o, axis=-1, keepdims=True) + RMS_EPS) * ng
        g = g_ref[0, rows, :].astype(F32)
        o_ref[0, rows, :] = (o * (g * _sigmoid(g))).astype(o_ref.dtype)
        return 0

    lax.fori_loop(0, seq // blk, fin, 0)


def _gla(proj, lb_logits, norm_g, batch, seq):
    view = proj.reshape(batch, seq, 5 * D_MODEL)

    def col(seg):
        return pl.BlockSpec((1, seq, HG_DIM), lambda b, h: (b, 0, seg * HG_HEADS + h))

    return pl.pallas_call(
        functools.partial(_gla_kernel, seq=seq),
        grid=(batch, HG_HEADS),
        in_specs=[col(0), col(1), col(2), col(3), col(4),
                  pl.BlockSpec((DEPTH, HG_DIM), lambda b, h: (0, h)),
                  pl.BlockSpec((1, HG_DIM), lambda b, h: (0, h))],
        out_specs=pl.BlockSpec((1, seq, HG_DIM), lambda b, h: (b, 0, h)),
        out_shape=jax.ShapeDtypeStruct((batch, seq, D_MODEL), BF16),
        scratch_shapes=[pltpu.VMEM((seq, HG_DIM), F32), pltpu.VMEM((seq, HG_DIM), F32),
                        pltpu.VMEM((seq, HG_DIM), BF16), pltpu.VMEM((seq, HG_DIM), BF16),
                        pltpu.VMEM((seq, HG_DIM), BF16), pltpu.VMEM((seq, HG_DIM), BF16),
                        pltpu.VMEM((seq // GLA_CHUNK, HG_DIM), F32),
                        pltpu.VMEM((seq // GLA_CHUNK, HG_DIM), F32),
                        pltpu.VMEM((seq, GLA_CHUNK), BF16), pltpu.VMEM((seq, GLA_CHUNK), BF16),
                        pltpu.VMEM((2, 2 * GLA_INTRA_UNROLL * GLA_CHUNK, HG_DIM), F32),
                        pltpu.VMEM((2, 2 * GLA_INTRA_UNROLL * GLA_CHUNK, HG_DIM), F32),
                        pltpu.VMEM((seq // GLA_CHUNK, HG_DIM, GLA_CHUNK), BF16)],
        compiler_params=_cparams("parallel", "parallel"),
        cost_estimate=_cost(batch * seq * HG_HEADS * 2 * 2 * HG_DIM * (5 * GLA_CHUNK + 2 * HG_DIM) // 2,
                            batch * seq * D_MODEL * 12, batch * seq * D_MODEL * 16),
        name="gla",
    )(view, view, view, view, view, lb_logits, norm_g).reshape(batch * seq, D_MODEL)


def _proj_ln_kernel(a_ref, w_ref, x_ref, g_ref, b_ref, y_ref):
    mix = jnp.dot(a_ref[...], w_ref[...], preferred_element_type=F32)
    y_ref[...] = _layer_norm(DEEPNORM_ALPHA * x_ref[...] + mix, g_ref[...], b_ref[...])


def _proj_ln(a, w, x, g, b, tm):
    n = x.shape[0]
    full = lambda t: pl.BlockSpec(t.shape, lambda i: (0, 0))
    row = pl.BlockSpec((tm, D_MODEL), lambda i: (i, 0))
    return pl.pallas_call(
        _proj_ln_kernel,
        grid=(n // tm,),
        in_specs=[row, full(w), row, full(g), full(b)],
        out_specs=row,
        out_shape=jax.ShapeDtypeStruct((n, D_MODEL), F32),
        compiler_params=_cparams("parallel"),
        cost_estimate=_cost(2 * n * D_MODEL * D_MODEL, n * D_MODEL * 10),
        name="proj_ln",
    )(a, w, x, g, b)


ROUTER_ROWS = 64
ROUTE_OUT_ROWS = 8


def _first_index_of(vals, target, n_rows):
    idx = lax.broadcasted_iota(jnp.int32, vals.shape, 0)
    return jnp.min(jnp.where(vals == target, idx, n_rows), axis=0, keepdims=True)


def _xattn_kernel(x_ref, wq_ref, kv_ref, wo_ref, g_ref, b_ref, wr_ref, br_ref, y_ref, yb_ref, route_ref):
    x = x_ref[...]
    q = jnp.dot(x.astype(BF16), wq_ref[...], preferred_element_type=F32) * (XA_HEAD_DIM ** -0.5)
    qb = q.astype(BF16)
    nt = (((1,), (1,)), ((), ()))
    heads = []
    for h in range(XA_HEADS):
        cols = slice(h * XA_HEAD_DIM, (h + 1) * XA_HEAD_DIM)
        k = kv_ref[0, :, cols]
        v = kv_ref[0, :, D_MODEL + h * XA_HEAD_DIM:D_MODEL + (h + 1) * XA_HEAD_DIM]
        sc = lax.dot_general(qb[:, cols], k, nt, preferred_element_type=F32)
        m = jnp.max(sc, axis=-1, keepdims=True)
        pe = jnp.exp(sc - m)
        p = pe / jnp.sum(pe, axis=-1, keepdims=True)
        heads.append(jnp.dot(p.astype(BF16), v, preferred_element_type=F32).astype(BF16))
    o = jnp.concatenate(heads, axis=1)
    xa = jnp.dot(o, wo_ref[...], preferred_element_type=F32)
    y = _layer_norm(DEEPNORM_ALPHA * x + xa, g_ref[...], b_ref[...])
    y_ref[...] = y
    yb_ref[...] = _pack_bf16_pair(y)

    lg = lax.dot_general(wr_ref[...], y.astype(BF16), nt, preferred_element_type=F32) + br_ref[:, 0:1]
    gl = lg[0:N_GROUPS, :]
    gmax = jnp.max(gl, axis=0, keepdims=True)
    g_w = 1.0 / jnp.sum(jnp.exp(gl - gmax), axis=0, keepdims=True)
    g_sel = _first_index_of(gl, gmax, N_GROUPS)
    el = jnp.zeros((EXPERTS_PER_GROUP, gl.shape[1]), F32)
    for grp in range(N_GROUPS):
        rows = slice(8 + grp * EXPERTS_PER_GROUP, 8 + (grp + 1) * EXPERTS_PER_GROUP)
        el = el + jnp.where(g_sel == grp, lg[rows, :], 0.0)
    m1 = jnp.max(el, axis=0, keepdims=True)
    i1 = _first_index_of(el, m1, EXPERTS_PER_GROUP)
    eidx = lax.broadcasted_iota(jnp.int32, el.shape, 0)
    el2 = jnp.where(eidx == i1, -jnp.inf, el)
    m2 = jnp.max(el2, axis=0, keepdims=True)
    i2 = _first_index_of(el2, m2, EXPERTS_PER_GROUP)
    e2 = jnp.exp(m2 - m1)
    den = 1.0 + e2
    w1 = g_w / den
    w2 = g_w * e2 / den
    base = g_sel * EXPERTS_PER_GROUP
    zero = jnp.zeros_like(w1)
    route_ref[...] = jnp.concatenate(
        [(base + i1).astype(F32), (base + i2).astype(F32), w1, w2, zero, zero, zero, zero], axis=0)


def _xattn(x, kv, wq, wo, g, b, wr, br, seq, tm):
    n = x.shape[0]
    spt = seq // tm
    full = lambda t: pl.BlockSpec(t.shape, lambda i: (0, 0))
    row = pl.BlockSpec((tm, D_MODEL), lambda i: (i, 0))
    return pl.pallas_call(
        _xattn_kernel,
        grid=(n // tm,),
        in_specs=[row, full(wq),
                  pl.BlockSpec((1,) + kv.shape[1:], lambda i: (i // spt, 0, 0)),
                  full(wo), full(g), full(b), full(wr), full(br)],
        out_specs=[row, pl.BlockSpec((tm, D_MODEL // 2), lambda i: (i, 0)),
                   pl.BlockSpec((ROUTE_OUT_ROWS, tm), lambda i: (0, i))],
        out_shape=[jax.ShapeDtypeStruct((n, D_MODEL), F32),
                   jax.ShapeDtypeStruct((n, D_MODEL // 2), jnp.int32),
                   jax.ShapeDtypeStruct((ROUTE_OUT_ROWS, n), F32)],
        compiler_params=_cparams("parallel"),
        cost_estimate=_cost(n * (4 * D_MODEL * D_MODEL + 4 * D_MODEL * kv.shape[1] + 2 * ROUTER_ROWS * D_MODEL),
                            n * D_MODEL * 10, n * XA_HEADS * kv.shape[1]),
        name="xattn_router",
    )(x, wq, kv, wo, g, b, wr, br)


def _expert_kernel(te_ref, nu_ref, xs_ref, w1_ref, w3_ref, w2_ref, ys_ref):
    j = pl.program_id(0)

    @pl.when(j < nu_ref[0])
    def _():
        half = D_MODEL // 2
        x_hi, x_lo = _unpack_bf16_pair(xs_ref[...])
        x_hi, x_lo = x_hi.astype(BF16), x_lo.astype(BF16)

        def up(w_ref):
            return (jnp.dot(x_hi, w_ref[0, 0:half, :].astype(BF16), preferred_element_type=F32)
                    + jnp.dot(x_lo, w_ref[0, half:D_MODEL, :].astype(BF16), preferred_element_type=F32))

        h1 = up(w1_ref)
        hid = h1 * _sigmoid(h1) * up(w3_ref)
        ys_ref[...] = _pack_bf16_pair(
            jnp.dot(hid.astype(BF16), w2_ref[0].astype(BF16), preferred_element_type=F32))

    @pl.when(j >= nu_ref[0])
    def _():
        ys_ref[...] = jnp.zeros_like(ys_ref)


def _experts(xs, tile_expert, n_used, w1, w3, w2, tm):
    mp = xs.shape[0]
    w_in = pl.BlockSpec((1, D_MODEL, EXPERT_FF), lambda j, te, nu: (te[j], 0, 0))
    rows = pl.BlockSpec((tm, D_MODEL // 2), lambda j, te, nu: (j, 0))
    grid_spec = pltpu.PrefetchScalarGridSpec(
        num_scalar_prefetch=2,
        grid=(mp // tm,),
        in_specs=[rows, w_in, w_in,
                  pl.BlockSpec((1, EXPERT_FF, D_MODEL), lambda j, te, nu: (te[j], 0, 0))],
        out_specs=rows,
    )
    return pl.pallas_call(
        _expert_kernel,
        grid_spec=grid_spec,
        out_shape=jax.ShapeDtypeStruct((mp, D_MODEL // 2), jnp.int32),
        compiler_params=_cparams("arbitrary"),
        cost_estimate=_cost(6 * mp * D_MODEL * EXPERT_FF, mp * D_MODEL * 4 + 12 * N_EXPERTS * D_MODEL * EXPERT_FF,
                            mp * EXPERT_FF),
        name="experts",
    )(tile_expert, n_used, xs, w1, w3, w2)


def _moe_ln_kernel(x_ref, y0_ref, y1_ref, gate_ref, g_ref, b_ref, o_ref):
    g0, g1 = gate_ref[:, 0:1], gate_ref[:, 1:2]
    hi0, lo0 = _unpack_bf16_pair(y0_ref[...])
    hi1, lo1 = _unpack_bf16_pair(y1_ref[...])
    ff = jnp.concatenate([g0 * hi0 + g1 * hi1, g0 * lo0 + g1 * lo1], axis=1)
    o_ref[...] = _layer_norm(DEEPNORM_ALPHA * x_ref[...] + ff, g_ref[...], b_ref[...])


def _moe_ln(x, y01, gates, g, b, tm):
    n = x.shape[0]
    full = lambda t: pl.BlockSpec(t.shape, lambda i: (0, 0))
    row = pl.BlockSpec((tm, D_MODEL), lambda i: (i, 0))
    second = n // tm
    return pl.pallas_call(
        _moe_ln_kernel,
        grid=(n // tm,),
        in_specs=[row,
                  pl.BlockSpec((tm, D_MODEL // 2), lambda i: (i, 0)),
                  pl.BlockSpec((tm, D_MODEL // 2), lambda i: (i + second, 0)),
                  pl.BlockSpec((tm, 2), lambda i: (i, 0)), full(g), full(b)],
        out_specs=row,
        out_shape=jax.ShapeDtypeStruct((n, D_MODEL), F32),
        compiler_params=_cparams("parallel"),
        cost_estimate=_cost(0, n * D_MODEL * 12),
        name="moe_ln",
    )(x, y01, y01, gates, g, b)


SC_CORES = 2
SC_SUBCORES = 16
SC_GATHER_ROWS = 32
SC_GATHER_BUFFERS = 4


def _sc_gather_rows(table, idx):
    n_out = idx.shape[0]
    width = table.shape[1]
    workers = SC_CORES * SC_SUBCORES
    ch = SC_GATHER_ROWS
    per_w = n_out // workers
    steps = per_w // ch
    nb = SC_GATHER_BUFFERS
    assert per_w * workers == n_out and steps * ch == per_w and steps % nb == 0
    mesh = plsc.VectorSubcoreMesh(core_axis_name="c", subcore_axis_name="s")

    @functools.partial(
        pl.kernel, mesh=mesh,
        out_type=jax.ShapeDtypeStruct((n_out, width), table.dtype),
        cost_estimate=_cost(0, n_out * (2 * width * table.dtype.itemsize + 4)),
        scratch_types=[pltpu.VMEM((per_w,), jnp.int32)]
                      + [pltpu.VMEM((ch, width), table.dtype)] * nb
                      + [pltpu.SemaphoreType.DMA] * (2 * nb),
    )
    def gather_kernel(table_hbm, idx_hbm, out_hbm, idx_v, *scratch):
        rows, gather_sems, write_sems = scratch[:nb], scratch[nb:2 * nb], scratch[2 * nb:]
        wid = lax.axis_index("s") * SC_CORES + lax.axis_index("c")
        base = wid * per_w
        pltpu.sync_copy(idx_hbm.at[pl.ds(base, per_w)], idx_v)

        def gather(i, b):
            return pltpu.make_async_copy(table_hbm.at[idx_v.at[pl.ds(i * ch, ch)]], rows[b], gather_sems[b])

        def write(i, b):
            return pltpu.make_async_copy(rows[b], out_hbm.at[pl.ds(base + i * ch, ch)], write_sems[b])

        for b in range(nb - 1):
            gather(b, b).start()

        @pl.loop(0, steps, step=nb)
        def _(i):
            for b in range(nb):
                ii = i + b
                gather(ii, b).wait()
                write(ii, b).start()
                prev = (b + nb - 1) % nb

                @pl.when(ii >= 1)
                def _():
                    write(ii - 1, prev).wait()

                @pl.when(ii + nb - 1 < steps)
                def _():
                    gather(ii + nb - 1, prev).start()

        write(steps - 1, (steps - 1) % nb).wait()

    return gather_kernel(table, idx)


def _dispatch_plan(ids, n, tm_e):
    n_asg = 2 * n
    mp = n_asg + N_EXPERTS * tm_e
    order = jnp.argsort(ids, stable=True).astype(jnp.int32)
    pos = jnp.argsort(order).astype(jnp.int32)
    onehot = (ids[:, None] == jnp.arange(N_EXPERTS, dtype=jnp.int32)[None, :]).astype(jnp.int32)
    counts = jnp.sum(onehot, axis=0)
    dense_start = jnp.cumsum(counts) - counts
    padded = ((counts + tm_e - 1) // tm_e) * tm_e
    row_end = jnp.cumsum(padded)
    row_start = row_end - padded
    row_of_asg = pos + jnp.sum(onehot * (row_start - dense_start)[None, :], axis=1)
    tile_start = jnp.arange(mp // tm_e, dtype=jnp.int32) * tm_e
    tile_expert = jnp.minimum(jnp.sum((tile_start[:, None] >= row_end[None, :]).astype(jnp.int32), axis=1),
                              N_EXPERTS - 1)
    shift = (dense_start - row_start)[tile_expert]
    src = (tile_start + shift)[:, None] + jnp.arange(tm_e, dtype=jnp.int32)[None, :]
    asg_of_row = order[jnp.clip(src.reshape(mp), 0, n_asg - 1)]
    tok_of_row = jnp.where(asg_of_row >= n, asg_of_row - n, asg_of_row)
    n_used = (row_end[-1] // tm_e).astype(jnp.int32).reshape(1)
    return tok_of_row, row_of_asg, tile_expert.astype(jnp.int32), n_used


def _moe_steps(x, xp, route, layer, w1, w3, w2, g, b, tm_e, tm, baton):
    n = x.shape[0]
    ids = route[0:2].astype(jnp.int32).reshape(2 * n)
    tok_of_row, row_of_asg, tile_expert, n_used = _dispatch_plan(ids, n, tm_e)
    if baton is not None and not baton["lead"] and baton["rows"]:
        tok_of_row, _ = lax.optimization_barrier((tok_of_row, baton["rows"].pop()))
    xs = _sc_gather_rows(xp, tok_of_row)
    if baton is not None and baton["lead"]:
        baton["box"].append(tok_of_row)
        baton["rows"].append(xs)
    if baton is not None and not baton["lead"]:
        baton["rows_follower"].append(xs)
    yield
    ys = _experts(xs, tile_expert + layer * N_EXPERTS, n_used, w1, w3, w2, tm_e)
    if baton is not None and baton["lead"] and baton["rows_follower"]:
        row_of_asg, _ = lax.optimization_barrier((row_of_asg, baton["rows_follower"].pop()))
    y01 = _sc_gather_rows(ys, row_of_asg)
    yield
    return _moe_ln(x, y01, route[2:4].T, g, b, tm)


def _router_weights(w_group, b_group, w_expert, b_expert):
    wr = jnp.zeros((ROUTER_ROWS, D_MODEL), F32)
    wr = wr.at[0:N_GROUPS].set(w_group.T).at[8:8 + N_EXPERTS].set(w_expert.T)
    br = jnp.zeros((ROUTER_ROWS, LANES), F32)
    br = br.at[0:N_GROUPS, :].set(b_group[:, None]).at[8:8 + N_EXPERTS, :].set(b_expert[:, None])
    return wr.astype(BF16), br


def _trunk_steps(x3, mem3, p, baton=None):
    batch, seq, _ = x3.shape
    n = batch * seq
    x = x3.reshape(n, D_MODEL)
    mem = mem3.reshape(batch * mem3.shape[1], D_MODEL)
    tm = 512
    tables = _rope_tables(seq)
    for layer in range(DEPTH):
        j = layer // 2
        row = lambda a: a.reshape(1, D_MODEL)
        if baton is not None and not baton["lead"] and baton["box"]:
            x, _ = lax.optimization_barrier((x, baton["box"].pop()))
        if layer % 2 == 0:
            *views, gu = _even_proj(x, p["ev_w_in"][j], tables, seq, tm)
            res = [_band_attention(v, batch, seq, dil) for v, (_, dil) in zip(views, DILATED_PATTERNS)]
            x = _even_out([o for o, _ in res], [l for _, l in res], gu, x, p["ev_w_out"][j],
                          p["ev_conv_w"][j], row(p["ln_g"][layer, 0]), row(p["ln_b"][layer, 0]), seq, tm)
        else:
            proj = _matmul(x, p["od_w_in"][j], BF16, tm, D_MODEL)
            o = _gla(proj, p["lb_logits"], p["od_norm_g"][j].reshape(1, D_MODEL), batch, seq)
            x = _proj_ln(o, p["od_w_out"][j], x, row(p["ln_g"][layer, 0]), row(p["ln_b"][layer, 0]), tm)
        kv = _matmul(mem, p["xa_w_kv"][layer], BF16, 256, D_MODEL).reshape(batch, mem3.shape[1], 2 * D_MODEL)
        wr, br = _router_weights(p["moe_w_group"][layer], p["moe_b_group"][layer],
                                 p["moe_w_expert"][layer], p["moe_b_expert"][layer])
        x, xp, route = _xattn(x, kv, p["xa_w_q"][layer], p["xa_w_out"][layer],
                              row(p["ln_g"][layer, 1]), row(p["ln_b"][layer, 1]), wr, br, seq, tm)
        x = yield from _moe_steps(x, xp, route, layer, p["moe_w1"], p["moe_w3"], p["moe_w2"],
                                  row(p["ln_g"][layer, 2]), row(p["ln_b"][layer, 2]), 512, tm, baton)
    return x.reshape(batch, seq, D_MODEL)


def _run_interleaved(generators):
    results = [None] * len(generators)
    live = list(range(len(generators)))
    while live:
        for k in list(live):
            try:
                next(generators[k])
            except StopIteration as stop:
                results[k] = stop.value
                live.remove(k)
    return results


def _trunk(x3, mem3, p):
    return _run_interleaved([_trunk_steps(x3, mem3, p)])[0]


def kernel(x_prompt, x_sample, mem_prompt, mem_sample, ev_w_in, ev_conv_w, ev_w_out, od_w_in, lb_logits,
           od_norm_g, od_w_out, xa_w_q, xa_w_kv, xa_w_out, moe_w_group, moe_b_group, moe_w_expert,
           moe_b_expert, moe_w1, moe_w3, moe_w2, ln_g, ln_b):
    ff = moe_w1.shape[-1]
    p = dict(
        ev_w_in=ev_w_in.astype(BF16), ev_conv_w=ev_conv_w, ev_w_out=ev_w_out.astype(BF16),
        od_w_in=od_w_in.astype(BF16), lb_logits=lb_logits, od_norm_g=od_norm_g,
        od_w_out=od_w_out.astype(BF16), xa_w_q=xa_w_q.astype(BF16), xa_w_kv=xa_w_kv.astype(BF16),
        xa_w_out=xa_w_out.astype(BF16), moe_w_group=moe_w_group, moe_b_group=moe_b_group,
        moe_w_expert=moe_w_expert, moe_b_expert=moe_b_expert,
        moe_w1=moe_w1.reshape(DEPTH * N_EXPERTS, D_MODEL, ff),
        moe_w3=moe_w3.reshape(DEPTH * N_EXPERTS, D_MODEL, ff),
        moe_w2=moe_w2.reshape(DEPTH * N_EXPERTS, ff, D_MODEL),
        ln_g=ln_g, ln_b=ln_b)
    shared = dict(box=[], rows=[], rows_follower=[])
    y_prompt, y_sample = _run_interleaved([
        _trunk_steps(x_prompt, mem_prompt, p, dict(lead=True, **shared)),
        _trunk_steps(x_sample, mem_sample, p, dict(lead=False, **shared))])
    return y_prompt, y_sample
```
